```python
import jax, jax.numpy as jnp
from jax import lax
import numpy as np

D_MODEL = 2048
BATCH = 2
SEQ = 8192
DEPTH = 2

GRID_W = 64
CTX_LEN = 256
N_MIXERS = 2
N_RET_LAYERS = (DEPTH + 1) // 2
N_POOL_LAYERS = DEPTH // 2
RET_HEADS = 8
QK_HEAD_DIM = D_MODEL // RET_HEADS
V_HEAD_DIM = 2 * QK_HEAD_DIM
QK_DIM = RET_HEADS * QK_HEAD_DIM
V_DIM = RET_HEADS * V_HEAD_DIM
RET_IN_DIM = 2 * QK_DIM + 2 * V_DIM
RET_CHUNK = 128
ROPE_BASE = 10000.0
POOL_WINDOWS = (2, 4, 8, 16)
POOL_GROUPS = 4
POOL_GROUP_DIM = D_MODEL // POOL_GROUPS
N_EXPERTS = 64
TOP_K = 8
N_EXPERT_GROUPS = 8
TOPK_GROUPS = 4
EXPERT_DIM = 512
SHARED_DIM = 512
ROUTED_SCALE = 2.5
MOE_BLOCK = 128
N_MOD = 6
LN_EPS = 1e-5
DEEPNORM_ALPHA = (2.0 * DEPTH) ** 0.25
DEEPNORM_BETA = (8.0 * DEPTH) ** -0.25

kernel_name = 'hybrid_retention_pool_moe_dit'

f32 = jnp.float32


def layer_norm(x, g, b):
    xf = x.astype(f32)
    mu = xf.mean(-1, keepdims=True)
    var = jnp.square(xf - mu).mean(-1, keepdims=True)
    return ((xf - mu) * lax.rsqrt(var + LN_EPS) * g.astype(f32) + b.astype(f32)).astype(x.dtype)


def _axial_rope(t, row, col):
    n_freq = QK_HEAD_DIM // 4
    freqs = ROPE_BASE ** (-jnp.arange(n_freq, dtype=f32) / n_freq)
    ang = jnp.concatenate([row[:, None] * freqs, col[:, None] * freqs], axis=-1)
    cos, sin = jnp.cos(ang), jnp.sin(ang)
    tp = t.reshape(t.shape[:-1] + (QK_HEAD_DIM // 2, 2))
    t1, t2 = tp[..., 0], tp[..., 1]
    return jnp.stack([t1 * cos - t2 * sin, t1 * sin + t2 * cos], axis=-1).reshape(t.shape)


def _retention_chunkwise(q, k, v, log_decay, s0, include_diag):
    b, h, n, _ = q.shape
    nc = n // RET_CHUNK
    pos = jnp.arange(RET_CHUNK, dtype=f32)
    rel = pos[:, None] - pos[None, :]
    mask = rel >= 0 if include_diag else rel > 0
    decay_in = jnp.where(mask, jnp.exp(log_decay[:, None, None] * jnp.maximum(rel, 0.0)), 0.0)
    q_dec = jnp.exp(log_decay[:, None] * (pos + 1.0))
    k_dec = jnp.exp(log_decay[:, None] * (RET_CHUNK - 1.0 - pos))
    chunk_dec = jnp.exp(log_decay * RET_CHUNK)

    def to_chunks(t):
        return jnp.moveaxis(t.reshape(b, h, nc, RET_CHUNK, t.shape[-1]), 2, 0)

    def step(s, qkv):
        qc, kc, vc = qkv
        scores = jnp.einsum('bhik,bhjk->bhij', qc, kc) * decay_in
        o = (jnp.einsum('bhij,bhjv->bhiv', scores, vc)
             + jnp.einsum('bhik,bhkv->bhiv', qc * q_dec[:, :, None], s))
        s = s * chunk_dec[:, None, None] + jnp.einsum('bhjk,bhjv->bhkv', kc * k_dec[:, :, None], vc)
        return s, o

    s_final, o = lax.scan(step, s0, (to_chunks(q), to_chunks(k), to_chunks(v)))
    o = jnp.moveaxis(o, 0, 2).reshape(b, h, n, v.shape[-1])
    return o, s_final


def _ret_project(h, w_in):
    b, n, _ = h.shape
    p = jnp.einsum('bnd,de->bne', h, w_in).astype(f32)
    q, k, v, g = jnp.split(p, [QK_DIM, 2 * QK_DIM, 2 * QK_DIM + V_DIM], axis=-1)

    def heads(t, hd):
        return t.reshape(b, n, RET_HEADS, hd).transpose(0, 2, 1, 3)

    return heads(q, QK_HEAD_DIM), heads(k, QK_HEAD_DIM) * QK_HEAD_DIM ** -0.5, heads(v, V_HEAD_DIM), g


def _ret_output(o, g, w_out):
    mu = o.mean(-1, keepdims=True)
    var = jnp.square(o - mu).mean(-1, keepdims=True)
    o = (o - mu) * lax.rsqrt(var + LN_EPS)
    b, h, n, dv = o.shape
    o = o.transpose(0, 2, 1, 3).reshape(b, n, h * dv)
    return jnp.einsum('bne,ed->bnd', (jax.nn.silu(g) * o).astype(w_out.dtype), w_out)


def retention_mixer(hl, hc, w_in, w_out, raw_fwd, raw_bwd, row, col, with_ctx_out):
    ld_fwd = jax.nn.log_sigmoid(raw_fwd.astype(f32))
    ld_bwd = jax.nn.log_sigmoid(raw_bwd.astype(f32))
    ql, kl, vl, gl = _ret_project(hl, w_in)
    ql = _axial_rope(ql, row, col)
    kl = _axial_rope(kl, row, col)
    qc, kc, vc, gc = _ret_project(hc, w_in)
    s0 = jnp.zeros((hl.shape[0], RET_HEADS, QK_HEAD_DIM, V_HEAD_DIM), f32)

    def flip(t):
        return jnp.flip(t, axis=2)

    oc_f, sc_f = _retention_chunkwise(qc, kc, vc, ld_fwd, s0, True)
    oc_b, sc_b = _retention_chunkwise(flip(qc), flip(kc), flip(vc), ld_bwd, s0, False)
    ol_f, _ = _retention_chunkwise(ql, kl, vl, ld_fwd, sc_f, True)
    ol_b, _ = _retention_chunkwise(flip(ql), flip(kl), flip(vl), ld_bwd, sc_b, False)
    yl = _ret_output(ol_f + flip(ol_b), gl, w_out).astype(hl.dtype)
    yc = _ret_output(oc_f + flip(oc_b), gc, w_out).astype(hc.dtype) if with_ctx_out else None
    return yl, yc


def _centred_mean_minus_self(t, window):
    n = t.shape[-2]
    tf = t.astype(f32)
    cs = jnp.concatenate([jnp.zeros_like(tf[..., :1, :]), jnp.cumsum(tf, axis=-2)], axis=-2)
    idx = jnp.arange(n)
    lo = jnp.clip(idx - window // 2, 0, n)
    hi = jnp.clip(idx + window // 2, 0, n)
    total = jnp.take(cs, hi, axis=-2) - jnp.take(cs, lo, axis=-2)
    return total / (hi - lo).astype(f32)[:, None] - tf


def pool_mixer(h, w_pool, scale):
    parts = [_centred_mean_minus_self(h[..., g * POOL_GROUP_DIM:(g + 1) * POOL_GROUP_DIM], POOL_WINDOWS[g])
             for g in range(POOL_GROUPS)]
    p = jnp.stack(parts, axis=-2).astype(h.dtype)
    y = jnp.einsum('...gc,gce->...ge', p, w_pool)
    return y.reshape(h.shape) * scale


def _swiglu(h, w_gate, w_up, w_down):
    return jnp.dot(jax.nn.silu(jnp.dot(h, w_gate)) * jnp.dot(h, w_up), w_down)


def moe_ffn(h, router_w, router_b, w_gate, w_up, w_down, sh_gate, sh_up, sh_down):
    n_tok, dm = h.shape
    scores = jax.nn.sigmoid(jnp.dot(h.astype(f32), router_w.astype(f32)))
    biased = scores + router_b.astype(f32)
    grouped = biased.reshape(n_tok, N_EXPERT_GROUPS, N_EXPERTS // N_EXPERT_GROUPS)
    group_score = lax.top_k(grouped, 2)[0].sum(-1)
    _, gidx = lax.top_k(group_score, TOPK_GROUPS)
    gmask = jax.nn.one_hot(gidx, N_EXPERT_GROUPS, dtype=f32).sum(1) > 0
    emask = jnp.repeat(gmask, N_EXPERTS // N_EXPERT_GROUPS, axis=1)
    _, eidx = lax.top_k(jnp.where(emask, biased, -jnp.inf), TOP_K)
    gate = jnp.take_along_axis(scores, eidx, axis=-1)
    gate = gate / gate.sum(-1, keepdims=True) * ROUTED_SCALE

    n_assign = n_tok * TOP_K
    e_flat = eidx.reshape(n_assign)
    tok_flat = jnp.repeat(jnp.arange(n_tok, dtype=jnp.int32), TOP_K)
    g_flat = gate.reshape(n_assign)
    order = jnp.argsort(e_flat)
    e_s, tok_s, g_s = e_flat[order], tok_flat[order], g_flat[order]
    counts = jnp.bincount(e_flat, length=N_EXPERTS)
    padded = (counts + MOE_BLOCK - 1) // MOE_BLOCK * MOE_BLOCK
    grp_start = jnp.cumsum(counts) - counts
    pad_end = jnp.cumsum(padded)
    pad_start = pad_end - padded
    slot = pad_start[e_s] + jnp.arange(n_assign, dtype=jnp.int32) - grp_start[e_s]
    n_slots = n_assign + N_EXPERTS * MOE_BLOCK
    n_blocks = n_slots // MOE_BLOCK
    tok_buf = jnp.full((n_slots,), n_tok, jnp.int32).at[slot].set(tok_s)
    g_buf = jnp.zeros((n_slots,), f32).at[slot].set(g_s)
    blk_expert = jnp.minimum(jnp.searchsorted(pad_end, jnp.arange(n_blocks) * MOE_BLOCK, side='right'),
                             N_EXPERTS - 1)
    h_pad = jnp.concatenate([h, jnp.zeros((1, dm), h.dtype)], axis=0)

    def block(y, inp):
        tok_b, g_b, e_b = inp
        a = _swiglu(h_pad[tok_b], w_gate[e_b], w_up[e_b], w_down[e_b])
        return y.at[tok_b].add((a * g_b[:, None]).astype(y.dtype)), None

    y, _ = lax.scan(block, jnp.zeros((n_tok + 1, dm), h.dtype),
                    (tok_buf.reshape(n_blocks, MOE_BLOCK), g_buf.reshape(n_blocks, MOE_BLOCK), blk_expert))
    return y[:n_tok] + _swiglu(h, sh_gate, sh_up, sh_down).astype(h.dtype)


def setup_inputs(seed: int = 0) -> dict:
    key = jax.random.key(seed)
    ks = jax.random.split(key, 24)

    def nrm(k, shape, s):
        return jax.random.normal(k, shape, f32) * s

    dm = D_MODEL
    beta = DEEPNORM_BETA
    col_scale = jnp.concatenate([jnp.ones((2 * QK_DIM,), f32), jnp.full((V_DIM,), beta, f32),
                                 jnp.ones((V_DIM,), f32)])
    decay_base = jnp.log(2.0 ** (5.0 + jnp.arange(RET_HEADS, dtype=f32)) - 1.0)
    return {
        'x': nrm(ks[0], (BATCH, SEQ, dm), 1.0),
        'c': nrm(ks[1], (BATCH, dm), 1.0),
        'ctx': nrm(ks[2], (BATCH, CTX_LEN, dm), 1.0),
        'c_ctx': nrm(ks[3], (dm,), 1.0),
        'ada_w': nrm(ks[4], (DEPTH, dm, N_MOD * dm), 0.5 * dm ** -0.5),
        'ada_b': nrm(ks[5], (DEPTH, N_MOD * dm), 0.01),
        'ln_mix_g': 1.0 + nrm(ks[6], (DEPTH, dm), 0.02),
        'ln_mix_b': nrm(ks[7], (DEPTH, dm), 0.02),
        'ln_ffn_g': 1.0 + nrm(ks[8], (DEPTH, dm), 0.02),
        'ln_ffn_b': nrm(ks[9], (DEPTH, dm), 0.02),
        'ret_w_in': nrm(ks[10], (N_RET_LAYERS, dm, RET_IN_DIM), dm ** -0.5) * col_scale,
        'ret_w_out': nrm(ks[11], (N_RET_LAYERS, V_DIM, dm), beta * V_DIM ** -0.5),
        'ret_decay_fwd': decay_base + nrm(ks[12], (N_RET_LAYERS, RET_HEADS), 0.1),
        'ret_decay_bwd': decay_base + nrm(ks[13], (N_RET_LAYERS, RET_HEADS), 0.1),
        'pool_w': nrm(ks[14], (N_POOL_LAYERS, POOL_GROUPS, POOL_GROUP_DIM, POOL_GROUP_DIM),
                      beta * POOL_GROUP_DIM ** -0.5),
        'pool_scale': 1.0 + nrm(ks[15], (N_POOL_LAYERS, dm), 0.1),
        'router_w': nrm(ks[16], (DEPTH, dm, N_EXPERTS), dm ** -0.5),
        'router_b': nrm(ks[17], (DEPTH, N_EXPERTS), 0.01),
        'exp_w_gate': nrm(ks[18], (DEPTH, N_EXPERTS, dm, EXPERT_DIM), dm ** -0.5),
        'exp_w_up': nrm(ks[19], (DEPTH, N_EXPERTS, dm, EXPERT_DIM), dm ** -0.5),
        'exp_w_down': nrm(ks[20], (DEPTH, N_EXPERTS, EXPERT_DIM, dm), beta * EXPERT_DIM ** -0.5),
        'sh_w_gate': nrm(ks[21], (DEPTH, dm, SHARED_DIM), dm ** -0.5),
        'sh_w_up': nrm(ks[22], (DEPTH, dm, SHARED_DIM), dm ** -0.5),
        'sh_w_down': nrm(ks[23], (DEPTH, SHARED_DIM, dm), beta * SHARED_DIM ** -0.5),
    }


def reference(x, c, ctx, c_ctx, ada_w, ada_b, ln_mix_g, ln_mix_b, ln_ffn_g, ln_ffn_b,
              ret_w_in, ret_w_out, ret_decay_fwd, ret_decay_bwd, pool_w, pool_scale,
              router_w, router_b, exp_w_gate, exp_w_up, exp_w_down, sh_w_gate, sh_w_up, sh_w_down):
    b, n_lat, dm = x.shape
    n_ctx = ctx.shape[1]
    rows = n_lat // GRID_W
    pos = jnp.arange(n_lat, dtype=jnp.int32)
    row = (pos // GRID_W).astype(f32)
    col = (pos % GRID_W).astype(f32)
    silu_c = jax.nn.silu(c.astype(f32))
    silu_cc = jax.nn.silu(c_ctx.astype(f32))
    xl, xc = x, ctx
    for i in range(DEPTH):
        ctx_out = i < DEPTH - 1
        w_a, b_a = ada_w[i].astype(f32), ada_b[i].astype(f32)
        mod_l = (jnp.dot(silu_c, w_a) + b_a).astype(x.dtype)[:, None, :]
        mod_c = (jnp.dot(silu_cc, w_a) + b_a).astype(x.dtype)
        sh1_l, sc1_l, g1_l, sh2_l, sc2_l, g2_l = jnp.split(mod_l, N_MOD, axis=-1)
        sh1_c, sc1_c, g1_c, sh2_c, sc2_c, g2_c = jnp.split(mod_c, N_MOD, axis=-1)

        hl = xl * (1.0 + sc1_l) + sh1_l
        hc = xc * (1.0 + sc1_c) + sh1_c
        j = i // N_MIXERS
        if i % N_MIXERS == 0:
            yl, yc = retention_mixer(hl, hc, ret_w_in[j], ret_w_out[j], ret_decay_fwd[j], ret_decay_bwd[j],
                                     row, col, ctx_out)
        else:
            yl = pool_mixer(hl.reshape(b, rows, GRID_W, dm), pool_w[j], pool_scale[j]).reshape(b, n_lat, dm)
            yc = pool_mixer(hc, pool_w[j], pool_scale[j]) if ctx_out else None
        xl = layer_norm(DEEPNORM_ALPHA * xl + g1_l * yl, ln_mix_g[i], ln_mix_b[i])
        if ctx_out:
            xc = layer_norm(DEEPNORM_ALPHA * xc + g1_c * yc, ln_mix_g[i], ln_mix_b[i])

        h_tok = (xl * (1.0 + sc2_l) + sh2_l).reshape(b * n_lat, dm)
        if ctx_out:
            h_tok = jnp.concatenate([h_tok, (xc * (1.0 + sc2_c) + sh2_c).reshape(b * n_ctx, dm)], axis=0)
        f = moe_ffn(h_tok, router_w[i], router_b[i], exp_w_gate[i], exp_w_up[i], exp_w_down[i],
                    sh_w_gate[i], sh_w_up[i], sh_w_down[i])
        xl = layer_norm(DEEPNORM_ALPHA * xl + g2_l * f[:b * n_lat].reshape(b, n_lat, dm),
                        ln_ffn_g[i], ln_ffn_b[i])
        if ctx_out:
            xc = layer_norm(DEEPNORM_ALPHA * xc + g2_c * f[b * n_lat:].reshape(b, n_ctx, dm),
                            ln_ffn_g[i], ln_ffn_b[i])
    return xl
```

```python
import functools

import jax
import jax.numpy as jnp
from jax import lax
from jax.experimental import pallas as pl
from jax.experimental.pallas import tpu as pltpu

F32 = jnp.float32
BF16 = jnp.bfloat16
HIGHEST = lax.Precision.HIGHEST

DEPTH = 2
GRID_W = 64
RET_HEADS = 8
RET_CHUNK = 128
ROPE_BASE = 10000.0
POOL_WINDOWS = (2, 4, 8, 16)
N_EXPERT_GROUPS = 8
TOPK_GROUPS = 4
TOP_K = 8
ROUTED_SCALE = 2.5
LN_EPS = 1e-5
N_MOD = 6
DEEPNORM_ALPHA = (2.0 * DEPTH) ** 0.25
MOD_SH1, MOD_SC1, MOD_G1, MOD_SH2, MOD_SC2, MOD_G2 = range(N_MOD)

TOKEN_TILE = 256
RET_BLOCK = 768
ROUTER_TILE = 512
MOE_BLOCK = 256
COMBINE_TILE = 128
MOD_ROWS = 8
VMEM_LIMIT = 56 * 1024 * 1024


def _silu(x):
    return x * jax.nn.sigmoid(x)


def _layer_norm(r, g, b):
    mu = jnp.mean(r, axis=-1, keepdims=True)
    var = jnp.mean(jnp.square(r - mu), axis=-1, keepdims=True)
    return (r - mu) * lax.rsqrt(var + LN_EPS) * g + b


class _Order:
    def __init__(self, batch, seq, ctx_len, tile):
        assert seq % tile == 0 and ctx_len % tile == 0
        self.batch, self.seq, self.ctx_len, self.tile = batch, seq, ctx_len, tile
        self.ctx_tiles = ctx_len // tile
        self.lat_tiles = seq // tile
        self.per_batch = self.ctx_tiles + self.lat_tiles
        self.n_tiles = batch * self.per_batch
        self.n_tok = self.n_tiles * tile

    def split(self, i):
        b = i // self.per_batch
        r = i % self.per_batch
        return b, r, r < self.ctx_tiles

    def group(self, i):
        b, _, is_ctx = self.split(i)
        return jnp.where(is_ctx, self.batch, b)

    def lat_block(self, i):
        b, r, _ = self.split(i)
        return b * self.lat_tiles + jnp.maximum(r - self.ctx_tiles, 0)

    def ctx_block(self, i):
        b, r, _ = self.split(i)
        return b * self.ctx_tiles + jnp.minimum(r, max(self.ctx_tiles - 1, 0))


def _mod_kernel(c_ref, w_ref, b_ref, o_ref):
    s = _silu(c_ref[...])
    o_ref[0] = jnp.dot(s, w_ref[0], precision=HIGHEST, preferred_element_type=F32) + b_ref[0]


def _modulation(cvec, ada_w, ada_b):
    depth, d, n = ada_w.shape
    tn = 1024
    return pl.pallas_call(
        _mod_kernel,
        grid=(depth, n // tn),
        in_specs=[pl.BlockSpec((MOD_ROWS, d), lambda l, j: (0, 0)),
                  pl.BlockSpec((1, d, tn), lambda l, j: (l, 0, j)),
                  pl.BlockSpec((1, 1, tn), lambda l, j: (l, 0, j))],
        out_specs=pl.BlockSpec((1, MOD_ROWS, tn), lambda l, j: (l, 0, j)),
        out_shape=jax.ShapeDtypeStruct((depth, MOD_ROWS, n), F32),
        compiler_params=pltpu.CompilerParams(vmem_limit_bytes=VMEM_LIMIT),
    )(cvec, ada_w, ada_b.reshape(depth, 1, n))


def _mod_spec(layer, chunk, d, n_grid_args):
    if n_grid_args == 1:
        return pl.BlockSpec((1, MOD_ROWS, d), lambda i: (layer, 0, chunk))
    return pl.BlockSpec((1, MOD_ROWS, d), lambda j, i: (layer, 0, chunk))


def _inproj_kernel(x_ref, c_ref, sc_ref, sh_ref, w_ref, *rest, order, rope, qk_dim, head_dim):
    i = pl.program_id(1)
    j = pl.program_id(0)
    g = order.group(i)
    _, _, is_ctx = order.split(i)
    sc = sc_ref[0, pl.ds(g, 1), :]
    sh = sh_ref[0, pl.ds(g, 1), :]
    xin = jnp.where(is_ctx, c_ref[...], x_ref[...])
    a = (xin * (1.0 + sc) + sh).astype(BF16)
    acc = jnp.dot(a, w_ref[...], preferred_element_type=F32)
    if not rope:
        (o_ref,) = rest
        o_ref[...] = acc.astype(o_ref.dtype)
        return
    cos_ref, sin_ref, o_ref = rest
    cos = cos_ref[...]
    sin = sin_ref[...]
    tn = acc.shape[1]
    half = head_dim // 2
    scale = jnp.where(j * tn >= qk_dim, head_dim ** -0.5, 1.0).astype(F32)
    for h in range(tn // head_dim):
        t1 = acc[:, h * head_dim:h * head_dim + half]
        t2 = acc[:, h * head_dim + half:(h + 1) * head_dim]
        o_ref[:, h * head_dim:h * head_dim + half] = ((t1 * cos - t2 * sin) * scale).astype(o_ref.dtype)
        o_ref[:, h * head_dim + half:(h + 1) * head_dim] = ((t1 * sin + t2 * cos) * scale).astype(o_ref.dtype)


def _inproj(x2d, ctx2d, mod, w, order, out_dtype, rope_tabs=None, qk_dim=0, head_dim=0):
    d, n = w.shape
    tm = order.tile
    tn = min(n, 2048)
    rope = rope_tabs is not None
    in_specs = [pl.BlockSpec((tm, d), lambda j, i: (order.lat_block(i), 0)),
                pl.BlockSpec((tm, d), lambda j, i: (order.ctx_block(i), 0)),
                _mod_spec(0, MOD_SC1, d, 2), _mod_spec(0, MOD_SH1, d, 2),
                pl.BlockSpec((d, tn), lambda j, i: (0, j))]
    args = [x2d, ctx2d, mod, mod, w]
    if rope:
        def tab_map(j, i):
            _, r, is_ctx = order.split(i)
            return (jnp.where(is_ctx, order.lat_tiles, r - order.ctx_tiles), 0)
        half = head_dim // 2
        in_specs += [pl.BlockSpec((tm, half), tab_map), pl.BlockSpec((tm, half), tab_map)]
        args += list(rope_tabs)
    return pl.pallas_call(
        functools.partial(_inproj_kernel, order=order, rope=rope, qk_dim=qk_dim, head_dim=head_dim),
        grid=(n // tn, order.n_tiles),
        in_specs=in_specs,
        out_specs=pl.BlockSpec((tm, tn), lambda j, i: (i, j)),
        out_shape=jax.ShapeDtypeStruct((order.n_tok, n), out_dtype),
        compiler_params=pltpu.CompilerParams(vmem_limit_bytes=VMEM_LIMIT),
    )(*args)


def _rope_tables(seq, head_dim, tile):
    n_freq = head_dim // 4
    pos = jnp.arange(seq, dtype=jnp.int32)
    row = (pos // GRID_W).astype(F32)
    col = (pos % GRID_W).astype(F32)
    freqs = ROPE_BASE ** (-jnp.arange(n_freq, dtype=F32) / n_freq)
    ang = jnp.concatenate([row[:, None] * freqs, col[:, None] * freqs], axis=-1)
    ident = jnp.zeros((tile, head_dim // 2), F32)
    return (jnp.concatenate([jnp.cos(ang), ident + 1.0], axis=0),
            jnp.concatenate([jnp.sin(ang), ident], axis=0))


def _ret_kernel(q_ref, k_ref, v_ref, g_ref, df_ref, db_ref, o_ref, sf_ref, sb_ref, obuf_ref, *,
                n_blocks, chunks_per_block, ctx_chunks):
    C = RET_CHUNK
    NB, CPB, NCC = n_blocks, chunks_per_block, ctx_chunks
    st = pl.program_id(2)
    phase = st // (NB + 1)
    s = st % (NB + 1)
    ldf = jax.nn.log_sigmoid(df_ref[0])[:, :1]
    ldb = jax.nn.log_sigmoid(db_ref[0])[:, :1]
    pos = lax.broadcasted_iota(jnp.int32, (C, 1), 0).astype(F32)
    tdims = (((0,), (0,)), ((), ()))

    @pl.when(phase == 0)
    def _backward_states():
        @pl.when(s == 0)
        def _():
            sb_ref[...] = jnp.zeros_like(sb_ref)

        blk = jnp.where(s == 0, 0, NB - s)
        lo = jnp.where(s == NB, NCC, 0)
        hi = jnp.where(s == 0, NCC, CPB)
        q_dec = jnp.exp(ldb * (C - pos))
        k_dec = jnp.exp(ldb * pos)
        chunk_dec = jnp.exp(ldb * C)

        def body(t, carry):
            c = hi - 1 - t
            off = pl.multiple_of(c * C, C)
            q = q_ref[pl.ds(off, C), :].astype(F32)
            k = k_ref[pl.ds(off, C), :].astype(F32)
            v = v_ref[pl.ds(off, C), :]
            state = sb_ref[...]
            goff = pl.multiple_of((blk * CPB + c) * C, C)
            obuf_ref[pl.ds(goff, C), :] = jnp.dot((q * q_dec).astype(BF16), state.astype(BF16),
                                                  preferred_element_type=F32)
            sb_ref[...] = state * chunk_dec + lax.dot_general((k * k_dec).astype(BF16), v, tdims,
                                                              preferred_element_type=F32)
            return carry

        lax.fori_loop(0, hi - lo, body, 0)

    @pl.when((phase == 1) & (s < NB))
    def _forward_and_output():
        @pl.when(s == 0)
        def _():
            sf_ref[...] = jnp.zeros_like(sf_ref)

        q_dec = jnp.exp(ldf * (pos + 1.0))
        k_dec = jnp.exp(ldf * (C - 1.0 - pos))
        chunk_dec = jnp.exp(ldf * C)
        ii = lax.broadcasted_iota(jnp.int32, (C, C), 0)
        jj = lax.broadcasted_iota(jnp.int32, (C, C), 1)
        rel = (ii - jj).astype(F32)
        decay = jnp.where(rel >= 0, jnp.exp(ldf * jnp.maximum(rel, 0.0)), jnp.exp(ldb * jnp.maximum(-rel, 0.0)))

        def body(c, carry):
            off = pl.multiple_of(c * C, C)
            qb = q_ref[pl.ds(off, C), :]
            kb = k_ref[pl.ds(off, C), :]
            v = v_ref[pl.ds(off, C), :]
            q = qb.astype(F32)
            k = kb.astype(F32)
            state = sf_ref[...]
            scores = lax.dot_general(qb, kb, (((1,), (1,)), ((), ())), preferred_element_type=F32) * decay
            goff = pl.multiple_of((s * CPB + c) * C, C)
            o = (jnp.dot(scores.astype(BF16), v, preferred_element_type=F32)
                 + jnp.dot((q * q_dec).astype(BF16), state.astype(BF16), preferred_element_type=F32)
                 + obuf_ref[pl.ds(goff, C), :])
            sf_ref[...] = state * chunk_dec + lax.dot_general((k * k_dec).astype(BF16), v, tdims,
                                                              preferred_element_type=F32)
            mu = jnp.mean(o, axis=-1, keepdims=True)
            var = jnp.mean(jnp.square(o - mu), axis=-1, keepdims=True)
            on = (o - mu) * lax.rsqrt(var + LN_EPS)
            o_ref[pl.ds(off, C), :] = (_silu(g_ref[pl.ds(off, C), :]) * on).astype(o_ref.dtype)
            return carry

        lax.fori_loop(0, CPB, body, 0)


def _retention(qk, v, g, raw_fwd, raw_bwd, batch, tokens_per_batch, ctx_len):
    n_tok, qk2 = qk.shape
    v_dim = v.shape[1]
    heads = RET_HEADS
    dk = qk2 // 2 // heads
    dv = v_dim // heads
    tb = RET_BLOCK
    assert tokens_per_batch % tb == 0 and tb % RET_CHUNK == 0 and ctx_len % RET_CHUNK == 0 and ctx_len <= tb
    nb = tokens_per_batch // tb
    steps = 2 * (nb + 1)

    def blk(st):
        phase = st // (nb + 1)
        s = st % (nb + 1)
        back = jnp.where(s == 0, 0, nb - s)
        return jnp.where(phase == 0, back, jnp.minimum(s, nb - 1))

    def fwd_blk(st):
        phase = st // (nb + 1)
        s = st % (nb + 1)
        return jnp.where(phase == 0, 0, jnp.minimum(s, nb - 1))

    lanes = 128
    dec = lambda raw: jnp.broadcast_to(raw.astype(F32)[:, None, None], (heads, 1, lanes))
    return pl.pallas_call(
        functools.partial(_ret_kernel, n_blocks=nb, chunks_per_block=tb // RET_CHUNK,
                          ctx_chunks=ctx_len // RET_CHUNK),
        grid=(batch, heads, steps),
        in_specs=[pl.BlockSpec((tb, dk), lambda b, h, st: (b * nb + blk(st), h)),
                  pl.BlockSpec((tb, dk), lambda b, h, st: (b * nb + blk(st), heads + h)),
                  pl.BlockSpec((tb, dv), lambda b, h, st: (b * nb + blk(st), h)),
                  pl.BlockSpec((tb, dv), lambda b, h, st: (b * nb + fwd_blk(st), h)),
                  pl.BlockSpec((1, 1, lanes), lambda b, h, st: (h, 0, 0)),
                  pl.BlockSpec((1, 1, lanes), lambda b, h, st: (h, 0, 0))],
        out_specs=pl.BlockSpec((tb, dv), lambda b, h, st: (b * nb + fwd_blk(st), h)),
        out_shape=jax.ShapeDtypeStruct((n_tok, v_dim), BF16),
        scratch_shapes=[pltpu.VMEM((dk, dv), F32), pltpu.VMEM((dk, dv), F32),
                        pltpu.VMEM((tokens_per_batch, dv), F32)],
        compiler_params=pltpu.CompilerParams(vmem_limit_bytes=VMEM_LIMIT),
    )(qk, qk, v, g, dec(raw_fwd), dec(raw_bwd))


def _outproj_kernel(o_ref, x_ref, c_ref, w_ref, g1_ref, sc2_ref, sh2_ref, lng_ref, lnb_ref, x1_ref, h2_ref, *, order):
    i = pl.program_id(0)
    g = order.group(i)
    _, _, is_ctx = order.split(i)
    y = jnp.dot(o_ref[...], w_ref[...], preferred_element_type=F32)
    xin = jnp.where(is_ctx, c_ref[...], x_ref[...])
    r = DEEPNORM_ALPHA * xin + g1_ref[0, pl.ds(g, 1), :] * y
    x1 = _layer_norm(r, lng_ref[...], lnb_ref[...])
    x1_ref[...] = x1
    h2_ref[...] = x1 * (1.0 + sc2_ref[0, pl.ds(g, 1), :]) + sh2_ref[0, pl.ds(g, 1), :]


def _outproj(og, x2d, ctx2d, w_out, mod, ln_g, ln_b, order):
    v_dim, d = w_out.shape
    tm = order.tile
    row = pl.BlockSpec((1, d), lambda i: (0, 0))
    tok = pl.BlockSpec((tm, d), lambda i: (i, 0))
    return pl.pallas_call(
        functools.partial(_outproj_kernel, order=order),
        grid=(order.n_tiles,),
        in_specs=[pl.BlockSpec((tm, v_dim), lambda i: (i, 0)),
                  pl.BlockSpec((tm, d), lambda i: (order.lat_block(i), 0)),
                  pl.BlockSpec((tm, d), lambda i: (order.ctx_block(i), 0)),
                  pl.BlockSpec((v_dim, d), lambda i: (0, 0)),
                  _mod_spec(0, MOD_G1, d, 1), _mod_spec(0, MOD_SC2, d, 1), _mod_spec(0, MOD_SH2, d, 1),
                  row, row],
        out_specs=[tok, tok],
        out_shape=[jax.ShapeDtypeStruct((order.n_tok, d), F32)] * 2,
        compiler_params=pltpu.CompilerParams(vmem_limit_bytes=VMEM_LIMIT),
    )(og, x2d, ctx2d, w_out, mod, mod, mod, ln_g.reshape(1, d), ln_b.reshape(1, d))


def _pool_kernel(x_ref, pw_ref, ps_ref, sc1_ref, sh1_ref, g1_ref, sc2_ref, sh2_ref, lng_ref, lnb_ref,
                 x1_ref, h2_ref, y_ref, *, lat_tiles):
    i = pl.program_id(0)
    g = i // lat_tiles
    x = x_ref[...]
    tm, d = x.shape
    gd = d // len(POOL_WINDOWS)
    h = x * (1.0 + sc1_ref[0, pl.ds(g, 1), :]) + sh1_ref[0, pl.ds(g, 1), :]
    ni = lax.broadcasted_iota(jnp.int32, (tm, tm), 0)
    mi = lax.broadcasted_iota(jnp.int32, (tm, tm), 1)
    same_row = (ni // GRID_W) == (mi // GRID_W)
    delta = mi - ni
    col = lax.broadcasted_iota(jnp.int32, (tm, 1), 0) % GRID_W
    for gi, w in enumerate(POOL_WINDOWS):
        window = jnp.where(same_row & (delta >= -(w // 2)) & (delta < w // 2), 1.0, 0.0).astype(F32)
        count = (jnp.minimum(col + w // 2, GRID_W) - jnp.maximum(col - w // 2, 0)).astype(F32)
        hg = h[:, gi * gd:(gi + 1) * gd]
        total = jnp.dot(window, hg, precision=HIGHEST, preferred_element_type=F32)
        p = total / count - hg
        y_ref[:, gi * gd:(gi + 1) * gd] = jnp.dot(p.astype(BF16), pw_ref[gi], preferred_element_type=F32)
    y = y_ref[...] * ps_ref[...]
    r = DEEPNORM_ALPHA * x + g1_ref[0, pl.ds(g, 1), :] * y
    x1 = _layer_norm(r, lng_ref[...], lnb_ref[...])
    x1_ref[...] = x1
    h2_ref[...] = x1 * (1.0 + sc2_ref[0, pl.ds(g, 1), :]) + sh2_ref[0, pl.ds(g, 1), :]


def _pool_layer(x_prev, order0, pool_w, pool_scale, mod, ln_g, ln_b, layer, batch, seq):
    d = x_prev.shape[1]
    tm = order0.tile
    lat_tiles = seq // tm
    n_tiles = batch * lat_tiles
    ng, gd, _ = pool_w.shape

    def src(i):
        b = i // lat_tiles
        return (b * order0.per_batch + order0.ctx_tiles + i % lat_tiles, 0)

    row = pl.BlockSpec((1, d), lambda i: (0, 0))
    tok = pl.BlockSpec((tm, d), lambda i: (i, 0))
    ms = lambda chunk: _mod_spec(layer, chunk, d, 1)
    return pl.pallas_call(
        functools.partial(_pool_kernel, lat_tiles=lat_tiles),
        grid=(n_tiles,),
        in_specs=[pl.BlockSpec((tm, d), src),
                  pl.BlockSpec((ng, gd, gd), lambda i: (0, 0, 0)),
                  row, ms(MOD_SC1), ms(MOD_SH1), ms(MOD_G1), ms(MOD_SC2), ms(MOD_SH2), row, row],
        out_specs=[tok, tok],
        out_shape=[jax.ShapeDtypeStruct((n_tiles * tm, d), F32)] * 2,
        scratch_shapes=[pltpu.VMEM((tm, d), F32)],
        compiler_params=pltpu.CompilerParams(vmem_limit_bytes=VMEM_LIMIT),
    )(x_prev, pool_w.astype(BF16), pool_scale.reshape(1, d), mod, mod, mod, mod, mod,
      ln_g.reshape(1, d), ln_b.reshape(1, d))


def _router_kernel(h_ref, w_ref, b_ref, eidx_ref, gate_ref, rank_ref, cnt_ref, carry_ref):
    i = pl.program_id(0)
    n_exp = w_ref.shape[0]
    tr = h_ref.shape[0]
    per = n_exp // N_EXPERT_GROUPS
    neg = -jnp.inf

    @pl.when(i == 0)
    def _():
        carry_ref[...] = jnp.zeros_like(carry_ref)

    logits = lax.dot_general(w_ref[...], h_ref[...], (((1,), (1,)), ((), ())),
                             precision=HIGHEST, preferred_element_type=F32)
    scores = jax.nn.sigmoid(logits)
    biased = scores + b_ref[...]

    sub_idx = lax.broadcasted_iota(jnp.int32, (per, tr), 0)
    group_score = []
    for gi in range(N_EXPERT_GROUPS):
        sub = biased[gi * per:(gi + 1) * per, :]
        m1 = jnp.max(sub, axis=0, keepdims=True)
        first = jnp.min(jnp.where(sub == m1, sub_idx, per), axis=0, keepdims=True)
        m2 = jnp.max(jnp.where(sub_idx == first, neg, sub), axis=0, keepdims=True)
        group_score.append(m1 + m2)
    masked = []
    for gi in range(N_EXPERT_GROUPS):
        ahead = jnp.zeros((1, tr), jnp.int32)
        for gj in range(N_EXPERT_GROUPS):
            if gj == gi:
                continue
            wins = (group_score[gj] >= group_score[gi]) if gj < gi else (group_score[gj] > group_score[gi])
            ahead = ahead + wins.astype(jnp.int32)
        keep = ahead < TOPK_GROUPS
        masked.append(jnp.where(keep, biased[gi * per:(gi + 1) * per, :], neg))
    masked = jnp.concatenate(masked, axis=0)

    eid = lax.broadcasted_iota(jnp.int32, (n_exp, tr), 0)
    picks, gates = [], []
    onehot = jnp.zeros((n_exp, tr), F32)
    for _ in range(TOP_K):
        m = jnp.max(masked, axis=0, keepdims=True)
        first = jnp.min(jnp.where(masked == m, eid, n_exp), axis=0, keepdims=True)
        sel = eid == first
        gates.append(jnp.sum(jnp.where(sel, scores, 0.0), axis=0, keepdims=True))
        picks.append(first)
        masked = jnp.where(sel, neg, masked)
        onehot = onehot + sel.astype(F32)
    gate_sum = gates[0]
    for gk in gates[1:]:
        gate_sum = gate_sum + gk

    si = lax.broadcasted_iota(jnp.int32, (tr, tr), 0)
    ti = lax.broadcasted_iota(jnp.int32, (tr, tr), 1)
    before = jnp.where(si < ti, 1.0, 0.0).astype(BF16)
    prefix = jnp.dot(onehot.astype(BF16), before, preferred_element_type=F32) + carry_ref[:, :1]
    for k in range(TOP_K):
        eidx_ref[k:k + 1, :] = picks[k]
        gate_ref[k:k + 1, :] = gates[k] / gate_sum * ROUTED_SCALE
        rank_ref[k:k + 1, :] = jnp.sum(jnp.where(eid == picks[k], prefix, 0.0), axis=0,
                                       keepdims=True).astype(jnp.int32)
    carry_ref[...] = carry_ref[...] + jnp.sum(onehot, axis=1, keepdims=True)
    cnt_ref[...] = carry_ref[...]


def _router(h2, router_w, router_b):
    n_tok, d = h2.shape
    n_exp = router_w.shape[1]
    tr = ROUTER_TILE
    lanes = 128
    assert n_tok % tr == 0
    sel = pl.BlockSpec((TOP_K, tr), lambda i: (0, i))
    return pl.pallas_call(
        _router_kernel,
        grid=(n_tok // tr,),
        in_specs=[pl.BlockSpec((tr, d), lambda i: (i, 0)),
                  pl.BlockSpec((n_exp, d), lambda i: (0, 0)),
                  pl.BlockSpec((n_exp, 1), lambda i: (0, 0))],
        out_specs=[sel, sel, sel, pl.BlockSpec((n_exp, lanes), lambda i: (0, 0))],
        out_shape=[jax.ShapeDtypeStruct((TOP_K, n_tok), jnp.int32),
                   jax.ShapeDtypeStruct((TOP_K, n_tok), F32),
                   jax.ShapeDtypeStruct((TOP_K, n_tok), jnp.int32),
                   jax.ShapeDtypeStruct((n_exp, lanes), F32)],
        scratch_shapes=[pltpu.VMEM((n_exp, lanes), F32)],
        compiler_params=pltpu.CompilerParams(vmem_limit_bytes=VMEM_LIMIT),
    )(h2, router_w.astype(F32).T, router_b.astype(F32).reshape(n_exp, 1))


def _moe_kernel(be_ref, nu_ref, tok_hbm, h_hbm, wg_ref, wu_ref, wd_ref, y_ref, idx_ref, xbuf, isem, rsem):
    i = pl.program_id(0)
    nu = nu_ref[0]
    bm = xbuf.shape[1]

    def idx_copy(block, slot):
        return pltpu.make_async_copy(tok_hbm.at[block], idx_ref.at[slot], isem.at[slot])

    def issue_rows(slot):
        def body(r, carry):
            t = idx_ref[slot, r]
            pltpu.make_async_copy(h_hbm.at[pl.ds(t, 1), :], xbuf.at[slot, pl.ds(r, 1), :], rsem.at[slot]).start()
            return carry
        lax.fori_loop(0, bm, body, 0, unroll=8)

    @pl.when(i == 0)
    def _():
        idx_copy(0, 0).start()
        idx_copy(0, 0).wait()
        issue_rows(0)

        @pl.when(nu > 1)
        def _():
            idx_copy(1, 1).start()

    nxt = (i + 1) % 2

    @pl.when(i + 1 < nu)
    def _():
        idx_copy(i + 1, nxt).wait()
        issue_rows(nxt)

    @pl.when(i + 2 < nu)
    def _():
        idx_copy(i + 2, i % 2).start()

    @pl.when(i < nu)
    def _():
        slot = i % 2
        pltpu.make_async_copy(h_hbm.at[pl.ds(0, bm), :], xbuf.at[slot], rsem.at[slot]).wait()
        x = xbuf[slot].astype(BF16)
        a = (_silu(jnp.dot(x, wg_ref[0], preferred_element_type=F32))
             * jnp.dot(x, wu_ref[0], preferred_element_type=F32))
        y_ref[...] = jnp.dot(a.astype(BF16), wd_ref[0], preferred_element_type=F32)

    @pl.when(i >= nu)
    def _():
        y_ref[...] = jnp.zeros_like(y_ref)


def _moe_experts(h2, tok_blocks, blk_expert, n_used, wg, wu, wd):
    n_tok, d = h2.shape
    nb, bm = tok_blocks.shape
    n_exp, _, de = wg.shape
    grid_spec = pltpu.PrefetchScalarGridSpec(
        num_scalar_prefetch=2,
        grid=(nb,),
        in_specs=[pl.BlockSpec(memory_space=pl.ANY),
                  pl.BlockSpec(memory_space=pl.ANY),
                  pl.BlockSpec((1, d, de), lambda i, be, nu: (be[i], 0, 0)),
                  pl.BlockSpec((1, d, de), lambda i, be, nu: (be[i], 0, 0)),
                  pl.BlockSpec((1, de, d), lambda i, be, nu: (be[i], 0, 0))],
        out_specs=pl.BlockSpec((bm, d), lambda i, be, nu: (i, 0)),
        scratch_shapes=[pltpu.SMEM((2, bm), jnp.int32),
                        pltpu.VMEM((2, bm, d), F32),
                        pltpu.SemaphoreType.DMA((2,)),
                        pltpu.SemaphoreType.DMA((2,))])
    return pl.pallas_call(
        _moe_kernel,
        grid_spec=grid_spec,
        out_shape=jax.ShapeDtypeStruct((nb * bm, d), F32),
        compiler_params=pltpu.CompilerParams(vmem_limit_bytes=VMEM_LIMIT, disable_bounds_checks=True),
    )(blk_expert, n_used, tok_blocks, h2, wg, wu, wd)


def _combine_kernel(slot_hbm, y_hbm, gate_ref, h2_ref, x1_ref, sg_ref, su_ref, sd_ref, g2_ref, lng_ref, lnb_ref,
                    out_ref, idx_ref, ybuf, isem, rsem, *, order, n_steps):
    i = pl.program_id(0)
    tt = h2_ref.shape[0]
    rows = TOP_K * tt

    def idx_copy(tile, slot):
        return pltpu.make_async_copy(slot_hbm.at[tile], idx_ref.at[slot], isem.at[slot])

    def issue_rows(slot):
        def body(r, carry):
            t = idx_ref[slot, r]
            pltpu.make_async_copy(y_hbm.at[pl.ds(t, 1), :], ybuf.at[slot, pl.ds(r, 1), :], rsem.at[slot]).start()
            return carry
        lax.fori_loop(0, rows, body, 0, unroll=8)

    @pl.when(i == 0)
    def _():
        idx_copy(0, 0).start()
        idx_copy(0, 0).wait()
        issue_rows(0)
        if n_steps > 1:
            idx_copy(1, 1).start()

    nxt = (i + 1) % 2

    @pl.when(i + 1 < n_steps)
    def _():
        idx_copy(i + 1, nxt).wait()
        issue_rows(nxt)

    @pl.when(i + 2 < n_steps)
    def _():
        idx_copy(i + 2, i % 2).start()

    slot = i % 2
    hb = h2_ref[...].astype(BF16)
    a = (_silu(jnp.dot(hb, sg_ref[...], preferred_element_type=F32))
         * jnp.dot(hb, su_ref[...], preferred_element_type=F32))
    f = jnp.dot(a.astype(BF16), sd_ref[...], preferred_element_type=F32)
    pltpu.make_async_copy(y_hbm.at[pl.ds(0, rows), :], ybuf.at[slot], rsem.at[slot]).wait()
    gate = gate_ref[...]
    for k in range(TOP_K):
        f = f + gate[:, k:k + 1] * ybuf[slot, k * tt:(k + 1) * tt, :]
    g = order.group(i)
    r = DEEPNORM_ALPHA * x1_ref[...] + g2_ref[0, pl.ds(g, 1), :] * f
    out_ref[...] = _layer_norm(r, lng_ref[...], lnb_ref[...])


def _combine(slots, y_sorted, gate, h2, x1, sg, su, sd, mod, ln_g, ln_b, layer, order):
    n_tok, d = h2.shape
    tt = order.tile
    n_steps = order.n_tiles
    ds_ = sg.shape[1]
    row = pl.BlockSpec((1, d), lambda i: (0, 0))
    tok = pl.BlockSpec((tt, d), lambda i: (i, 0))
    return pl.pallas_call(
        functools.partial(_combine_kernel, order=order, n_steps=n_steps),
        grid=(n_steps,),
        in_specs=[pl.BlockSpec(memory_space=pl.ANY),
                  pl.BlockSpec(memory_space=pl.ANY),
                  pl.BlockSpec((tt, TOP_K), lambda i: (i, 0)),
                  tok, tok,
                  pl.BlockSpec((d, ds_), lambda i: (0, 0)),
                  pl.BlockSpec((d, ds_), lambda i: (0, 0)),
                  pl.BlockSpec((ds_, d), lambda i: (0, 0)),
                  _mod_spec(layer, MOD_G2, d, 1), row, row],
        out_specs=tok,
        out_shape=jax.ShapeDtypeStruct((n_tok, d), F32),
        scratch_shapes=[pltpu.SMEM((2, TOP_K * tt), jnp.int32),
                        pltpu.VMEM((2, TOP_K * tt, d), F32),
                        pltpu.SemaphoreType.DMA((2,)),
                        pltpu.SemaphoreType.DMA((2,))],
        compiler_params=pltpu.CompilerParams(vmem_limit_bytes=VMEM_LIMIT, disable_bounds_checks=True),
    )(slots, y_sorted, gate, h2, x1, sg, su, sd, mod, ln_g.reshape(1, d), ln_b.reshape(1, d))


def _moe_layer(x1, h2, layer, mod, order, router_w, router_b, wg, wu, wd, sg, su, sd, ln_g, ln_b):
    n_tok, d = h2.shape
    n_exp = router_w.shape[1]
    bm = MOE_BLOCK
    eidx_t, gate_t, rank_t, cnt = _router(h2, router_w, router_b)

    counts = cnt[:, 0].astype(jnp.int32)
    padded = (counts + bm - 1) // bm * bm
    pad_end = jnp.cumsum(padded)
    pad_start = pad_end - padded
    slot_t = pad_start[eidx_t] + rank_t
    n_slots = n_tok * TOP_K + n_exp * bm
    nb = n_slots // bm
    n_used = (pad_end[-1:] // bm).astype(jnp.int32)
    blk_expert = jnp.minimum(jnp.searchsorted(pad_end, jnp.arange(nb, dtype=jnp.int32) * bm, side='right'),
                             n_exp - 1).astype(jnp.int32)
    tok_ids = jnp.broadcast_to(jnp.arange(n_tok, dtype=jnp.int32)[None, :], (TOP_K, n_tok))
    tok_buf = jnp.zeros((n_slots,), jnp.int32).at[slot_t.reshape(-1)].set(tok_ids.reshape(-1))

    y_sorted = _moe_experts(h2, tok_buf.reshape(nb, bm), blk_expert, n_used,
                            wg.astype(BF16), wu.astype(BF16), wd.astype(BF16))

    tt = COMBINE_TILE
    c_order = _Order(order.batch, order.seq, order.ctx_len, tt)
    slots = slot_t.reshape(TOP_K, n_tok // tt, tt).transpose(1, 0, 2).reshape(n_tok // tt, TOP_K * tt)
    return _combine(slots, y_sorted, gate_t.T, h2, x1, sg.astype(BF16), su.astype(BF16), sd.astype(BF16),
                    mod, ln_g, ln_b, layer, c_order)


def kernel(x, c, ctx, c_ctx, ada_w, ada_b, ln_mix_g, ln_mix_b, ln_ffn_g, ln_ffn_b, ret_w_in, ret_w_out,
           ret_decay_fwd, ret_decay_bwd, pool_w, pool_scale, router_w, router_b, exp_w_gate, exp_w_up,
           exp_w_down, sh_w_gate, sh_w_up, sh_w_down):
    batch, seq, d = x.shape
    ctx_len = ctx.shape[1]
    heads = RET_HEADS
    qk_dim = d
    head_dim = qk_dim // heads
    v_dim = ret_w_out.shape[1]
    assert ada_w.shape[0] == DEPTH == 2 and batch + 1 <= MOD_ROWS

    cvec = jnp.zeros((MOD_ROWS, d), F32).at[:batch].set(c.astype(F32)).at[batch].set(c_ctx.astype(F32))
    mod = _modulation(cvec, ada_w.astype(F32), ada_b.astype(F32))

    x2d = x.reshape(batch * seq, d)
    ctx2d = ctx.reshape(batch * ctx_len, d)
    order0 = _Order(batch, seq, ctx_len, TOKEN_TILE)

    w_in = ret_w_in[0]

    def pairs_apart(w):
        return w.reshape(d, heads, head_dim // 2, 2).transpose(0, 1, 3, 2).reshape(d, qk_dim)

    w_qk = jnp.concatenate([pairs_apart(w_in[:, :qk_dim]), pairs_apart(w_in[:, qk_dim:2 * qk_dim])],
                           axis=1).astype(BF16)
    w_v = w_in[:, 2 * qk_dim:2 * qk_dim + v_dim].astype(BF16)
    w_g = w_in[:, 2 * qk_dim + v_dim:].astype(BF16)
    tabs = _rope_tables(seq, head_dim, TOKEN_TILE)
    qk = _inproj(x2d, ctx2d, mod, w_qk, order0, BF16, rope_tabs=tabs, qk_dim=qk_dim, head_dim=head_dim)
    v = _inproj(x2d, ctx2d, mod, w_v, order0, BF16)
    g = _inproj(x2d, ctx2d, mod, w_g, order0, F32)
    og = _retention(qk, v, g, ret_decay_fwd[0], ret_decay_bwd[0], batch, ctx_len + seq, ctx_len)
    x1, h2 = _outproj(og, x2d, ctx2d, ret_w_out[0].astype(BF16), mod, ln_mix_g[0], ln_mix_b[0], order0)
    xa = _moe_layer(x1, h2, 0, mod, order0, router_w[0], router_b[0], exp_w_gate[0], exp_w_up[0],
                    exp_w_down[0], sh_w_gate[0], sh_w_up[0], sh_w_down[0], ln_ffn_g[0], ln_ffn_b[0])

    order1 = _Order(batch, seq, 0, TOKEN_TILE)
    x1, h2 = _pool_layer(xa, order0, pool_w[0], pool_scale[0], mod, ln_mix_g[1], ln_mix_b[1], 1, batch, seq)
    xb = _moe_layer(x1, h2, 1, mod, order1, router_w[1], router_b[1], exp_w_gate[1], exp_w_up[1],
                    exp_w_down[1], sh_w_gate[1], sh_w_up[1], sh_w_down[1], ln_ffn_g[1], ln_ffn_b[1])
    return xb.reshape(batch, seq, d)
```

```python
import functools

import jax
import jax.numpy as jnp
from jax import lax
from jax.experimental import pallas as pl
from jax.experimental.pallas import tpu as pltpu

F32 = jnp.float32
BF16 = jnp.bfloat16
U32 = jnp.uint32
HIGHEST = lax.Precision.HIGHEST

DEPTH = 2
GRID_W = 64
RET_HEADS = 8
RET_CHUNK = 128
ROPE_BASE = 10000.0
POOL_WINDOWS = (2, 4, 8, 16)
N_EXPERT_GROUPS = 8
TOPK_GROUPS = 4
TOP_K = 8
ROUTED_SCALE = 2.5
LN_EPS = 1e-5
N_MOD = 6
DEEPNORM_ALPHA = (2.0 * DEPTH) ** 0.25
MOD_SH1, MOD_SC1, MOD_G1, MOD_SH2, MOD_SC2, MOD_G2 = range(N_MOD)

LANES = 128
TOKEN_TILE = 256
RET_BLOCK = 768
ROUTER_TILE = 512
MOE_BLOCK = 256
MOD_ROWS = 8
VMEM_LIMIT = 56 * 1024 * 1024


def _silu(x):
    return x * jax.nn.sigmoid(x)


def _layer_norm(r, g, b):
    mu = jnp.mean(r, axis=-1, keepdims=True)
    var = jnp.mean(jnp.square(r - mu), axis=-1, keepdims=True)
    return (r - mu) * lax.rsqrt(var + LN_EPS) * g + b


def _to_token_rows(scr_ref, h):
    tm, d = h.shape
    s_rows = d // LANES
    for s in range(s_rows):
        scr_ref[pl.ds(s, tm, stride=s_rows), :] = h[:, s * LANES:(s + 1) * LANES]
    return scr_ref[...].astype(BF16)


def _from_token_rows(scr_ref, rows, n, s):
    return scr_ref[pl.ds(s, n, stride=rows), :]


class _Order:
    def __init__(self, batch, seq, ctx_len, tile):
        assert seq % tile == 0 and ctx_len % tile == 0
        self.batch, self.seq, self.ctx_len, self.tile = batch, seq, ctx_len, tile
        self.ctx_tiles = ctx_len // tile
        self.lat_tiles = seq // tile
        self.per_batch = self.ctx_tiles + self.lat_tiles
        self.n_tiles = batch * self.per_batch
        self.n_tok = self.n_tiles * tile

    def split(self, i):
        b = i // self.per_batch
        r = i % self.per_batch
        return b, r, r < self.ctx_tiles

    def group(self, i):
        b, _, is_ctx = self.split(i)
        return jnp.where(is_ctx, self.batch, b)

    def lat_block(self, i):
        b, r, _ = self.split(i)
        return b * self.lat_tiles + jnp.maximum(r - self.ctx_tiles, 0)

    def ctx_block(self, i):
        b, r, _ = self.split(i)
        return b * self.ctx_tiles + jnp.minimum(r, max(self.ctx_tiles - 1, 0))


def _mod_kernel(c_ref, w_ref, b_ref, o_ref):
    s = _silu(c_ref[...])
    o_ref[0] = jnp.dot(s, w_ref[0], precision=HIGHEST, preferred_element_type=F32) + b_ref[0]


def _modulation(cvec, ada_w, ada_b):
    depth, d, n = ada_w.shape
    tn = 1024
    return pl.pallas_call(
        _mod_kernel,
        grid=(depth, n // tn),
        in_specs=[pl.BlockSpec((MOD_ROWS, d), lambda l, j: (0, 0)),
                  pl.BlockSpec((1, d, tn), lambda l, j: (l, 0, j)),
                  pl.BlockSpec((1, 1, tn), lambda l, j: (l, 0, j))],
        out_specs=pl.BlockSpec((1, MOD_ROWS, tn), lambda l, j: (l, 0, j)),
        out_shape=jax.ShapeDtypeStruct((depth, MOD_ROWS, n), F32),
        compiler_params=pltpu.CompilerParams(vmem_limit_bytes=VMEM_LIMIT),
    )(cvec, ada_w, ada_b.reshape(depth, 1, n))


def _mod_spec(layer, chunk, d, n_grid_args):
    if n_grid_args == 1:
        return pl.BlockSpec((1, MOD_ROWS, d), lambda i: (layer, 0, chunk))
    return pl.BlockSpec((1, MOD_ROWS, d), lambda j, i: (layer, 0, chunk))


def _inproj_kernel(x_ref, c_ref, sc_ref, sh_ref, w_ref, *rest, order, rope, k_col0, head_dim):
    i = pl.program_id(1)
    j = pl.program_id(0)
    if rope:
        cos_ref, sin_ref, o_ref, wb_ref = rest
    else:
        o_ref, wb_ref = rest

    @pl.when(i == 0)
    def _():
        wb_ref[...] = w_ref[...].astype(BF16)

    g = order.group(i)
    _, _, is_ctx = order.split(i)
    sc = sc_ref[0, pl.ds(g, 1), :]
    sh = sh_ref[0, pl.ds(g, 1), :]
    xin = jnp.where(is_ctx, c_ref[...], x_ref[...])
    a = (xin * (1.0 + sc) + sh).astype(BF16)
    acc = jnp.dot(a, wb_ref[...], preferred_element_type=F32)
    if not rope:
        o_ref[...] = acc.astype(o_ref.dtype)
        return
    tn = acc.shape[1]
    lane = lax.broadcasted_iota(jnp.int32, acc.shape, 1)
    partner = jnp.where((lane & 1) == 0, pltpu.roll(acc, tn - 1, 1), pltpu.roll(acc, 1, 1))
    cos = cos_ref[...]
    sin = sin_ref[...]
    scale = jnp.where(j * tn >= k_col0, head_dim ** -0.5, 1.0).astype(F32)
    for h in range(tn // head_dim):
        sl = slice(h * head_dim, (h + 1) * head_dim)
        o_ref[:, sl] = ((acc[:, sl] * cos + partner[:, sl] * sin) * scale).astype(o_ref.dtype)


def _inproj(x2d, ctx2d, mod, w_in, col0, n, order, out_dtype, rope_tabs=None, k_col0=0, head_dim=0):
    d = w_in.shape[0]
    tm = order.tile
    tn = min(n, 2048)
    assert col0 % tn == 0 and n % tn == 0
    rope = rope_tabs is not None
    in_specs = [pl.BlockSpec((tm, d), lambda j, i: (order.lat_block(i), 0)),
                pl.BlockSpec((tm, d), lambda j, i: (order.ctx_block(i), 0)),
                _mod_spec(0, MOD_SC1, d, 2), _mod_spec(0, MOD_SH1, d, 2),
                pl.BlockSpec((d, tn), lambda j, i: (0, col0 // tn + j), pipeline_mode=pl.Buffered(1))]
    args = [x2d, ctx2d, mod, mod, w_in]
    if rope:
        def tab_map(j, i):
            _, r, is_ctx = order.split(i)
            return (jnp.where(is_ctx, order.lat_tiles, r - order.ctx_tiles), 0)
        in_specs += [pl.BlockSpec((tm, head_dim), tab_map), pl.BlockSpec((tm, head_dim), tab_map)]
        args += list(rope_tabs)
    return pl.pallas_call(
        functools.partial(_inproj_kernel, order=order, rope=rope, k_col0=k_col0, head_dim=head_dim),
        grid=(n // tn, order.n_tiles),
        in_specs=in_specs,
        out_specs=pl.BlockSpec((tm, tn), lambda j, i: (i, j)),
        out_shape=jax.ShapeDtypeStruct((order.n_tok, n), out_dtype),
        scratch_shapes=[pltpu.VMEM((d, tn), BF16)],
        compiler_params=pltpu.CompilerParams(vmem_limit_bytes=VMEM_LIMIT),
    )(*args)


def _rope_tables(seq, head_dim, tile):
    n_freq = head_dim // 4
    pos = jnp.arange(seq, dtype=jnp.int32)
    row = (pos // GRID_W).astype(F32)
    col = (pos % GRID_W).astype(F32)
    freqs = ROPE_BASE ** (-jnp.arange(n_freq, dtype=F32) / n_freq)
    ang = jnp.concatenate([row[:, None] * freqs, col[:, None] * freqs], axis=-1)
    cos = jnp.repeat(jnp.cos(ang), 2, axis=-1)
    sin = jnp.stack([-jnp.sin(ang), jnp.sin(ang)], axis=-1).reshape(seq, head_dim)
    ident = jnp.zeros((tile, head_dim), F32)
    return jnp.concatenate([cos, ident + 1.0], axis=0), jnp.concatenate([sin, ident], axis=0)


def _ret_kernel(q_ref, k_ref, v_ref, g_ref, df_ref, db_ref, o_ref, sf_ref, sb_ref, obuf_ref, *,
                n_blocks, chunks_per_block, ctx_chunks):
    C = RET_CHUNK
    NB, CPB, NCC = n_blocks, chunks_per_block, ctx_chunks
    st = pl.program_id(2)
    phase = st // (NB + 1)
    s = st % (NB + 1)
    ldf = jax.nn.log_sigmoid(df_ref[0])[:, :1]
    ldb = jax.nn.log_sigmoid(db_ref[0])[:, :1]
    pos = lax.broadcasted_iota(jnp.int32, (C, 1), 0).astype(F32)
    tdims = (((0,), (0,)), ((), ()))

    @pl.when(phase == 0)
    def _backward_states():
        @pl.when(s == 0)
        def _():
            sb_ref[...] = jnp.zeros_like(sb_ref)

        blk = jnp.where(s == 0, 0, NB - s)
        lo = jnp.where(s == NB, NCC, 0)
        hi = jnp.where(s == 0, NCC, CPB)
        q_dec = jnp.exp(ldb * (C - pos))
        k_dec = jnp.exp(ldb * pos)
        chunk_dec = jnp.exp(ldb * C)

        def body(t, carry):
            c = hi - 1 - t
            off = pl.multiple_of(c * C, C)
            q = q_ref[pl.ds(off, C), :].astype(F32)
            k = k_ref[pl.ds(off, C), :].astype(F32)
            v = v_ref[pl.ds(off, C), :]
            state = sb_ref[...]
            goff = pl.multiple_of((blk * CPB + c) * C, C)
            obuf_ref[pl.ds(goff, C), :] = jnp.dot((q * q_dec).astype(BF16), state.astype(BF16),
                                                  preferred_element_type=F32)
            sb_ref[...] = state * chunk_dec + lax.dot_general((k * k_dec).astype(BF16), v, tdims,
                                                              preferred_element_type=F32)
            return carry

        lax.fori_loop(0, hi - lo, body, 0)

    @pl.when((phase == 1) & (s < NB))
    def _forward_and_output():
        @pl.when(s == 0)
        def _():
            sf_ref[...] = jnp.zeros_like(sf_ref)

        q_dec = jnp.exp(ldf * (pos + 1.0))
        k_dec = jnp.exp(ldf * (C - 1.0 - pos))
        chunk_dec = jnp.exp(ldf * C)
        ii = lax.broadcasted_iota(jnp.int32, (C, C), 0)
        jj = lax.broadcasted_iota(jnp.int32, (C, C), 1)
        rel = (ii - jj).astype(F32)
        decay = jnp.where(rel >= 0, jnp.exp(ldf * jnp.maximum(rel, 0.0)), jnp.exp(ldb * jnp.maximum(-rel, 0.0)))

        def body(c, carry):
            off = pl.multiple_of(c * C, C)
            qb = q_ref[pl.ds(off, C), :]
            kb = k_ref[pl.ds(off, C), :]
            v = v_ref[pl.ds(off, C), :]
            q = qb.astype(F32)
            k = kb.astype(F32)
            state = sf_ref[...]
            scores = lax.dot_general(qb, kb, (((1,), (1,)), ((), ())), preferred_element_type=F32) * decay
            goff = pl.multiple_of((s * CPB + c) * C, C)
            o = (jnp.dot(scores.astype(BF16), v, preferred_element_type=F32)
                 + jnp.dot((q * q_dec).astype(BF16), state.astype(BF16), preferred_element_type=F32)
                 + obuf_ref[pl.ds(goff, C), :])
            sf_ref[...] = state * chunk_dec + lax.dot_general((k * k_dec).astype(BF16), v, tdims,
                                                              preferred_element_type=F32)
            mu = jnp.mean(o, axis=-1, keepdims=True)
            var = jnp.mean(jnp.square(o - mu), axis=-1, keepdims=True)
            on = (o - mu) * lax.rsqrt(var + LN_EPS)
            o_ref[pl.ds(off, C), :] = (_silu(g_ref[pl.ds(off, C), :]) * on).astype(o_ref.dtype)
            return carry

        lax.fori_loop(0, CPB, body, 0)


def _retention(qk, v, g, raw_fwd, raw_bwd, batch, tokens_per_batch, ctx_len):
    n_tok, qk2 = qk.shape
    v_dim = v.shape[1]
    heads = RET_HEADS
    dk = qk2 // 2 // heads
    dv = v_dim // heads
    tb = RET_BLOCK
    assert tokens_per_batch % tb == 0 and tb % RET_CHUNK == 0 and ctx_len % RET_CHUNK == 0 and ctx_len <= tb
    nb = tokens_per_batch // tb
    steps = 2 * (nb + 1)

    def blk(st):
        phase = st // (nb + 1)
        s = st % (nb + 1)
        back = jnp.where(s == 0, 0, nb - s)
        return jnp.where(phase == 0, back, jnp.minimum(s, nb - 1))

    def fwd_blk(st):
        phase = st // (nb + 1)
        s = st % (nb + 1)
        return jnp.where(phase == 0, 0, jnp.minimum(s, nb - 1))

    dec = lambda raw: jnp.broadcast_to(raw.astype(F32)[:, None, None], (heads, 1, LANES))
    return pl.pallas_call(
        functools.partial(_ret_kernel, n_blocks=nb, chunks_per_block=tb // RET_CHUNK,
                          ctx_chunks=ctx_len // RET_CHUNK),
        grid=(batch, heads, steps),
        in_specs=[pl.BlockSpec((tb, dk), lambda b, h, st: (b * nb + blk(st), h)),
                  pl.BlockSpec((tb, dk), lambda b, h, st: (b * nb + blk(st), heads + h)),
                  pl.BlockSpec((tb, dv), lambda b, h, st: (b * nb + blk(st), h)),
                  pl.BlockSpec((tb, dv), lambda b, h, st: (b * nb + fwd_blk(st), h)),
                  pl.BlockSpec((1, 1, LANES), lambda b, h, st: (h, 0, 0)),
                  pl.BlockSpec((1, 1, LANES), lambda b, h, st: (h, 0, 0))],
        out_specs=pl.BlockSpec((tb, dv), lambda b, h, st: (b * nb + fwd_blk(st), h)),
        out_shape=jax.ShapeDtypeStruct((n_tok, v_dim), BF16),
        scratch_shapes=[pltpu.VMEM((dk, dv), F32), pltpu.VMEM((dk, dv), F32),
                        pltpu.VMEM((tokens_per_batch, dv), F32)],
        compiler_params=pltpu.CompilerParams(vmem_limit_bytes=VMEM_LIMIT),
    )(qk, qk, v, g, dec(raw_fwd), dec(raw_bwd))


def _outproj_kernel(o_ref, x_ref, c_ref, w_ref, g1_ref, sc2_ref, sh2_ref, lng_ref, lnb_ref,
                    x1_ref, h2_ref, hp_ref, scr_ref, *, order):
    i = pl.program_id(0)
    g = order.group(i)
    _, _, is_ctx = order.split(i)
    y = jnp.dot(o_ref[...], w_ref[...], preferred_element_type=F32)
    xin = jnp.where(is_ctx, c_ref[...], x_ref[...])
    r = DEEPNORM_ALPHA * xin + g1_ref[0, pl.ds(g, 1), :] * y
    x1 = _layer_norm(r, lng_ref[...], lnb_ref[...])
    x1_ref[...] = x1
    h2 = x1 * (1.0 + sc2_ref[0, pl.ds(g, 1), :]) + sh2_ref[0, pl.ds(g, 1), :]
    h2_ref[...] = h2
    hp_ref[...] = _to_token_rows(scr_ref, h2)


def _outproj(og, x2d, ctx2d, w_out, mod, ln_g, ln_b, order):
    v_dim, d = w_out.shape
    tm = order.tile
    s_rows = d // LANES
    row = pl.BlockSpec((1, d), lambda i: (0, 0))
    tok = pl.BlockSpec((tm, d), lambda i: (i, 0))
    return pl.pallas_call(
        functools.partial(_outproj_kernel, order=order),
        grid=(order.n_tiles,),
        in_specs=[pl.BlockSpec((tm, v_dim), lambda i: (i, 0)),
                  pl.BlockSpec((tm, d), lambda i: (order.lat_block(i), 0)),
                  pl.BlockSpec((tm, d), lambda i: (order.ctx_block(i), 0)),
                  pl.BlockSpec((v_dim, d), lambda i: (0, 0), pipeline_mode=pl.Buffered(1)),
                  _mod_spec(0, MOD_G1, d, 1), _mod_spec(0, MOD_SC2, d, 1), _mod_spec(0, MOD_SH2, d, 1),
                  row, row],
        out_specs=[tok, tok, pl.BlockSpec((tm * s_rows, LANES), lambda i: (i, 0))],
        out_shape=[jax.ShapeDtypeStruct((order.n_tok, d), F32)] * 2
                  + [jax.ShapeDtypeStruct((order.n_tok * s_rows, LANES), BF16)],
        scratch_shapes=[pltpu.VMEM((tm * s_rows, LANES), F32)],
        compiler_params=pltpu.CompilerParams(vmem_limit_bytes=VMEM_LIMIT),
    )(og, x2d, ctx2d, w_out, mod, mod, mod, ln_g.reshape(1, d), ln_b.reshape(1, d))


def _pool_kernel(x_ref, pw_ref, ps_ref, sc1_ref, sh1_ref, g1_ref, sc2_ref, sh2_ref, lng_ref, lnb_ref,
                 x1_ref, h2_ref, hp_ref, y_ref, scr_ref, *, lat_tiles):
    i = pl.program_id(0)
    g = i // lat_tiles
    x = x_ref[...]
    tm, d = x.shape
    gd = d // len(POOL_WINDOWS)
    h = x * (1.0 + sc1_ref[0, pl.ds(g, 1), :]) + sh1_ref[0, pl.ds(g, 1), :]
    ni = lax.broadcasted_iota(jnp.int32, (tm, tm), 0)
    mi = lax.broadcasted_iota(jnp.int32, (tm, tm), 1)
    same_row = (ni // GRID_W) == (mi // GRID_W)
    delta = mi - ni
    col = lax.broadcasted_iota(jnp.int32, (tm, 1), 0) % GRID_W
    for gi, w in enumerate(POOL_WINDOWS):
        window = jnp.where(same_row & (delta >= -(w // 2)) & (delta < w // 2), 1.0, 0.0).astype(F32)
        count = (jnp.minimum(col + w // 2, GRID_W) - jnp.maximum(col - w // 2, 0)).astype(F32)
        hg = h[:, gi * gd:(gi + 1) * gd]
        total = jnp.dot(window, hg, precision=HIGHEST, preferred_element_type=F32)
        p = total / count - hg
        y_ref[:, gi * gd:(gi + 1) * gd] = jnp.dot(p.astype(BF16), pw_ref[gi], preferred_element_type=F32)
    y = y_ref[...] * ps_ref[...]
    r = DEEPNORM_ALPHA * x + g1_ref[0, pl.ds(g, 1), :] * y
    x1 = _layer_norm(r, lng_ref[...], lnb_ref[...])
    x1_ref[...] = x1
    h2 = x1 * (1.0 + sc2_ref[0, pl.ds(g, 1), :]) + sh2_ref[0, pl.ds(g, 1), :]
    h2_ref[...] = h2
    hp_ref[...] = _to_token_rows(scr_ref, h2)


def _pool_layer(x_prev, order0, pool_w, pool_scale, mod, ln_g, ln_b, layer, batch, seq):
    d = x_prev.shape[1]
    tm = order0.tile
    s_rows = d // LANES
    lat_tiles = seq // tm
    n_tiles = batch * lat_tiles
    ng, gd, _ = pool_w.shape

    def src(i):
        b = i // lat_tiles
        return (b * order0.per_batch + order0.ctx_tiles + i % lat_tiles, 0)

    row = pl.BlockSpec((1, d), lambda i: (0, 0))
    tok = pl.BlockSpec((tm, d), lambda i: (i, 0))
    ms = lambda chunk: _mod_spec(layer, chunk, d, 1)
    return pl.pallas_call(
        functools.partial(_pool_kernel, lat_tiles=lat_tiles),
        grid=(n_tiles,),
        in_specs=[pl.BlockSpec((tm, d), src),
                  pl.BlockSpec((ng, gd, gd), lambda i: (0, 0, 0)),
                  row, ms(MOD_SC1), ms(MOD_SH1), ms(MOD_G1), ms(MOD_SC2), ms(MOD_SH2), row, row],
        out_specs=[tok, tok, pl.BlockSpec((tm * s_rows, LANES), lambda i: (i, 0))],
        out_shape=[jax.ShapeDtypeStruct((n_tiles * tm, d), F32)] * 2
                  + [jax.ShapeDtypeStruct((n_tiles * tm * s_rows, LANES), BF16)],
        scratch_shapes=[pltpu.VMEM((tm, d), F32), pltpu.VMEM((tm * s_rows, LANES), F32)],
        compiler_params=pltpu.CompilerParams(vmem_limit_bytes=VMEM_LIMIT),
    )(x_prev, pool_w.astype(BF16), pool_scale.reshape(1, d), mod, mod, mod, mod, mod,
      ln_g.reshape(1, d), ln_b.reshape(1, d))


def _router_kernel(h_ref, w_ref, b_ref, eidx_ref, gate_ref, rank_ref, cnt_ref, carry_ref):
    i = pl.program_id(0)
    n_exp = w_ref.shape[0]
    tr = h_ref.shape[0]
    per = n_exp // N_EXPERT_GROUPS
    neg = -jnp.inf

    @pl.when(i == 0)
    def _():
        carry_ref[...] = jnp.zeros_like(carry_ref)

    logits = lax.dot_general(w_ref[...], h_ref[...], (((1,), (1,)), ((), ())),
                             precision=HIGHEST, preferred_element_type=F32)
    scores = jax.nn.sigmoid(logits)
    biased = scores + b_ref[...]

    sub_idx = lax.broadcasted_iota(jnp.int32, (per, tr), 0)
    group_score = []
    for gi in range(N_EXPERT_GROUPS):
        sub = biased[gi * per:(gi + 1) * per, :]
        m1 = jnp.max(sub, axis=0, keepdims=True)
        first = jnp.min(jnp.where(sub == m1, sub_idx, per), axis=0, keepdims=True)
        m2 = jnp.max(jnp.where(sub_idx == first, neg, sub), axis=0, keepdims=True)
        group_score.append(m1 + m2)
    masked = []
    for gi in range(N_EXPERT_GROUPS):
        ahead = jnp.zeros((1, tr), jnp.int32)
        for gj in range(N_EXPERT_GROUPS):
            if gj == gi:
                continue
            wins = (group_score[gj] >= group_score[gi]) if gj < gi else (group_score[gj] > group_score[gi])
            ahead = ahead + wins.astype(jnp.int32)
        keep = ahead < TOPK_GROUPS
        masked.append(jnp.where(keep, biased[gi * per:(gi + 1) * per, :], neg))
    masked = jnp.concatenate(masked, axis=0)

    eid = lax.broadcasted_iota(jnp.int32, (n_exp, tr), 0)
    picks, gates = [], []
    onehot = jnp.zeros((n_exp, tr), F32)
    for _ in range(TOP_K):
        m = jnp.max(masked, axis=0, keepdims=True)
        first = jnp.min(jnp.where(masked == m, eid, n_exp), axis=0, keepdims=True)
        sel = eid == first
        gates.append(jnp.sum(jnp.where(sel, scores, 0.0), axis=0, keepdims=True))
        picks.append(first)
        masked = jnp.where(sel, neg, masked)
        onehot = onehot + sel.astype(F32)
    gate_sum = gates[0]
    for gk in gates[1:]:
        gate_sum = gate_sum + gk

    si = lax.broadcasted_iota(jnp.int32, (tr, tr), 0)
    ti = lax.broadcasted_iota(jnp.int32, (tr, tr), 1)
    before = jnp.where(si < ti, 1.0, 0.0).astype(BF16)
    prefix = jnp.dot(onehot.astype(BF16), before, preferred_element_type=F32) + carry_ref[:, :1]
    for k in range(TOP_K):
        eidx_ref[k:k + 1, :] = picks[k]
        gate_ref[k:k + 1, :] = gates[k] / gate_sum * ROUTED_SCALE
        rank_ref[k:k + 1, :] = jnp.sum(jnp.where(eid == picks[k], prefix, 0.0), axis=0,
                                       keepdims=True).astype(jnp.int32)
    carry_ref[...] = carry_ref[...] + jnp.sum(onehot, axis=1, keepdims=True)
    cnt_ref[...] = carry_ref[...]


def _router(h2, router_w, router_b):
    n_tok, d = h2.shape
    n_exp = router_w.shape[1]
    tr = ROUTER_TILE
    assert n_tok % tr == 0
    sel = pl.BlockSpec((TOP_K, tr), lambda i: (0, i))
    return pl.pallas_call(
        _router_kernel,
        grid=(n_tok // tr,),
        in_specs=[pl.BlockSpec((tr, d), lambda i: (i, 0)),
                  pl.BlockSpec((n_exp, d), lambda i: (0, 0)),
                  pl.BlockSpec((n_exp, 1), lambda i: (0, 0))],
        out_specs=[sel, sel, sel, pl.BlockSpec((n_exp, LANES), lambda i: (0, 0))],
        out_shape=[jax.ShapeDtypeStruct((TOP_K, n_tok), jnp.int32),
                   jax.ShapeDtypeStruct((TOP_K, n_tok), F32),
                   jax.ShapeDtypeStruct((TOP_K, n_tok), jnp.int32),
                   jax.ShapeDtypeStruct((n_exp, LANES), F32)],
        scratch_shapes=[pltpu.VMEM((n_exp, LANES), F32)],
        compiler_params=pltpu.CompilerParams(vmem_limit_bytes=VMEM_LIMIT),
    )(h2, router_w.astype(F32).T, router_b.astype(F32).reshape(n_exp, 1))


def _dispatch_kernel(pe_ref, pd_ref, nu_ref, slot_hbm, hp_ref, xs_hbm, idx_ref, zero_ref, isem, zsem, rsem, *,
                     n_steps, n_exp, n_blocks, s_rows, bm):
    i = pl.program_id(0)
    tt = hp_ref.shape[0] // s_rows
    zrows = bm * s_rows

    def idx_copy(tile, buf):
        return pltpu.make_async_copy(slot_hbm.at[tile], idx_ref.at[buf], isem.at[buf])

    def zero_copy(row0):
        return pltpu.make_async_copy(zero_ref, xs_hbm.at[pl.ds(pl.multiple_of(row0, s_rows), zrows), :], zsem)

    @pl.when(i == 0)
    def _():
        zero_ref[...] = jnp.zeros_like(zero_ref)
        nu = nu_ref[0]

        def fill(fn):
            def per_expert(e, carry):
                @pl.when(pd_ref[e] > 0)
                def _():
                    fn(zero_copy((pe_ref[e] - bm) * s_rows))
                return carry
            lax.fori_loop(0, n_exp, per_expert, 0)

            def per_tail(b, carry):
                fn(zero_copy(b * zrows))
                return carry
            lax.fori_loop(nu, n_blocks, per_tail, 0)

        fill(lambda cp: cp.start())
        fill(lambda cp: cp.wait())
        idx_copy(0, 0).start()
        idx_copy(0, 0).wait()
        if n_steps > 1:
            idx_copy(1, 1).start()

    buf = i % 2

    @pl.when(i > 0)
    def _():
        idx_copy(i, buf).wait()

    def body(r, carry):
        src = hp_ref.at[pl.ds(pl.multiple_of(r * s_rows, s_rows), s_rows), :]
        for k in range(TOP_K):
            t = idx_ref[buf, r * TOP_K + k]
            pltpu.make_async_copy(src, xs_hbm.at[pl.ds(pl.multiple_of(t * s_rows, s_rows), s_rows), :], rsem).start()
        return carry

    lax.fori_loop(0, tt, body, 0)

    @pl.when((i > 0) & (i + 1 < n_steps))
    def _():
        idx_copy(i + 1, 1 - buf).start()

    for _ in range(TOP_K):
        pltpu.make_async_copy(hp_ref, xs_hbm.at[pl.ds(0, tt * s_rows), :], rsem).wait()


def _dispatch(hp, slots, pad_end, padded, n_used, n_slots, bm):
    rows, _ = hp.shape
    n_steps, per_tile = slots.shape
    tt = per_tile // TOP_K
    s_rows = rows // (n_steps * tt)
    n_exp = pad_end.shape[0]
    grid_spec = pltpu.PrefetchScalarGridSpec(
        num_scalar_prefetch=3,
        grid=(n_steps,),
        in_specs=[pl.BlockSpec(memory_space=pl.ANY),
                  pl.BlockSpec((tt * s_rows, LANES), lambda i, pe, pd, nu: (i, 0))],
        out_specs=pl.BlockSpec(memory_space=pl.ANY),
        scratch_shapes=[pltpu.SMEM((2, per_tile), jnp.int32),
                        pltpu.VMEM((bm * s_rows, LANES), BF16),
                        pltpu.SemaphoreType.DMA((2,)),
                        pltpu.SemaphoreType.DMA(()),
                        pltpu.SemaphoreType.DMA(())])
    return pl.pallas_call(
        functools.partial(_dispatch_kernel, n_steps=n_steps, n_exp=n_exp, n_blocks=n_slots // bm,
                          s_rows=s_rows, bm=bm),
        grid_spec=grid_spec,
        out_shape=jax.ShapeDtypeStruct((n_slots * s_rows, LANES), BF16),
        compiler_params=pltpu.CompilerParams(vmem_limit_bytes=VMEM_LIMIT, disable_bounds_checks=True),
    )(pad_end, padded, n_used, slots, hp)


def _moe_kernel(be_ref, nu_ref, x_ref, wg_ref, wu_ref, wd_ref, y_ref, wgb, wub, wdb, scr_ref, *, s_rows):
    i = pl.program_id(0)
    nu = nu_ref[0]
    bm = x_ref.shape[0] // s_rows

    @pl.when(i < nu)
    def _():
        @pl.when((i == 0) | (be_ref[i] != be_ref[jnp.maximum(i - 1, 0)]))
        def _():
            wgb[...] = wg_ref[0, 0].astype(BF16)
            wub[...] = wu_ref[0, 0].astype(BF16)
            wdb[...] = wd_ref[0, 0].astype(BF16)

        scr_ref[...] = x_ref[...].astype(F32)
        gate = up = None
        for c in range(s_rows // 2):
            xc = jnp.concatenate([_from_token_rows(scr_ref, s_rows, bm, 2 * c),
                                  _from_token_rows(scr_ref, s_rows, bm, 2 * c + 1)], axis=1).astype(BF16)
            rows = slice(2 * c * LANES, (2 * c + 2) * LANES)
            gc = jnp.dot(xc, wgb[rows, :], preferred_element_type=F32)
            uc = jnp.dot(xc, wub[rows, :], preferred_element_type=F32)
            gate = gc if gate is None else gate + gc
            up = uc if up is None else up + uc
        a = (_silu(gate) * up).astype(BF16)
        for c in range(s_rows // 2):
            yc = jnp.dot(a, wdb[:, 2 * c * LANES:(2 * c + 2) * LANES], preferred_element_type=F32)
            scr_ref[pl.ds(2 * c, bm, stride=s_rows), :] = yc[:, :LANES]
            scr_ref[pl.ds(2 * c + 1, bm, stride=s_rows), :] = yc[:, LANES:]
        y_ref[...] = scr_ref[...].astype(BF16)

    @pl.when(i >= nu)
    def _():
        y_ref[...] = jnp.zeros_like(y_ref)


def _moe_experts(xs, blk_expert, n_used, wg, wu, wd, layer, bm):
    _, n_exp, d, de = wg.shape
    s_rows = d // LANES
    nb = xs.shape[0] // (bm * s_rows)
    blk = pl.BlockSpec((bm * s_rows, LANES), lambda i, be, nu: (jnp.minimum(i, nu[0] - 1), 0))
    grid_spec = pltpu.PrefetchScalarGridSpec(
        num_scalar_prefetch=2,
        grid=(nb,),
        in_specs=[blk,
                  pl.BlockSpec((1, 1, d, de), lambda i, be, nu: (layer, be[i], 0, 0)),
                  pl.BlockSpec((1, 1, d, de), lambda i, be, nu: (layer, be[i], 0, 0)),
                  pl.BlockSpec((1, 1, de, d), lambda i, be, nu: (layer, be[i], 0, 0))],
        out_specs=pl.BlockSpec((bm * s_rows, LANES), lambda i, be, nu: (i, 0)),
        scratch_shapes=[pltpu.VMEM((d, de), BF16), pltpu.VMEM((d, de), BF16), pltpu.VMEM((de, d), BF16),
                        pltpu.VMEM((bm * s_rows, LANES), F32)])
    return pl.pallas_call(
        functools.partial(_moe_kernel, s_rows=s_rows),
        grid_spec=grid_spec,
        out_shape=jax.ShapeDtypeStruct(xs.shape, BF16),
        compiler_params=pltpu.CompilerParams(vmem_limit_bytes=VMEM_LIMIT),
    )(blk_expert, n_used, xs, wg, wu, wd)


def _combine_kernel(slot_hbm, y_hbm, gate_ref, h2_ref, x1_ref, sg_ref, su_ref, sd_ref, g2_ref, lng_ref, lnb_ref,
                    out_ref, idx_ref, ybuf, f_ref, frow_ref, isem, rsem, *, order, n_steps, s_rows):
    i = pl.program_id(0)
    tt, d = h2_ref.shape
    n_rows = TOP_K * tt

    def idx_copy(tile, buf):
        return pltpu.make_async_copy(slot_hbm.at[tile], idx_ref.at[buf], isem.at[buf])

    def issue_token(buf, r):
        for k in range(TOP_K):
            j = k * tt + r
            t = idx_ref[buf, j]
            pltpu.make_async_copy(y_hbm.at[pl.ds(pl.multiple_of(t * s_rows, s_rows), s_rows), :],
                                  ybuf.at[buf, pl.ds(pl.multiple_of(j * s_rows, s_rows), s_rows), :],
                                  rsem.at[buf]).start()

    @pl.when(i == 0)
    def _():
        idx_copy(0, 0).start()
        idx_copy(0, 0).wait()

        def first(r, carry):
            issue_token(0, r)
            return carry
        lax.fori_loop(0, tt, first, 0)
        if n_steps > 1:
            idx_copy(1, 1).start()

    buf = i % 2
    nxt = 1 - buf
    has_next = i + 1 < n_steps

    @pl.when(has_next)
    def _():
        idx_copy(i + 1, nxt).wait()

    hb = h2_ref[...].astype(BF16)
    a = (_silu(jnp.dot(hb, sg_ref[...], preferred_element_type=F32))
         * jnp.dot(hb, su_ref[...], preferred_element_type=F32))
    f_ref[...] = jnp.dot(a.astype(BF16), sd_ref[...], preferred_element_type=F32)
    pltpu.make_async_copy(y_hbm.at[pl.ds(0, n_rows * s_rows), :], ybuf.at[buf], rsem.at[buf]).wait()

    def token_loop(prefetch):
        def body(r, carry):
            if prefetch:
                issue_token(nxt, r)
            gates = gate_ref[pl.ds(pl.multiple_of(r * TOP_K, TOP_K), TOP_K), :]
            acc = jnp.zeros((s_rows, LANES), F32)
            for k in range(TOP_K):
                rows = pl.ds(pl.multiple_of((k * tt + r) * s_rows, s_rows), s_rows)
                acc = acc + jnp.broadcast_to(gates[k:k + 1, :], (s_rows, LANES)) * ybuf[buf, rows, :].astype(F32)
            frow_ref[pl.ds(pl.multiple_of(r * s_rows, s_rows), s_rows), :] = acc
            return carry
        lax.fori_loop(0, tt, body, 0, unroll=2)

    @pl.when(has_next)
    def _():
        token_loop(True)

    @pl.when(jnp.logical_not(has_next))
    def _():
        token_loop(False)

    @pl.when(i + 2 < n_steps)
    def _():
        idx_copy(i + 2, buf).start()

    for s in range(s_rows):
        sl = slice(s * LANES, (s + 1) * LANES)
        f_ref[:, sl] = f_ref[:, sl] + _from_token_rows(frow_ref, s_rows, tt, s)
    g = order.group(i)
    r = DEEPNORM_ALPHA * x1_ref[...] + g2_ref[0, pl.ds(g, 1), :] * f_ref[...]
    out_ref[...] = _layer_norm(r, lng_ref[...], lnb_ref[...])


def _combine(slots, y_sorted, gate, h2, x1, sg, su, sd, mod, ln_g, ln_b, layer, order):
    n_tok, d = h2.shape
    tt = order.tile
    n_steps = order.n_tiles
    s_rows = d // LANES
    ds_ = sg.shape[1]
    row = pl.BlockSpec((1, d), lambda i: (0, 0))
    tok = pl.BlockSpec((tt, d), lambda i: (i, 0))
    return pl.pallas_call(
        functools.partial(_combine_kernel, order=order, n_steps=n_steps, s_rows=s_rows),
        grid=(n_steps,),
        in_specs=[pl.BlockSpec(memory_space=pl.ANY),
                  pl.BlockSpec(memory_space=pl.ANY),
                  pl.BlockSpec((tt * TOP_K, LANES), lambda i: (i, 0)),
                  tok, tok,
                  pl.BlockSpec((d, ds_), lambda i: (0, 0)),
                  pl.BlockSpec((d, ds_), lambda i: (0, 0)),
                  pl.BlockSpec((ds_, d), lambda i: (0, 0)),
                  _mod_spec(layer, MOD_G2, d, 1), row, row],
        out_specs=tok,
        out_shape=jax.ShapeDtypeStruct((n_tok, d), F32),
        scratch_shapes=[pltpu.SMEM((2, TOP_K * tt), jnp.int32),
                        pltpu.VMEM((2, TOP_K * tt * s_rows, LANES), BF16),
                        pltpu.VMEM((tt, d), F32),
                        pltpu.VMEM((tt * s_rows, LANES), F32),
                        pltpu.SemaphoreType.DMA((2,)),
                        pltpu.SemaphoreType.DMA((2,))],
        compiler_params=pltpu.CompilerParams(vmem_limit_bytes=VMEM_LIMIT, disable_bounds_checks=True),
    )(slots, y_sorted, gate, h2, x1, sg, su, sd, mod, ln_g.reshape(1, d), ln_b.reshape(1, d))


def _moe_layer(x1, h2, hp, layer, mod, order, router_w, router_b, wg, wu, wd, sg, su, sd, ln_g, ln_b):
    n_tok, d = h2.shape
    n_exp = router_w.shape[1]
    bm = MOE_BLOCK
    tt = order.tile
    eidx_t, gate_t, rank_t, cnt = _router(h2, router_w, router_b)

    counts = cnt[:, 0].astype(jnp.int32)
    padded = (counts + bm - 1) // bm * bm
    pad_end = jnp.cumsum(padded).astype(jnp.int32)
    pad_start = pad_end - padded
    experts = jnp.arange(n_exp, dtype=jnp.int32)
    slot_t = rank_t + jnp.sum(jnp.where(eidx_t[:, :, None] == experts, pad_start, 0), axis=-1)
    n_slots = n_tok * TOP_K + n_exp * bm
    nb = n_slots // bm
    n_used = pad_end[-1:] // bm
    blk_expert = jnp.minimum(jnp.sum(pad_end[None, :] <= (jnp.arange(nb, dtype=jnp.int32) * bm)[:, None], axis=1),
                             n_exp - 1).astype(jnp.int32)
    n_tiles = n_tok // tt
    slots_tok = slot_t.T.reshape(n_tiles, tt * TOP_K)
    slots_k = slot_t.reshape(TOP_K, n_tiles, tt).transpose(1, 0, 2).reshape(n_tiles, TOP_K * tt)

    xs = _dispatch(hp, slots_tok, pad_end, padded, n_used, n_slots, bm)
    ys = _moe_experts(xs, blk_expert, n_used, wg, wu, wd, layer, bm)
    gates = jnp.broadcast_to(gate_t.T.reshape(n_tok * TOP_K, 1), (n_tok * TOP_K, LANES))
    return _combine(slots_k, ys, gates, h2, x1, sg[layer].astype(BF16), su[layer].astype(BF16),
                    sd[layer].astype(BF16), mod, ln_g[layer], ln_b[layer], layer, order)


def kernel(x, c, ctx, c_ctx, ada_w, ada_b, ln_mix_g, ln_mix_b, ln_ffn_g, ln_ffn_b, ret_w_in, ret_w_out,
           ret_decay_fwd, ret_decay_bwd, pool_w, pool_scale, router_w, router_b, exp_w_gate, exp_w_up,
           exp_w_down, sh_w_gate, sh_w_up, sh_w_down):
    batch, seq, d = x.shape
    ctx_len = ctx.shape[1]
    heads = RET_HEADS
    qk_dim = d
    head_dim = qk_dim // heads
    v_dim = ret_w_out.shape[1]
    assert ada_w.shape[0] == DEPTH == 2 and batch + 1 <= MOD_ROWS

    cvec = jnp.zeros((MOD_ROWS, d), F32).at[:batch].set(c.astype(F32)).at[batch].set(c_ctx.astype(F32))
    mod = _modulation(cvec, ada_w.astype(F32), ada_b.astype(F32))

    x2d = x.reshape(batch * seq, d)
    ctx2d = ctx.reshape(batch * ctx_len, d)
    order0 = _Order(batch, seq, ctx_len, TOKEN_TILE)

    w_in = ret_w_in[0].astype(F32)
    tabs = _rope_tables(seq, head_dim, TOKEN_TILE)
    qk = _inproj(x2d, ctx2d, mod, w_in, 0, 2 * qk_dim, order0, BF16, rope_tabs=tabs, k_col0=qk_dim,
                 head_dim=head_dim)
    v = _inproj(x2d, ctx2d, mod, w_in, 2 * qk_dim, v_dim, order0, BF16)
    g = _inproj(x2d, ctx2d, mod, w_in, 2 * qk_dim + v_dim, v_dim, order0, F32)
    og = _retention(qk, v, g, ret_decay_fwd[0], ret_decay_bwd[0], batch, ctx_len + seq, ctx_len)
    x1, h2, hp = _outproj(og, x2d, ctx2d, ret_w_out[0].astype(BF16), mod, ln_mix_g[0], ln_mix_b[0], order0)
    xa = _moe_layer(x1, h2, hp, 0, mod, order0, router_w[0], router_b[0], exp_w_gate, exp_w_up,
                    exp_w_down, sh_w_gate, sh_w_up, sh_w_down, ln_ffn_g, ln_ffn_b)

    order1 = _Order(batch, seq, 0, TOKEN_TILE)
    x1, h2, hp = _pool_layer(xa, order0, pool_w[0], pool_scale[0], mod, ln_mix_g[1], ln_mix_b[1], 1, batch, seq)
    xb = _moe_layer(x1, h2, hp, 1, mod, order1, router_w[1], router_b[1], exp_w_gate, exp_w_up,
                    exp_w_down, sh_w_gate, sh_w_up, sh_w_down, ln_ffn_g, ln_ffn_b)
    return xb.reshape(batch, seq, d)
```

```python
import functools

import jax
import jax.numpy as jnp
from jax import lax
from jax.experimental import pallas as pl
from jax.experimental.pallas import tpu as pltpu

F32 = jnp.float32
BF16 = jnp.bfloat16
U32 = jnp.uint32
HIGHEST = lax.Precision.HIGHEST

DEPTH = 2
GRID_W = 64
RET_HEADS = 8
ROPE_BASE = 10000.0
POOL_WINDOWS = (2, 4, 8, 16)
N_EXPERT_GROUPS = 8
TOPK_GROUPS = 4
TOP_K = 8
ROUTED_SCALE = 2.5
LN_EPS = 1e-5
N_MOD = 6
DEEPNORM_ALPHA = (2.0 * DEPTH) ** 0.25
MOD_SH1, MOD_SC1, MOD_G1, MOD_SH2, MOD_SC2, MOD_G2 = range(N_MOD)

LANES = 128
TOKEN_TILE = 256
RET_BLOCK = 768
RET_CHUNK = 256
ROUTER_TILE = 512
MOE_BLOCK = 512
MOD_ROWS = 8
VMEM_LIMIT = 56 * 1024 * 1024


def _silu(x):
    return x * jax.nn.sigmoid(x)


def _layer_norm(r, g, b):
    mu = jnp.mean(r, axis=-1, keepdims=True)
    var = jnp.mean(jnp.square(r - mu), axis=-1, keepdims=True)
    return (r - mu) * lax.rsqrt(var + LN_EPS) * g + b


def _to_token_rows(scr_ref, h):
    tm, d = h.shape
    s_rows = d // LANES
    for s in range(s_rows):
        scr_ref[pl.ds(s, tm, stride=s_rows), :] = h[:, s * LANES:(s + 1) * LANES]
    return scr_ref[...].astype(BF16)


def _from_token_rows(scr_ref, rows, n, s):
    return scr_ref[pl.ds(s, n, stride=rows), :]


class _Order:
    def __init__(self, batch, seq, ctx_len, tile):
        assert seq % tile == 0 and ctx_len % tile == 0
        self.batch, self.seq, self.ctx_len, self.tile = batch, seq, ctx_len, tile
        self.ctx_tiles = ctx_len // tile
        self.lat_tiles = seq // tile
        self.per_batch = self.ctx_tiles + self.lat_tiles
        self.n_tiles = batch * self.per_batch
        self.n_tok = self.n_tiles * tile

    def split(self, i):
        b = i // self.per_batch
        r = i % self.per_batch
        return b, r, r < self.ctx_tiles

    def group(self, i):
        b, _, is_ctx = self.split(i)
        return jnp.where(is_ctx, self.batch, b)

    def lat_block(self, i):
        b, r, _ = self.split(i)
        return b * self.lat_tiles + jnp.maximum(r - self.ctx_tiles, 0)

    def ctx_block(self, i):
        b, r, _ = self.split(i)
        return b * self.ctx_tiles + jnp.minimum(r, max(self.ctx_tiles - 1, 0))


def _mod_kernel(c_ref, w_ref, b_ref, o_ref):
    s = _silu(c_ref[...])
    o_ref[0] = jnp.dot(s, w_ref[0], precision=HIGHEST, preferred_element_type=F32) + b_ref[0]


def _modulation(cvec, ada_w, ada_b):
    depth, d, n = ada_w.shape
    tn = 1024
    return pl.pallas_call(
        _mod_kernel,
        grid=(depth, n // tn),
        in_specs=[pl.BlockSpec((MOD_ROWS, d), lambda l, j: (0, 0)),
                  pl.BlockSpec((1, d, tn), lambda l, j: (l, 0, j)),
                  pl.BlockSpec((1, 1, tn), lambda l, j: (l, 0, j))],
        out_specs=pl.BlockSpec((1, MOD_ROWS, tn), lambda l, j: (l, 0, j)),
        out_shape=jax.ShapeDtypeStruct((depth, MOD_ROWS, n), F32),
        compiler_params=pltpu.CompilerParams(vmem_limit_bytes=VMEM_LIMIT),
    )(cvec, ada_w, ada_b.reshape(depth, 1, n))


def _mod_spec(layer, chunk, d, n_grid_args):
    if n_grid_args == 1:
        return pl.BlockSpec((1, MOD_ROWS, d), lambda i: (layer, 0, chunk))
    return pl.BlockSpec((1, MOD_ROWS, d), lambda j, i: (layer, 0, chunk))


def _inproj_kernel(x_ref, c_ref, sc_ref, sh_ref, w_ref, *rest, order, rope, k_col0, head_dim):
    i = pl.program_id(1)
    j = pl.program_id(0)
    if rope:
        cos_ref, sin_ref, o_ref, wb_ref = rest
    else:
        o_ref, wb_ref = rest

    @pl.when(i == 0)
    def _():
        wb_ref[...] = w_ref[...].astype(BF16)

    g = order.group(i)
    _, _, is_ctx = order.split(i)
    sc = sc_ref[0, pl.ds(g, 1), :]
    sh = sh_ref[0, pl.ds(g, 1), :]
    xin = jnp.where(is_ctx, c_ref[...], x_ref[...])
    a = (xin * (1.0 + sc) + sh).astype(BF16)
    acc = jnp.dot(a, wb_ref[...], preferred_element_type=F32)
    if not rope:
        o_ref[...] = acc.astype(o_ref.dtype)
        return
    tn = acc.shape[1]
    lane = lax.broadcasted_iota(jnp.int32, acc.shape, 1)
    partner = jnp.where((lane & 1) == 0, pltpu.roll(acc, tn - 1, 1), pltpu.roll(acc, 1, 1))
    cos = cos_ref[...]
    sin = sin_ref[...]
    scale = jnp.where(j * tn >= k_col0, head_dim ** -0.5, 1.0).astype(F32)
    for h in range(tn // head_dim):
        sl = slice(h * head_dim, (h + 1) * head_dim)
        o_ref[:, sl] = ((acc[:, sl] * cos + partner[:, sl] * sin) * scale).astype(o_ref.dtype)


def _inproj(x2d, ctx2d, mod, w_in, col0, n, order, out_dtype, rope_tabs=None, k_col0=0, head_dim=0):
    d = w_in.shape[0]
    tm = order.tile
    tn = min(n, 2048)
    assert col0 % tn == 0 and n % tn == 0
    rope = rope_tabs is not None
    in_specs = [pl.BlockSpec((tm, d), lambda j, i: (order.lat_block(i), 0)),
                pl.BlockSpec((tm, d), lambda j, i: (order.ctx_block(i), 0)),
                _mod_spec(0, MOD_SC1, d, 2), _mod_spec(0, MOD_SH1, d, 2),
                pl.BlockSpec((d, tn), lambda j, i: (0, col0 // tn + j), pipeline_mode=pl.Buffered(1))]
    args = [x2d, ctx2d, mod, mod, w_in]
    if rope:
        def tab_map(j, i):
            _, r, is_ctx = order.split(i)
            return (jnp.where(is_ctx, order.lat_tiles, r - order.ctx_tiles), 0)
        in_specs += [pl.BlockSpec((tm, head_dim), tab_map), pl.BlockSpec((tm, head_dim), tab_map)]
        args += list(rope_tabs)
    return pl.pallas_call(
        functools.partial(_inproj_kernel, order=order, rope=rope, k_col0=k_col0, head_dim=head_dim),
        grid=(n // tn, order.n_tiles),
        in_specs=in_specs,
        out_specs=pl.BlockSpec((tm, tn), lambda j, i: (i, j)),
        out_shape=jax.ShapeDtypeStruct((order.n_tok, n), out_dtype),
        scratch_shapes=[pltpu.VMEM((d, tn), BF16)],
        compiler_params=pltpu.CompilerParams(vmem_limit_bytes=VMEM_LIMIT),
    )(*args)


def _rope_tables(seq, head_dim, tile):
    n_freq = head_dim // 4
    pos = jnp.arange(seq, dtype=jnp.int32)
    row = (pos // GRID_W).astype(F32)
    col = (pos % GRID_W).astype(F32)
    freqs = ROPE_BASE ** (-jnp.arange(n_freq, dtype=F32) / n_freq)
    ang = jnp.concatenate([row[:, None] * freqs, col[:, None] * freqs], axis=-1)
    cos = jnp.repeat(jnp.cos(ang), 2, axis=-1)
    sin = jnp.stack([-jnp.sin(ang), jnp.sin(ang)], axis=-1).reshape(seq, head_dim)
    ident = jnp.zeros((tile, head_dim), F32)
    return jnp.concatenate([cos, ident + 1.0], axis=0), jnp.concatenate([sin, ident], axis=0)


def _ret_kernel(q_ref, k_ref, v_ref, g_ref, df_ref, db_ref, o_ref, sf_ref, sb_ref, obuf_ref, *,
                n_blocks, chunks_per_block, ctx_chunks):
    C = RET_CHUNK
    NB, CPB, NCC = n_blocks, chunks_per_block, ctx_chunks
    st = pl.program_id(2)
    phase = st // (NB + 1)
    s = st % (NB + 1)
    ldf = jax.nn.log_sigmoid(df_ref[0])[:, :1]
    ldb = jax.nn.log_sigmoid(db_ref[0])[:, :1]
    pos = lax.broadcasted_iota(jnp.int32, (C, 1), 0).astype(F32)
    tdims = (((0,), (0,)), ((), ()))

    @pl.when(phase == 0)
    def _backward_states():
        @pl.when(s == 0)
        def _():
            sb_ref[...] = jnp.zeros_like(sb_ref)

        q_dec = jnp.exp(ldb * (C - pos))
        k_dec = jnp.exp(ldb * pos)
        chunk_dec = jnp.exp(ldb * C)

        def chunk(blk, c):
            rows = pl.ds(c * C, C)
            q = q_ref[rows, :].astype(F32)
            k = k_ref[rows, :].astype(F32)
            v = v_ref[rows, :]
            state = sb_ref[...]
            goff = pl.multiple_of((blk * CPB + c) * C, C)
            obuf_ref[pl.ds(goff, C), :] = jnp.dot((q * q_dec).astype(BF16), state.astype(BF16),
                                                  preferred_element_type=F32)
            sb_ref[...] = state * chunk_dec + lax.dot_general((k * k_dec).astype(BF16), v, tdims,
                                                              preferred_element_type=F32)

        @pl.when(s == 0)
        def _():
            for c in reversed(range(NCC)):
                chunk(0, c)

        @pl.when((s > 0) & (s < NB))
        def _():
            for c in reversed(range(CPB)):
                chunk(NB - s, c)

        @pl.when(s == NB)
        def _():
            for c in reversed(range(NCC, CPB)):
                chunk(0, c)

    @pl.when((phase == 1) & (s < NB))
    def _forward_and_output():
        @pl.when(s == 0)
        def _():
            sf_ref[...] = jnp.zeros_like(sf_ref)

        q_dec = jnp.exp(ldf * (pos + 1.0))
        k_dec = jnp.exp(ldf * (C - 1.0 - pos))
        chunk_dec = jnp.exp(ldf * C)
        ii = lax.broadcasted_iota(jnp.int32, (C, C), 0)
        jj = lax.broadcasted_iota(jnp.int32, (C, C), 1)
        rel = (ii - jj).astype(F32)
        decay = jnp.where(rel >= 0, jnp.exp(ldf * jnp.maximum(rel, 0.0)), jnp.exp(ldb * jnp.maximum(-rel, 0.0)))

        for c in range(CPB):
            rows = pl.ds(c * C, C)
            qb = q_ref[rows, :]
            kb = k_ref[rows, :]
            v = v_ref[rows, :]
            q = qb.astype(F32)
            k = kb.astype(F32)
            state = sf_ref[...]
            scores = lax.dot_general(qb, kb, (((1,), (1,)), ((), ())), preferred_element_type=F32) * decay
            goff = pl.multiple_of((s * CPB + c) * C, C)
            o = (jnp.dot(scores.astype(BF16), v, preferred_element_type=F32)
                 + jnp.dot((q * q_dec).astype(BF16), state.astype(BF16), preferred_element_type=F32)
                 + obuf_ref[pl.ds(goff, C), :])
            sf_ref[...] = state * chunk_dec + lax.dot_general((k * k_dec).astype(BF16), v, tdims,
                                                              preferred_element_type=F32)
            mu = jnp.mean(o, axis=-1, keepdims=True)
            var = jnp.mean(jnp.square(o - mu), axis=-1, keepdims=True)
            on = (o - mu) * lax.rsqrt(var + LN_EPS)
            o_ref[rows, :] = (_silu(g_ref[rows, :]) * on).astype(o_ref.dtype)


def _retention(qk, v, g, raw_fwd, raw_bwd, batch, tokens_per_batch, ctx_len):
    n_tok, qk2 = qk.shape
    v_dim = v.shape[1]
    heads = RET_HEADS
    dk = qk2 // 2 // heads
    dv = v_dim // heads
    tb = RET_BLOCK
    assert tokens_per_batch % tb == 0 and tb % RET_CHUNK == 0 and ctx_len % RET_CHUNK == 0 and ctx_len <= tb
    nb = tokens_per_batch // tb
    steps = 2 * (nb + 1)

    def blk(st):
        phase = st // (nb + 1)
        s = st % (nb + 1)
        back = jnp.where(s == 0, 0, nb - s)
        return jnp.where(phase == 0, back, jnp.minimum(s, nb - 1))

    def fwd_blk(st):
        phase = st // (nb + 1)
        s = st % (nb + 1)
        return jnp.where(phase == 0, 0, jnp.minimum(s, nb - 1))

    dec = lambda raw: jnp.broadcast_to(raw.astype(F32)[:, None, None], (heads, 1, LANES))
    return pl.pallas_call(
        functools.partial(_ret_kernel, n_blocks=nb, chunks_per_block=tb // RET_CHUNK,
                          ctx_chunks=ctx_len // RET_CHUNK),
        grid=(batch, heads, steps),
        in_specs=[pl.BlockSpec((tb, dk), lambda b, h, st: (b * nb + blk(st), h)),
                  pl.BlockSpec((tb, dk), lambda b, h, st: (b * nb + blk(st), heads + h)),
                  pl.BlockSpec((tb, dv), lambda b, h, st: (b * nb + blk(st), h)),
                  pl.BlockSpec((tb, dv), lambda b, h, st: (b * nb + fwd_blk(st), h)),
                  pl.BlockSpec((1, 1, LANES), lambda b, h, st: (h, 0, 0)),
                  pl.BlockSpec((1, 1, LANES), lambda b, h, st: (h, 0, 0))],
        out_specs=pl.BlockSpec((tb, dv), lambda b, h, st: (b * nb + fwd_blk(st), h)),
        out_shape=jax.ShapeDtypeStruct((n_tok, v_dim), BF16),
        scratch_shapes=[pltpu.VMEM((dk, dv), F32), pltpu.VMEM((dk, dv), F32),
                        pltpu.VMEM((tokens_per_batch, dv), F32)],
        compiler_params=pltpu.CompilerParams(vmem_limit_bytes=VMEM_LIMIT),
    )(qk, qk, v, g, dec(raw_fwd), dec(raw_bwd))


def _outproj_kernel(o_ref, x_ref, c_ref, w_ref, g1_ref, sc2_ref, sh2_ref, lng_ref, lnb_ref,
                    x1_ref, h2_ref, hp_ref, scr_ref, *, order):
    i = pl.program_id(0)
    g = order.group(i)
    _, _, is_ctx = order.split(i)
    y = jnp.dot(o_ref[...], w_ref[...], preferred_element_type=F32)
    xin = jnp.where(is_ctx, c_ref[...], x_ref[...])
    r = DEEPNORM_ALPHA * xin + g1_ref[0, pl.ds(g, 1), :] * y
    x1 = _layer_norm(r, lng_ref[...], lnb_ref[...])
    x1_ref[...] = x1
    h2 = x1 * (1.0 + sc2_ref[0, pl.ds(g, 1), :]) + sh2_ref[0, pl.ds(g, 1), :]
    h2_ref[...] = h2
    hp_ref[...] = _to_token_rows(scr_ref, h2)


def _outproj(og, x2d, ctx2d, w_out, mod, ln_g, ln_b, order):
    v_dim, d = w_out.shape
    tm = order.tile
    s_rows = d // LANES
    row = pl.BlockSpec((1, d), lambda i: (0, 0))
    tok = pl.BlockSpec((tm, d), lambda i: (i, 0))
    return pl.pallas_call(
        functools.partial(_outproj_kernel, order=order),
        grid=(order.n_tiles,),
        in_specs=[pl.BlockSpec((tm, v_dim), lambda i: (i, 0)),
                  pl.BlockSpec((tm, d), lambda i: (order.lat_block(i), 0)),
                  pl.BlockSpec((tm, d), lambda i: (order.ctx_block(i), 0)),
                  pl.BlockSpec((v_dim, d), lambda i: (0, 0), pipeline_mode=pl.Buffered(1)),
                  _mod_spec(0, MOD_G1, d, 1), _mod_spec(0, MOD_SC2, d, 1), _mod_spec(0, MOD_SH2, d, 1),
                  row, row],
        out_specs=[tok, tok, pl.BlockSpec((tm * s_rows, LANES), lambda i: (i, 0))],
        out_shape=[jax.ShapeDtypeStruct((order.n_tok, d), F32)] * 2
                  + [jax.ShapeDtypeStruct((order.n_tok * s_rows, LANES), BF16)],
        scratch_shapes=[pltpu.VMEM((tm * s_rows, LANES), F32)],
        compiler_params=pltpu.CompilerParams(vmem_limit_bytes=VMEM_LIMIT),
    )(og, x2d, ctx2d, w_out, mod, mod, mod, ln_g.reshape(1, d), ln_b.reshape(1, d))


def _pool_kernel(x_ref, pw_ref, ps_ref, sc1_ref, sh1_ref, g1_ref, sc2_ref, sh2_ref, lng_ref, lnb_ref,
                 x1_ref, h2_ref, hp_ref, y_ref, scr_ref, *, lat_tiles):
    i = pl.program_id(0)
    g = i // lat_tiles
    x = x_ref[...]
    tm, d = x.shape
    gd = d // len(POOL_WINDOWS)
    h = x * (1.0 + sc1_ref[0, pl.ds(g, 1), :]) + sh1_ref[0, pl.ds(g, 1), :]
    ni = lax.broadcasted_iota(jnp.int32, (tm, tm), 0)
    mi = lax.broadcasted_iota(jnp.int32, (tm, tm), 1)
    same_row = (ni // GRID_W) == (mi // GRID_W)
    delta = mi - ni
    col = lax.broadcasted_iota(jnp.int32, (tm, 1), 0) % GRID_W
    for gi, w in enumerate(POOL_WINDOWS):
        window = jnp.where(same_row & (delta >= -(w // 2)) & (delta < w // 2), 1.0, 0.0).astype(F32)
        count = (jnp.minimum(col + w // 2, GRID_W) - jnp.maximum(col - w // 2, 0)).astype(F32)
        hg = h[:, gi * gd:(gi + 1) * gd]
        total = jnp.dot(window, hg, precision=HIGHEST, preferred_element_type=F32)
        p = total / count - hg
        y_ref[:, gi * gd:(gi + 1) * gd] = jnp.dot(p.astype(BF16), pw_ref[gi], preferred_element_type=F32)
    y = y_ref[...] * ps_ref[...]
    r = DEEPNORM_ALPHA * x + g1_ref[0, pl.ds(g, 1), :] * y
    x1 = _layer_norm(r, lng_ref[...], lnb_ref[...])
    x1_ref[...] = x1
    h2 = x1 * (1.0 + sc2_ref[0, pl.ds(g, 1), :]) + sh2_ref[0, pl.ds(g, 1), :]
    h2_ref[...] = h2
    hp_ref[...] = _to_token_rows(scr_ref, h2)


def _pool_layer(x_prev, order0, pool_w, pool_scale, mod, ln_g, ln_b, layer, batch, seq):
    d = x_prev.shape[1]
    tm = order0.tile
    s_rows = d // LANES
    lat_tiles = seq // tm
    n_tiles = batch * lat_tiles
    ng, gd, _ = pool_w.shape

    def src(i):
        b = i // lat_tiles
        return (b * order0.per_batch + order0.ctx_tiles + i % lat_tiles, 0)

    row = pl.BlockSpec((1, d), lambda i: (0, 0))
    tok = pl.BlockSpec((tm, d), lambda i: (i, 0))
    ms = lambda chunk: _mod_spec(layer, chunk, d, 1)
    return pl.pallas_call(
        functools.partial(_pool_kernel, lat_tiles=lat_tiles),
        grid=(n_tiles,),
        in_specs=[pl.BlockSpec((tm, d), src),
                  pl.BlockSpec((ng, gd, gd), lambda i: (0, 0, 0)),
                  row, ms(MOD_SC1), ms(MOD_SH1), ms(MOD_G1), ms(MOD_SC2), ms(MOD_SH2), row, row],
        out_specs=[tok, tok, pl.BlockSpec((tm * s_rows, LANES), lambda i: (i, 0))],
        out_shape=[jax.ShapeDtypeStruct((n_tiles * tm, d), F32)] * 2
                  + [jax.ShapeDtypeStruct((n_tiles * tm * s_rows, LANES), BF16)],
        scratch_shapes=[pltpu.VMEM((tm, d), F32), pltpu.VMEM((tm * s_rows, LANES), F32)],
        compiler_params=pltpu.CompilerParams(vmem_limit_bytes=VMEM_LIMIT),
    )(x_prev, pool_w.astype(BF16), pool_scale.reshape(1, d), mod, mod, mod, mod, mod,
      ln_g.reshape(1, d), ln_b.reshape(1, d))


def _router_kernel(h_ref, w_ref, b_ref, eidx_ref, gate_ref, rank_ref, cnt_ref, carry_ref):
    i = pl.program_id(0)
    n_exp = w_ref.shape[0]
    tr = h_ref.shape[0]
    per = n_exp // N_EXPERT_GROUPS
    neg = -jnp.inf

    @pl.when(i == 0)
    def _():
        carry_ref[...] = jnp.zeros_like(carry_ref)

    logits = lax.dot_general(w_ref[...], h_ref[...], (((1,), (1,)), ((), ())),
                             precision=HIGHEST, preferred_element_type=F32)
    scores = jax.nn.sigmoid(logits)
    biased = scores + b_ref[...]

    sub_idx = lax.broadcasted_iota(jnp.int32, (per, tr), 0)
    group_score = []
    for gi in range(N_EXPERT_GROUPS):
        sub = biased[gi * per:(gi + 1) * per, :]
        m1 = jnp.max(sub, axis=0, keepdims=True)
        first = jnp.min(jnp.where(sub == m1, sub_idx, per), axis=0, keepdims=True)
        m2 = jnp.max(jnp.where(sub_idx == first, neg, sub), axis=0, keepdims=True)
        group_score.append(m1 + m2)
    masked = []
    for gi in range(N_EXPERT_GROUPS):
        ahead = jnp.zeros((1, tr), jnp.int32)
        for gj in range(N_EXPERT_GROUPS):
            if gj == gi:
                continue
            wins = (group_score[gj] >= group_score[gi]) if gj < gi else (group_score[gj] > group_score[gi])
            ahead = ahead + wins.astype(jnp.int32)
        keep = ahead < TOPK_GROUPS
        masked.append(jnp.where(keep, biased[gi * per:(gi + 1) * per, :], neg))
    masked = jnp.concatenate(masked, axis=0)

    eid = lax.broadcasted_iota(jnp.int32, (n_exp, tr), 0)
    picks, gates = [], []
    onehot = jnp.zeros((n_exp, tr), F32)
    for _ in range(TOP_K):
        m = jnp.max(masked, axis=0, keepdims=True)
        first = jnp.min(jnp.where(masked == m, eid, n_exp), axis=0, keepdims=True)
        sel = eid == first
        gates.append(jnp.sum(jnp.where(sel, scores, 0.0), axis=0, keepdims=True))
        picks.append(first)
        masked = jnp.where(sel, neg, masked)
        onehot = onehot + sel.astype(F32)
    gate_sum = gates[0]
    for gk in gates[1:]:
        gate_sum = gate_sum + gk

    si = lax.broadcasted_iota(jnp.int32, (tr, tr), 0)
    ti = lax.broadcasted_iota(jnp.int32, (tr, tr), 1)
    before = jnp.where(si < ti, 1.0, 0.0).astype(BF16)
    prefix = jnp.dot(onehot.astype(BF16), before, preferred_element_type=F32) + carry_ref[:, :1]
    for k in range(TOP_K):
        eidx_ref[k:k + 1, :] = picks[k]
        gate_ref[k:k + 1, :] = gates[k] / gate_sum * ROUTED_SCALE
        rank_ref[k:k + 1, :] = jnp.sum(jnp.where(eid == picks[k], prefix, 0.0), axis=0,
                                       keepdims=True).astype(jnp.int32)
    carry_ref[...] = carry_ref[...] + jnp.sum(onehot, axis=1, keepdims=True)
    cnt_ref[...] = carry_ref[...]


def _router(h2, router_w, router_b):
    n_tok, d = h2.shape
    n_exp = router_w.shape[1]
    tr = ROUTER_TILE
    assert n_tok % tr == 0
    sel = pl.BlockSpec((TOP_K, tr), lambda i: (0, i))
    return pl.pallas_call(
        _router_kernel,
        grid=(n_tok // tr,),
        in_specs=[pl.BlockSpec((tr, d), lambda i: (i, 0)),
                  pl.BlockSpec((n_exp, d), lambda i: (0, 0)),
                  pl.BlockSpec((n_exp, 1), lambda i: (0, 0))],
        out_specs=[sel, sel, sel, pl.BlockSpec((n_exp, LANES), lambda i: (0, 0))],
        out_shape=[jax.ShapeDtypeStruct((TOP_K, n_tok), jnp.int32),
                   jax.ShapeDtypeStruct((TOP_K, n_tok), F32),
                   jax.ShapeDtypeStruct((TOP_K, n_tok), jnp.int32),
                   jax.ShapeDtypeStruct((n_exp, LANES), F32)],
        scratch_shapes=[pltpu.VMEM((n_exp, LANES), F32)],
        compiler_params=pltpu.CompilerParams(vmem_limit_bytes=VMEM_LIMIT),
    )(h2, router_w.astype(F32).T, router_b.astype(F32).reshape(n_exp, 1))


def _dispatch_kernel(pe_ref, pd_ref, nu_ref, slot_hbm, hp_ref, xs_hbm, idx_ref, zero_ref, isem, zsem, rsem, *,
                     n_steps, n_exp, n_blocks, s_rows, bm):
    i = pl.program_id(0)
    tt = hp_ref.shape[0] // s_rows
    zrows = bm * s_rows
    per_tile = tt * TOP_K

    def idx_copy(tile, buf):
        return pltpu.make_async_copy(slot_hbm.at[tile], idx_ref.at[pl.ds(buf * per_tile, per_tile)], isem.at[buf])

    def zero_copy(row0):
        return pltpu.make_async_copy(zero_ref, xs_hbm.at[pl.ds(pl.multiple_of(row0, s_rows), zrows), :], zsem)

    @pl.when(i == 0)
    def _():
        zero_ref[...] = jnp.zeros_like(zero_ref)
        nu = nu_ref[0]

        def fill(fn):
            def per_expert(e, carry):
                @pl.when(pd_ref[e] > 0)
                def _():
                    fn(zero_copy((pe_ref[e] - bm) * s_rows))
                return carry
            lax.fori_loop(0, n_exp, per_expert, 0)

            def per_tail(b, carry):
                fn(zero_copy(b * zrows))
                return carry
            lax.fori_loop(nu, n_blocks, per_tail, 0)

        fill(lambda cp: cp.start())
        fill(lambda cp: cp.wait())
        idx_copy(0, 0).start()
        idx_copy(0, 0).wait()
        if n_steps > 1:
            idx_copy(1, 1).start()

    buf = i % 2

    @pl.when(i > 0)
    def _():
        idx_copy(i, buf).wait()

    group = 4
    assert tt % group == 0

    def body(gi, carry):
        base = pl.multiple_of(gi * (group * s_rows), group * s_rows)
        ibase = buf * per_tile + gi * (group * TOP_K)
        for u in range(group):
            src = hp_ref.at[pl.ds(base + u * s_rows, s_rows), :]
            for k in range(TOP_K):
                t = idx_ref[ibase + (u * TOP_K + k)]
                pltpu.make_async_copy(src, xs_hbm.at[pl.ds(pl.multiple_of(t * s_rows, s_rows), s_rows), :],
                                      rsem).start()
        return carry

    lax.fori_loop(0, tt // group, body, 0)

    @pl.when((i > 0) & (i + 1 < n_steps))
    def _():
        idx_copy(i + 1, 1 - buf).start()

    for _ in range(TOP_K):
        pltpu.make_async_copy(hp_ref, xs_hbm.at[pl.ds(0, tt * s_rows), :], rsem).wait()


def _dispatch(hp, slots, pad_end, padded, n_used, n_slots, bm):
    rows, _ = hp.shape
    n_steps, per_tile = slots.shape
    tt = per_tile // TOP_K
    s_rows = rows // (n_steps * tt)
    n_exp = pad_end.shape[0]
    grid_spec = pltpu.PrefetchScalarGridSpec(
        num_scalar_prefetch=3,
        grid=(n_steps,),
        in_specs=[pl.BlockSpec(memory_space=pl.ANY),
                  pl.BlockSpec((tt * s_rows, LANES), lambda i, pe, pd, nu: (i, 0))],
        out_specs=pl.BlockSpec(memory_space=pl.ANY),
        scratch_shapes=[pltpu.SMEM((2 * per_tile,), jnp.int32),
                        pltpu.VMEM((bm * s_rows, LANES), BF16),
                        pltpu.SemaphoreType.DMA((2,)),
                        pltpu.SemaphoreType.DMA(()),
                        pltpu.SemaphoreType.DMA(())])
    return pl.pallas_call(
        functools.partial(_dispatch_kernel, n_steps=n_steps, n_exp=n_exp, n_blocks=n_slots // bm,
                          s_rows=s_rows, bm=bm),
        grid_spec=grid_spec,
        out_shape=jax.ShapeDtypeStruct((n_slots * s_rows, LANES), BF16),
        compiler_params=pltpu.CompilerParams(vmem_limit_bytes=VMEM_LIMIT, disable_bounds_checks=True),
    )(pad_end, padded, n_used, slots, hp)


def _moe_kernel(be_ref, nu_ref, x_ref, wg_ref, wu_ref, wd_ref, y_ref, wgb, wub, wdb, scr_ref, *, s_rows):
    i = pl.program_id(0)
    nu = nu_ref[0]
    bm = x_ref.shape[0] // s_rows

    @pl.when(i < nu)
    def _():
        @pl.when((i == 0) | (be_ref[i] != be_ref[jnp.maximum(i - 1, 0)]))
        def _():
            wgb[...] = wg_ref[0, 0].astype(BF16)
            wub[...] = wu_ref[0, 0].astype(BF16)
            wdb[...] = wd_ref[0, 0].astype(BF16)

        scr_ref[...] = x_ref[...].astype(F32)
        gate = up = None
        for c in range(s_rows // 2):
            xc = jnp.concatenate([_from_token_rows(scr_ref, s_rows, bm, 2 * c),
                                  _from_token_rows(scr_ref, s_rows, bm, 2 * c + 1)], axis=1).astype(BF16)
            rows = slice(2 * c * LANES, (2 * c + 2) * LANES)
            gc = jnp.dot(xc, wgb[rows, :], preferred_element_type=F32)
            uc = jnp.dot(xc, wub[rows, :], preferred_element_type=F32)
            gate = gc if gate is None else gate + gc
            up = uc if up is None else up + uc
        a = (_silu(gate) * up).astype(BF16)
        for c in range(s_rows // 2):
            yc = jnp.dot(a, wdb[:, 2 * c * LANES:(2 * c + 2) * LANES], preferred_element_type=F32)
            scr_ref[pl.ds(2 * c, bm, stride=s_rows), :] = yc[:, :LANES]
            scr_ref[pl.ds(2 * c + 1, bm, stride=s_rows), :] = yc[:, LANES:]
        y_ref[...] = scr_ref[...].astype(BF16)

    @pl.when(i >= nu)
    def _():
        y_ref[...] = jnp.zeros_like(y_ref)


def _moe_experts(xs, blk_expert, n_used, wg, wu, wd, layer, bm):
    _, n_exp, d, de = wg.shape
    s_rows = d // LANES
    nb = xs.shape[0] // (bm * s_rows)
    blk = pl.BlockSpec((bm * s_rows, LANES), lambda i, be, nu: (jnp.minimum(i, nu[0] - 1), 0))
    grid_spec = pltpu.PrefetchScalarGridSpec(
        num_scalar_prefetch=2,
        grid=(nb,),
        in_specs=[blk,
                  pl.BlockSpec((1, 1, d, de), lambda i, be, nu: (layer, be[i], 0, 0)),
                  pl.BlockSpec((1, 1, d, de), lambda i, be, nu: (layer, be[i], 0, 0)),
                  pl.BlockSpec((1, 1, de, d), lambda i, be, nu: (layer, be[i], 0, 0))],
        out_specs=pl.BlockSpec((bm * s_rows, LANES), lambda i, be, nu: (i, 0)),
        scratch_shapes=[pltpu.VMEM((d, de), BF16), pltpu.VMEM((d, de), BF16), pltpu.VMEM((de, d), BF16),
                        pltpu.VMEM((bm * s_rows, LANES), F32)])
    return pl.pallas_call(
        functools.partial(_moe_kernel, s_rows=s_rows),
        grid_spec=grid_spec,
        out_shape=jax.ShapeDtypeStruct(xs.shape, BF16),
        compiler_params=pltpu.CompilerParams(vmem_limit_bytes=VMEM_LIMIT),
    )(blk_expert, n_used, xs, wg, wu, wd)


def _combine_kernel(slot_hbm, y_hbm, gate_ref, h2_ref, x1_ref, sg_ref, su_ref, sd_ref, g2_ref, lng_ref, lnb_ref,
                    out_ref, idx_ref, ybuf, f_ref, frow_ref, isem, rsem, *, order, n_steps, s_rows):
    i = pl.program_id(0)
    tt, d = h2_ref.shape
    n_rows = TOP_K * tt

    def idx_copy(tile, buf):
        return pltpu.make_async_copy(slot_hbm.at[tile], idx_ref.at[pl.ds(buf * n_rows, n_rows)], isem.at[buf])

    group = 4
    assert tt % group == 0

    def issue_token(buf, r0, u):
        base = pl.multiple_of(r0 * s_rows, group * s_rows)
        ibase = buf * n_rows + r0
        for k in range(TOP_K):
            t = idx_ref[ibase + (k * tt + u)]
            pltpu.make_async_copy(y_hbm.at[pl.ds(pl.multiple_of(t * s_rows, s_rows), s_rows), :],
                                  ybuf.at[buf, pl.ds(base + (k * tt + u) * s_rows, s_rows), :],
                                  rsem.at[buf]).start()

    @pl.when(i == 0)
    def _():
        idx_copy(0, 0).start()
        idx_copy(0, 0).wait()

        def first(gi, carry):
            for u in range(group):
                issue_token(0, gi * group, u)
            return carry
        lax.fori_loop(0, tt // group, first, 0)
        if n_steps > 1:
            idx_copy(1, 1).start()

    buf = i % 2
    nxt = 1 - buf
    has_next = i + 1 < n_steps

    @pl.when(has_next)
    def _():
        idx_copy(i + 1, nxt).wait()

    hb = h2_ref[...].astype(BF16)
    a = (_silu(jnp.dot(hb, sg_ref[...], preferred_element_type=F32))
         * jnp.dot(hb, su_ref[...], preferred_element_type=F32))
    f_ref[...] = jnp.dot(a.astype(BF16), sd_ref[...], preferred_element_type=F32)
    pltpu.make_async_copy(y_hbm.at[pl.ds(0, n_rows * s_rows), :], ybuf.at[buf], rsem.at[buf]).wait()

    def token_loop(prefetch):
        def body(gi, carry):
            r0 = gi * group
            base = pl.multiple_of(r0 * s_rows, group * s_rows)
            gbase = pl.multiple_of(r0 * TOP_K, group * TOP_K)
            for u in range(group):
                if prefetch:
                    issue_token(nxt, r0, u)
                gates = gate_ref[pl.ds(gbase + u * TOP_K, TOP_K), :]
                acc = jnp.zeros((s_rows, LANES), F32)
                for k in range(TOP_K):
                    rows = pl.ds(base + (k * tt + u) * s_rows, s_rows)
                    acc = acc + jnp.broadcast_to(gates[k:k + 1, :], (s_rows, LANES)) * ybuf[buf, rows, :].astype(F32)
                frow_ref[pl.ds(base + u * s_rows, s_rows), :] = acc
            return carry
        lax.fori_loop(0, tt // group, body, 0)

    @pl.when(has_next)
    def _():
        token_loop(True)

    @pl.when(jnp.logical_not(has_next))
    def _():
        token_loop(False)

    @pl.when(i + 2 < n_steps)
    def _():
        idx_copy(i + 2, buf).start()

    for s in range(s_rows):
        sl = slice(s * LANES, (s + 1) * LANES)
        f_ref[:, sl] = f_ref[:, sl] + _from_token_rows(frow_ref, s_rows, tt, s)
    g = order.group(i)
    r = DEEPNORM_ALPHA * x1_ref[...] + g2_ref[0, pl.ds(g, 1), :] * f_ref[...]
    out_ref[...] = _layer_norm(r, lng_ref[...], lnb_ref[...])


def _combine(slots, y_sorted, gate, h2, x1, sg, su, sd, mod, ln_g, ln_b, layer, order):
    n_tok, d = h2.shape
    tt = order.tile
    n_steps = order.n_tiles
    s_rows = d // LANES
    ds_ = sg.shape[1]
    row = pl.BlockSpec((1, d), lambda i: (0, 0))
    tok = pl.BlockSpec((tt, d), lambda i: (i, 0))
    return pl.pallas_call(
        functools.partial(_combine_kernel, order=order, n_steps=n_steps, s_rows=s_rows),
        grid=(n_steps,),
        in_specs=[pl.BlockSpec(memory_space=pl.ANY),
                  pl.BlockSpec(memory_space=pl.ANY),
                  pl.BlockSpec((tt * TOP_K, LANES), lambda i: (i, 0)),
                  tok, tok,
                  pl.BlockSpec((d, ds_), lambda i: (0, 0)),
                  pl.BlockSpec((d, ds_), lambda i: (0, 0)),
                  pl.BlockSpec((ds_, d), lambda i: (0, 0)),
                  _mod_spec(layer, MOD_G2, d, 1), row, row],
        out_specs=tok,
        out_shape=jax.ShapeDtypeStruct((n_tok, d), F32),
        scratch_shapes=[pltpu.SMEM((2 * TOP_K * tt,), jnp.int32),
                        pltpu.VMEM((2, TOP_K * tt * s_rows, LANES), BF16),
                        pltpu.VMEM((tt, d), F32),
                        pltpu.VMEM((tt * s_rows, LANES), F32),
                        pltpu.SemaphoreType.DMA((2,)),
                        pltpu.SemaphoreType.DMA((2,))],
        compiler_params=pltpu.CompilerParams(vmem_limit_bytes=VMEM_LIMIT, disable_bounds_checks=True),
    )(slots, y_sorted, gate, h2, x1, sg, su, sd, mod, ln_g.reshape(1, d), ln_b.reshape(1, d))


def _moe_layer(x1, h2, hp, layer, mod, order, router_w, router_b, wg, wu, wd, sg, su, sd, ln_g, ln_b):
    n_tok, d = h2.shape
    n_exp = router_w.shape[1]
    bm = MOE_BLOCK
    tt = order.tile
    eidx_t, gate_t, rank_t, cnt = _router(h2, router_w, router_b)

    counts = cnt[:, 0].astype(jnp.int32)
    padded = (counts + bm - 1) // bm * bm
    pad_end = jnp.cumsum(padded).astype(jnp.int32)
    pad_start = pad_end - padded
    experts = jnp.arange(n_exp, dtype=jnp.int32)
    slot_t = rank_t + jnp.sum(jnp.where(eidx_t[:, :, None] == experts, pad_start, 0), axis=-1)
    n_slots = n_tok * TOP_K + n_exp * bm
    nb = n_slots // bm
    n_used = pad_end[-1:] // bm
    blk_expert = jnp.minimum(jnp.sum(pad_end[None, :] <= (jnp.arange(nb, dtype=jnp.int32) * bm)[:, None], axis=1),
                             n_exp - 1).astype(jnp.int32)
    n_tiles = n_tok // tt
    slots_tok = slot_t.T.reshape(n_tiles, tt * TOP_K)
    slots_k = slot_t.reshape(TOP_K, n_tiles, tt).transpose(1, 0, 2).reshape(n_tiles, TOP_K * tt)

    xs = _dispatch(hp, slots_tok, pad_end, padded, n_used, n_slots, bm)
    ys = _moe_experts(xs, blk_expert, n_used, wg, wu, wd, layer, bm)
    gates = jnp.broadcast_to(gate_t.T.reshape(n_tok * TOP_K, 1), (n_tok * TOP_K, LANES))
    return _combine(slots_k, ys, gates, h2, x1, sg[layer].astype(BF16), su[layer].astype(BF16),
                    sd[layer].astype(BF16), mod, ln_g[layer], ln_b[layer], layer, order)


def kernel(x, c, ctx, c_ctx, ada_w, ada_b, ln_mix_g, ln_mix_b, ln_ffn_g, ln_ffn_b, ret_w_in, ret_w_out,
           ret_decay_fwd, ret_decay_bwd, pool_w, pool_scale, router_w, router_b, exp_w_gate, exp_w_up,
           exp_w_down, sh_w_gate, sh_w_up, sh_w_down):
    batch, seq, d = x.shape
    ctx_len = ctx.shape[1]
    heads = RET_HEADS
    qk_dim = d
    head_dim = qk_dim // heads
    v_dim = ret_w_out.shape[1]
    assert ada_w.shape[0] == DEPTH == 2 and batch + 1 <= MOD_ROWS

    cvec = jnp.zeros((MOD_ROWS, d), F32).at[:batch].set(c.astype(F32)).at[batch].set(c_ctx.astype(F32))
    mod = _modulation(cvec, ada_w.astype(F32), ada_b.astype(F32))

    x2d = x.reshape(batch * seq, d)
    ctx2d = ctx.reshape(batch * ctx_len, d)
    order0 = _Order(batch, seq, ctx_len, TOKEN_TILE)

    w_in = ret_w_in[0].astype(F32)
    tabs = _rope_tables(seq, head_dim, TOKEN_TILE)
    qk = _inproj(x2d, ctx2d, mod, w_in, 0, 2 * qk_dim, order0, BF16, rope_tabs=tabs, k_col0=qk_dim,
                 head_dim=head_dim)
    v = _inproj(x2d, ctx2d, mod, w_in, 2 * qk_dim, v_dim, order0, BF16)
    g = _inproj(x2d, ctx2d, mod, w_in, 2 * qk_dim + v_dim, v_dim, order0, F32)
    og = _retention(qk, v, g, ret_decay_fwd[0], ret_decay_bwd[0], batch, ctx_len + seq, ctx_len)
    x1, h2, hp = _outproj(og, x2d, ctx2d, ret_w_out[0].astype(BF16), mod, ln_mix_g[0], ln_mix_b[0], order0)
    xa = _moe_layer(x1, h2, hp, 0, mod, order0, router_w[0], router_b[0], exp_w_gate, exp_w_up,
                    exp_w_down, sh_w_gate, sh_w_up, sh_w_down, ln_ffn_g, ln_ffn_b)

    order1 = _Order(batch, seq, 0, TOKEN_TILE)
    x1, h2, hp = _pool_layer(xa, order0, pool_w[0], pool_scale[0], mod, ln_mix_g[1], ln_mix_b[1], 1, batch, seq)
    xb = _moe_layer(x1, h2, hp, 1, mod, order1, router_w[1], router_b[1], exp_w_gate, exp_w_up,
                    exp_w_down, sh_w_gate, sh_w_up, sh_w_down, ln_ffn_g, ln_ffn_b)
    return xb.reshape(batch, seq, d)
```

```python
import functools

import jax
import jax.numpy as jnp
from jax import lax
from jax.experimental import pallas as pl
from jax.experimental.pallas import tpu as pltpu

F32 = jnp.float32
BF16 = jnp.bfloat16
U32 = jnp.uint32
HIGHEST = lax.Precision.HIGHEST

DEPTH = 2
GRID_W = 64
RET_HEADS = 8
ROPE_BASE = 10000.0
POOL_WINDOWS = (2, 4, 8, 16)
N_EXPERT_GROUPS = 8
TOPK_GROUPS = 4
TOP_K = 8
ROUTED_SCALE = 2.5
LN_EPS = 1e-5
N_MOD = 6
DEEPNORM_ALPHA = (2.0 * DEPTH) ** 0.25
MOD_SH1, MOD_SC1, MOD_G1, MOD_SH2, MOD_SC2, MOD_G2 = range(N_MOD)

LANES = 128
TOKEN_TILE = 256
RET_BLOCK = 2816
RET_CHUNK = 256
ROUTER_TILE = 512
MOE_BLOCK = 512
MOE_PARTS = 2
MOD_ROWS = 8
VMEM_LIMIT = 56 * 1024 * 1024


def _silu(x):
    return x * jax.nn.sigmoid(x)


def _layer_norm(r, g, b):
    mu = jnp.mean(r, axis=-1, keepdims=True)
    var = jnp.mean(jnp.square(r - mu), axis=-1, keepdims=True)
    return (r - mu) * lax.rsqrt(var + LN_EPS) * g + b


def _to_token_rows(scr_ref, h):
    tm, d = h.shape
    s_rows = d // LANES
    for s in range(s_rows):
        scr_ref[pl.ds(s, tm, stride=s_rows), :] = h[:, s * LANES:(s + 1) * LANES]
    return scr_ref[...].astype(BF16)


def _from_token_rows(scr_ref, rows, n, s):
    return scr_ref[pl.ds(s, n, stride=rows), :]


class _Order:
    def __init__(self, batch, seq, ctx_len, tile):
        assert seq % tile == 0 and ctx_len % tile == 0
        self.batch, self.seq, self.ctx_len, self.tile = batch, seq, ctx_len, tile
        self.ctx_tiles = ctx_len // tile
        self.lat_tiles = seq // tile
        self.per_batch = self.ctx_tiles + self.lat_tiles
        self.n_tiles = batch * self.per_batch
        self.n_tok = self.n_tiles * tile

    def split(self, i):
        b = i // self.per_batch
        r = i % self.per_batch
        return b, r, r < self.ctx_tiles

    def group(self, i):
        b, _, is_ctx = self.split(i)
        return jnp.where(is_ctx, self.batch, b)

    def lat_block(self, i):
        b, r, _ = self.split(i)
        return b * self.lat_tiles + jnp.maximum(r - self.ctx_tiles, 0)

    def ctx_block(self, i):
        b, r, _ = self.split(i)
        return b * self.ctx_tiles + jnp.minimum(r, max(self.ctx_tiles - 1, 0))


def _mod_kernel(c_ref, w_ref, b_ref, o_ref):
    s = _silu(c_ref[...])
    o_ref[0] = jnp.dot(s, w_ref[0], precision=HIGHEST, preferred_element_type=F32) + b_ref[0]


def _modulation(cvec, ada_w, ada_b):
    depth, d, n = ada_w.shape
    tn = 1024
    return pl.pallas_call(
        _mod_kernel,
        grid=(depth, n // tn),
        in_specs=[pl.BlockSpec((MOD_ROWS, d), lambda l, j: (0, 0)),
                  pl.BlockSpec((1, d, tn), lambda l, j: (l, 0, j)),
                  pl.BlockSpec((1, 1, tn), lambda l, j: (l, 0, j))],
        out_specs=pl.BlockSpec((1, MOD_ROWS, tn), lambda l, j: (l, 0, j)),
        out_shape=jax.ShapeDtypeStruct((depth, MOD_ROWS, n), F32),
        compiler_params=pltpu.CompilerParams(vmem_limit_bytes=VMEM_LIMIT),
    )(cvec, ada_w, ada_b.reshape(depth, 1, n))


def _mod_spec(layer, chunk, d, n_grid_args):
    if n_grid_args == 1:
        return pl.BlockSpec((1, MOD_ROWS, d), lambda i: (layer, 0, chunk))
    return pl.BlockSpec((1, MOD_ROWS, d), lambda j, i: (layer, 0, chunk))


def _inproj_kernel(x_ref, c_ref, sc_ref, sh_ref, w_ref, *rest, order, rope, k_col0, head_dim):
    i = pl.program_id(1)
    j = pl.program_id(0)
    if rope:
        cos_ref, sin_ref, o_ref, wb_ref = rest
    else:
        o_ref, wb_ref = rest

    @pl.when(i == 0)
    def _():
        wb_ref[...] = w_ref[...].astype(BF16)

    g = order.group(i)
    _, _, is_ctx = order.split(i)
    sc = sc_ref[0, pl.ds(g, 1), :]
    sh = sh_ref[0, pl.ds(g, 1), :]
    xin = jnp.where(is_ctx, c_ref[...], x_ref[...])
    a = (xin * (1.0 + sc) + sh).astype(BF16)
    acc = jnp.dot(a, wb_ref[...], preferred_element_type=F32)
    if not rope:
        o_ref[...] = acc.astype(o_ref.dtype)
        return
    tn = acc.shape[1]
    lane = lax.broadcasted_iota(jnp.int32, acc.shape, 1)
    partner = jnp.where((lane & 1) == 0, pltpu.roll(acc, tn - 1, 1), pltpu.roll(acc, 1, 1))
    cos = cos_ref[...]
    sin = sin_ref[...]
    scale = jnp.where(j * tn >= k_col0, head_dim ** -0.5, 1.0).astype(F32)
    for h in range(tn // head_dim):
        sl = slice(h * head_dim, (h + 1) * head_dim)
        o_ref[:, sl] = ((acc[:, sl] * cos + partner[:, sl] * sin) * scale).astype(o_ref.dtype)


def _inproj(x2d, ctx2d, mod, w_in, col0, n, order, out_dtype, rope_tabs=None, k_col0=0, head_dim=0):
    d = w_in.shape[0]
    tm = order.tile
    tn = min(n, 2048)
    assert col0 % tn == 0 and n % tn == 0
    rope = rope_tabs is not None
    in_specs = [pl.BlockSpec((tm, d), lambda j, i: (order.lat_block(i), 0)),
                pl.BlockSpec((tm, d), lambda j, i: (order.ctx_block(i), 0)),
                _mod_spec(0, MOD_SC1, d, 2), _mod_spec(0, MOD_SH1, d, 2),
                pl.BlockSpec((d, tn), lambda j, i: (0, col0 // tn + j), pipeline_mode=pl.Buffered(1))]
    args = [x2d, ctx2d, mod, mod, w_in]
    if rope:
        def tab_map(j, i):
            _, r, is_ctx = order.split(i)
            return (jnp.where(is_ctx, order.lat_tiles, r - order.ctx_tiles), 0)
        in_specs += [pl.BlockSpec((tm, head_dim), tab_map), pl.BlockSpec((tm, head_dim), tab_map)]
        args += list(rope_tabs)
    return pl.pallas_call(
        functools.partial(_inproj_kernel, order=order, rope=rope, k_col0=k_col0, head_dim=head_dim),
        grid=(n // tn, order.n_tiles),
        in_specs=in_specs,
        out_specs=pl.BlockSpec((tm, tn), lambda j, i: (i, j)),
        out_shape=jax.ShapeDtypeStruct((order.n_tok, n), out_dtype),
        scratch_shapes=[pltpu.VMEM((d, tn), BF16)],
        compiler_params=pltpu.CompilerParams(vmem_limit_bytes=VMEM_LIMIT),
    )(*args)


def _rope_tables(seq, head_dim, tile):
    n_freq = head_dim // 4
    pos = jnp.arange(seq, dtype=jnp.int32)
    row = (pos // GRID_W).astype(F32)
    col = (pos % GRID_W).astype(F32)
    freqs = ROPE_BASE ** (-jnp.arange(n_freq, dtype=F32) / n_freq)
    ang = jnp.concatenate([row[:, None] * freqs, col[:, None] * freqs], axis=-1)
    cos = jnp.repeat(jnp.cos(ang), 2, axis=-1)
    sin = jnp.stack([-jnp.sin(ang), jnp.sin(ang)], axis=-1).reshape(seq, head_dim)
    ident = jnp.zeros((tile, head_dim), F32)
    return jnp.concatenate([cos, ident + 1.0], axis=0), jnp.concatenate([sin, ident], axis=0)


def _ret_kernel(q_ref, k_ref, v_ref, g_ref, df_ref, db_ref, o_ref, sf_ref, sb_ref, obuf_ref, *,
                n_blocks, chunks_per_block, ctx_chunks):
    C = RET_CHUNK
    NB, CPB, NCC = n_blocks, chunks_per_block, ctx_chunks
    st = pl.program_id(2)
    phase = st // (NB + 1)
    s = st % (NB + 1)
    ldf = jax.nn.log_sigmoid(df_ref[0])[:, :1]
    ldb = jax.nn.log_sigmoid(db_ref[0])[:, :1]
    pos = lax.broadcasted_iota(jnp.int32, (C, 1), 0).astype(F32)
    tdims = (((0,), (0,)), ((), ()))

    @pl.when(phase == 0)
    def _backward_states():
        @pl.when(s == 0)
        def _():
            sb_ref[...] = jnp.zeros_like(sb_ref)

        q_dec = jnp.exp(ldb * (C - pos))
        k_dec = jnp.exp(ldb * pos)
        chunk_dec = jnp.exp(ldb * C)

        def chunk(blk, c):
            rows = pl.ds(c * C, C)
            q = q_ref[rows, :].astype(F32)
            k = k_ref[rows, :].astype(F32)
            v = v_ref[rows, :]
            state = sb_ref[...]
            goff = pl.multiple_of((blk * CPB + c) * C, C)
            obuf_ref[pl.ds(goff, C), :] = jnp.dot((q * q_dec).astype(BF16), state.astype(BF16),
                                                  preferred_element_type=F32)
            sb_ref[...] = state * chunk_dec + lax.dot_general((k * k_dec).astype(BF16), v, tdims,
                                                              preferred_element_type=F32)

        @pl.when(s == 0)
        def _():
            for c in reversed(range(NCC)):
                chunk(0, c)

        @pl.when((s > 0) & (s < NB))
        def _():
            for c in reversed(range(CPB)):
                chunk(NB - s, c)

        @pl.when(s == NB)
        def _():
            for c in reversed(range(NCC, CPB)):
                chunk(0, c)

    @pl.when((phase == 1) & (s < NB))
    def _forward_and_output():
        @pl.when(s == 0)
        def _():
            sf_ref[...] = jnp.zeros_like(sf_ref)

        q_dec = jnp.exp(ldf * (pos + 1.0))
        k_dec = jnp.exp(ldf * (C - 1.0 - pos))
        chunk_dec = jnp.exp(ldf * C)
        ii = lax.broadcasted_iota(jnp.int32, (C, C), 0)
        jj = lax.broadcasted_iota(jnp.int32, (C, C), 1)
        rel = (ii - jj).astype(F32)
        decay = jnp.where(rel >= 0, jnp.exp(ldf * jnp.maximum(rel, 0.0)), jnp.exp(ldb * jnp.maximum(-rel, 0.0)))

        for c in range(CPB):
            rows = pl.ds(c * C, C)
            qb = q_ref[rows, :]
            kb = k_ref[rows, :]
            v = v_ref[rows, :]
            q = qb.astype(F32)
            k = kb.astype(F32)
            state = sf_ref[...]
            scores = lax.dot_general(qb, kb, (((1,), (1,)), ((), ())), preferred_element_type=F32) * decay
            goff = pl.multiple_of((s * CPB + c) * C, C)
            o = (jnp.dot(scores.astype(BF16), v, preferred_element_type=F32)
                 + jnp.dot((q * q_dec).astype(BF16), state.astype(BF16), preferred_element_type=F32)
                 + obuf_ref[pl.ds(goff, C), :])
            sf_ref[...] = state * chunk_dec + lax.dot_general((k * k_dec).astype(BF16), v, tdims,
                                                              preferred_element_type=F32)
            mu = jnp.mean(o, axis=-1, keepdims=True)
            var = jnp.mean(jnp.square(o - mu), axis=-1, keepdims=True)
            on = (o - mu) * lax.rsqrt(var + LN_EPS)
            o_ref[rows, :] = (_silu(g_ref[rows, :]) * on).astype(o_ref.dtype)


def _retention(qk, v, g, raw_fwd, raw_bwd, batch, tokens_per_batch, ctx_len):
    n_tok, qk2 = qk.shape
    v_dim = v.shape[1]
    heads = RET_HEADS
    dk = qk2 // 2 // heads
    dv = v_dim // heads
    tb = next(t for t in range(RET_BLOCK, 0, -RET_CHUNK) if tokens_per_batch % t == 0)
    assert tb % RET_CHUNK == 0 and ctx_len % RET_CHUNK == 0 and ctx_len <= tb
    nb = tokens_per_batch // tb
    steps = 2 * (nb + 1)

    def blk(st):
        phase = st // (nb + 1)
        s = st % (nb + 1)
        back = jnp.where(s == 0, 0, nb - s)
        return jnp.where(phase == 0, back, jnp.minimum(s, nb - 1))

    def fwd_blk(st):
        phase = st // (nb + 1)
        s = st % (nb + 1)
        return jnp.where(phase == 0, 0, jnp.minimum(s, nb - 1))

    dec = lambda raw: jnp.broadcast_to(raw.astype(F32)[:, None, None], (heads, 1, LANES))
    return pl.pallas_call(
        functools.partial(_ret_kernel, n_blocks=nb, chunks_per_block=tb // RET_CHUNK,
                          ctx_chunks=ctx_len // RET_CHUNK),
        grid=(batch, heads, steps),
        in_specs=[pl.BlockSpec((tb, dk), lambda b, h, st: (b * nb + blk(st), h)),
                  pl.BlockSpec((tb, dk), lambda b, h, st: (b * nb + blk(st), heads + h)),
                  pl.BlockSpec((tb, dv), lambda b, h, st: (b * nb + blk(st), h)),
                  pl.BlockSpec((tb, dv), lambda b, h, st: (b * nb + fwd_blk(st), h)),
                  pl.BlockSpec((1, 1, LANES), lambda b, h, st: (h, 0, 0)),
                  pl.BlockSpec((1, 1, LANES), lambda b, h, st: (h, 0, 0))],
        out_specs=pl.BlockSpec((tb, dv), lambda b, h, st: (b * nb + fwd_blk(st), h)),
        out_shape=jax.ShapeDtypeStruct((n_tok, v_dim), BF16),
        scratch_shapes=[pltpu.VMEM((dk, dv), F32), pltpu.VMEM((dk, dv), F32),
                        pltpu.VMEM((tokens_per_batch, dv), F32)],
        compiler_params=pltpu.CompilerParams(vmem_limit_bytes=VMEM_LIMIT),
    )(qk, qk, v, g, dec(raw_fwd), dec(raw_bwd))


def _outproj_kernel(o_ref, x_ref, c_ref, w_ref, g1_ref, sc2_ref, sh2_ref, lng_ref, lnb_ref,
                    x1_ref, h2_ref, hp_ref, scr_ref, *, order):
    i = pl.program_id(0)
    g = order.group(i)
    _, _, is_ctx = order.split(i)
    y = jnp.dot(o_ref[...], w_ref[...], preferred_element_type=F32)
    xin = jnp.where(is_ctx, c_ref[...], x_ref[...])
    r = DEEPNORM_ALPHA * xin + g1_ref[0, pl.ds(g, 1), :] * y
    x1 = _layer_norm(r, lng_ref[...], lnb_ref[...])
    x1_ref[...] = x1
    h2 = x1 * (1.0 + sc2_ref[0, pl.ds(g, 1), :]) + sh2_ref[0, pl.ds(g, 1), :]
    h2_ref[...] = h2
    hp_ref[...] = _to_token_rows(scr_ref, h2)


def _outproj(og, x2d, ctx2d, w_out, mod, ln_g, ln_b, order):
    v_dim, d = w_out.shape
    tm = order.tile
    s_rows = d // LANES
    row = pl.BlockSpec((1, d), lambda i: (0, 0))
    tok = pl.BlockSpec((tm, d), lambda i: (i, 0))
    return pl.pallas_call(
        functools.partial(_outproj_kernel, order=order),
        grid=(order.n_tiles,),
        in_specs=[pl.BlockSpec((tm, v_dim), lambda i: (i, 0)),
                  pl.BlockSpec((tm, d), lambda i: (order.lat_block(i), 0)),
                  pl.BlockSpec((tm, d), lambda i: (order.ctx_block(i), 0)),
                  pl.BlockSpec((v_dim, d), lambda i: (0, 0), pipeline_mode=pl.Buffered(1)),
                  _mod_spec(0, MOD_G1, d, 1), _mod_spec(0, MOD_SC2, d, 1), _mod_spec(0, MOD_SH2, d, 1),
                  row, row],
        out_specs=[tok, tok, pl.BlockSpec((tm * s_rows, LANES), lambda i: (i, 0))],
        out_shape=[jax.ShapeDtypeStruct((order.n_tok, d), F32)] * 2
                  + [jax.ShapeDtypeStruct((order.n_tok * s_rows, LANES), BF16)],
        scratch_shapes=[pltpu.VMEM((tm * s_rows, LANES), F32)],
        compiler_params=pltpu.CompilerParams(vmem_limit_bytes=VMEM_LIMIT),
    )(og, x2d, ctx2d, w_out, mod, mod, mod, ln_g.reshape(1, d), ln_b.reshape(1, d))


def _pool_kernel(x_ref, pw_ref, ps_ref, sc1_ref, sh1_ref, g1_ref, sc2_ref, sh2_ref, lng_ref, lnb_ref,
                 x1_ref, h2_ref, hp_ref, y_ref, scr_ref, *, lat_tiles):
    i = pl.program_id(0)
    g = i // lat_tiles
    x = x_ref[...]
    tm, d = x.shape
    gd = d // len(POOL_WINDOWS)
    h = x * (1.0 + sc1_ref[0, pl.ds(g, 1), :]) + sh1_ref[0, pl.ds(g, 1), :]
    ni = lax.broadcasted_iota(jnp.int32, (tm, tm), 0)
    mi = lax.broadcasted_iota(jnp.int32, (tm, tm), 1)
    same_row = (ni // GRID_W) == (mi // GRID_W)
    delta = mi - ni
    col = lax.broadcasted_iota(jnp.int32, (tm, 1), 0) % GRID_W
    for gi, w in enumerate(POOL_WINDOWS):
        window = jnp.where(same_row & (delta >= -(w // 2)) & (delta < w // 2), 1.0, 0.0).astype(BF16)
        count = (jnp.minimum(col + w // 2, GRID_W) - jnp.maximum(col - w // 2, 0)).astype(F32)
        hg = h[:, gi * gd:(gi + 1) * gd]
        h_hi = hg.astype(BF16)
        r1 = hg - h_hi.astype(F32)
        h_mid = r1.astype(BF16)
        h_lo = (r1 - h_mid.astype(F32)).astype(BF16)
        total = (jnp.dot(window, h_hi, preferred_element_type=F32)
                 + jnp.dot(window, h_mid, preferred_element_type=F32)
                 + jnp.dot(window, h_lo, preferred_element_type=F32))
        p = total / count - hg
        y_ref[:, gi * gd:(gi + 1) * gd] = jnp.dot(p.astype(BF16), pw_ref[gi], preferred_element_type=F32)
    y = y_ref[...] * ps_ref[...]
    r = DEEPNORM_ALPHA * x + g1_ref[0, pl.ds(g, 1), :] * y
    x1 = _layer_norm(r, lng_ref[...], lnb_ref[...])
    x1_ref[...] = x1
    h2 = x1 * (1.0 + sc2_ref[0, pl.ds(g, 1), :]) + sh2_ref[0, pl.ds(g, 1), :]
    h2_ref[...] = h2
    hp_ref[...] = _to_token_rows(scr_ref, h2)


def _pool_layer(x_prev, order0, pool_w, pool_scale, mod, ln_g, ln_b, layer, batch, seq):
    d = x_prev.shape[1]
    tm = order0.tile
    s_rows = d // LANES
    lat_tiles = seq // tm
    n_tiles = batch * lat_tiles
    ng, gd, _ = pool_w.shape

    def src(i):
        b = i // lat_tiles
        return (b * order0.per_batch + order0.ctx_tiles + i % lat_tiles, 0)

    row = pl.BlockSpec((1, d), lambda i: (0, 0))
    tok = pl.BlockSpec((tm, d), lambda i: (i, 0))
    ms = lambda chunk: _mod_spec(layer, chunk, d, 1)
    return pl.pallas_call(
        functools.partial(_pool_kernel, lat_tiles=lat_tiles),
        grid=(n_tiles,),
        in_specs=[pl.BlockSpec((tm, d), src),
                  pl.BlockSpec((ng, gd, gd), lambda i: (0, 0, 0)),
                  row, ms(MOD_SC1), ms(MOD_SH1), ms(MOD_G1), ms(MOD_SC2), ms(MOD_SH2), row, row],
        out_specs=[tok, tok, pl.BlockSpec((tm * s_rows, LANES), lambda i: (i, 0))],
        out_shape=[jax.ShapeDtypeStruct((n_tiles * tm, d), F32)] * 2
                  + [jax.ShapeDtypeStruct((n_tiles * tm * s_rows, LANES), BF16)],
        scratch_shapes=[pltpu.VMEM((tm, d), F32), pltpu.VMEM((tm * s_rows, LANES), F32)],
        compiler_params=pltpu.CompilerParams(vmem_limit_bytes=VMEM_LIMIT),
    )(x_prev, pool_w.astype(BF16), pool_scale.reshape(1, d), mod, mod, mod, mod, mod,
      ln_g.reshape(1, d), ln_b.reshape(1, d))


def _route_tile(h, w_ref, b_ref, eidx_ref, gate_ref, rank_ref, cnt_ref, carry_ref):
    i = pl.program_id(0)
    n_exp = w_ref.shape[0]
    tr = h.shape[0]
    per = n_exp // N_EXPERT_GROUPS
    neg = -jnp.inf

    @pl.when(i == 0)
    def _():
        carry_ref[...] = jnp.zeros_like(carry_ref)

    logits = lax.dot_general(w_ref[...], h, (((1,), (1,)), ((), ())),
                             precision=HIGHEST, preferred_element_type=F32)
    scores = jax.nn.sigmoid(logits)
    biased = scores + b_ref[...]

    sub_idx = lax.broadcasted_iota(jnp.int32, (per, tr), 0)
    group_score = []
    for gi in range(N_EXPERT_GROUPS):
        sub = biased[gi * per:(gi + 1) * per, :]
        m1 = jnp.max(sub, axis=0, keepdims=True)
        first = jnp.min(jnp.where(sub == m1, sub_idx, per), axis=0, keepdims=True)
        m2 = jnp.max(jnp.where(sub_idx == first, neg, sub), axis=0, keepdims=True)
        group_score.append(m1 + m2)
    masked = []
    for gi in range(N_EXPERT_GROUPS):
        ahead = jnp.zeros((1, tr), jnp.int32)
        for gj in range(N_EXPERT_GROUPS):
            if gj == gi:
                continue
            wins = (group_score[gj] >= group_score[gi]) if gj < gi else (group_score[gj] > group_score[gi])
            ahead = ahead + wins.astype(jnp.int32)
        keep = ahead < TOPK_GROUPS
        masked.append(jnp.where(keep, biased[gi * per:(gi + 1) * per, :], neg))
    masked = jnp.concatenate(masked, axis=0)

    eid = lax.broadcasted_iota(jnp.int32, (n_exp, tr), 0)
    picks, gates = [], []
    onehot = jnp.zeros((n_exp, tr), F32)
    for _ in range(TOP_K):
        m = jnp.max(masked, axis=0, keepdims=True)
        first = jnp.min(jnp.where(masked == m, eid, n_exp), axis=0, keepdims=True)
        sel = eid == first
        gates.append(jnp.sum(jnp.where(sel, scores, 0.0), axis=0, keepdims=True))
        picks.append(first)
        masked = jnp.where(sel, neg, masked)
        onehot = onehot + sel.astype(F32)
    gate_sum = gates[0]
    for gk in gates[1:]:
        gate_sum = gate_sum + gk

    si = lax.broadcasted_iota(jnp.int32, (tr, tr), 0)
    ti = lax.broadcasted_iota(jnp.int32, (tr, tr), 1)
    before = jnp.where(si < ti, 1.0, 0.0).astype(BF16)
    prefix = jnp.dot(onehot.astype(BF16), before, preferred_element_type=F32) + carry_ref[:, :1]
    for k in range(TOP_K):
        eidx_ref[k:k + 1, :] = picks[k]
        gate_ref[k:k + 1, :] = gates[k] / gate_sum * ROUTED_SCALE
        rank_ref[k:k + 1, :] = jnp.sum(jnp.where(eid == picks[k], prefix, 0.0), axis=0,
                                       keepdims=True).astype(jnp.int32)
    carry_ref[...] = carry_ref[...] + jnp.sum(onehot, axis=1, keepdims=True)
    cnt_ref[...] = carry_ref[...]


def _router_kernel(h_ref, w_ref, b_ref, eidx_ref, gate_ref, rank_ref, cnt_ref, carry_ref):
    _route_tile(h_ref[...], w_ref, b_ref, eidx_ref, gate_ref, rank_ref, cnt_ref, carry_ref)


def _router(h2, router_w, router_b):
    n_tok, d = h2.shape
    n_exp = router_w.shape[1]
    tr = ROUTER_TILE
    assert n_tok % tr == 0
    sel = pl.BlockSpec((TOP_K, tr), lambda i: (0, i))
    return pl.pallas_call(
        _router_kernel,
        grid=(n_tok // tr,),
        in_specs=[pl.BlockSpec((tr, d), lambda i: (i, 0)),
                  pl.BlockSpec((n_exp, d), lambda i: (0, 0)),
                  pl.BlockSpec((n_exp, 1), lambda i: (0, 0))],
        out_specs=[sel, sel, sel, pl.BlockSpec((n_exp, LANES), lambda i: (0, 0))],
        out_shape=[jax.ShapeDtypeStruct((TOP_K, n_tok), jnp.int32),
                   jax.ShapeDtypeStruct((TOP_K, n_tok), F32),
                   jax.ShapeDtypeStruct((TOP_K, n_tok), jnp.int32),
                   jax.ShapeDtypeStruct((n_exp, LANES), F32)],
        scratch_shapes=[pltpu.VMEM((n_exp, LANES), F32)],
        compiler_params=pltpu.CompilerParams(vmem_limit_bytes=VMEM_LIMIT),
    )(h2, router_w.astype(F32).T, router_b.astype(F32).reshape(n_exp, 1))


def _dispatch_kernel(pe_ref, pd_ref, nu_ref, slot_hbm, hp_ref, xs_hbm, idx_ref, zero_ref, isem, zsem, rsem, *,
                     n_steps, n_exp, n_blocks, s_rows, bm):
    i = pl.program_id(0)
    tt = hp_ref.shape[0] // s_rows
    zrows = bm * s_rows
    per_tile = tt * TOP_K

    def idx_copy(tile, buf):
        return pltpu.make_async_copy(slot_hbm.at[tile], idx_ref.at[pl.ds(buf * per_tile, per_tile)], isem.at[buf])

    def zero_copy(row0):
        return pltpu.make_async_copy(zero_ref, xs_hbm.at[pl.ds(pl.multiple_of(row0, s_rows), zrows), :], zsem)

    @pl.when(i == 0)
    def _():
        zero_ref[...] = jnp.zeros_like(zero_ref)
        nu = nu_ref[0]

        def fill(fn):
            def per_expert(e, carry):
                @pl.when(pd_ref[e] > 0)
                def _():
                    fn(zero_copy((pe_ref[e] - bm) * s_rows))
                return carry
            lax.fori_loop(0, n_exp, per_expert, 0)

            def per_tail(b, carry):
                fn(zero_copy(b * zrows))
                return carry
            lax.fori_loop(nu, n_blocks, per_tail, 0)

        fill(lambda cp: cp.start())
        fill(lambda cp: cp.wait())
        idx_copy(0, 0).start()
        idx_copy(0, 0).wait()
        if n_steps > 1:
            idx_copy(1, 1).start()

    buf = i % 2

    @pl.when(i > 0)
    def _():
        idx_copy(i, buf).wait()

    group = 8
    assert tt % group == 0

    def body(gi, carry):
        base = pl.multiple_of(gi * (group * s_rows), group * s_rows)
        ibase = buf * per_tile + gi * (group * TOP_K)
        for u in range(group):
            src = hp_ref.at[pl.ds(base + u * s_rows, s_rows), :]
            for k in range(TOP_K):
                t = idx_ref[ibase + (u * TOP_K + k)]
                pltpu.make_async_copy(src, xs_hbm.at[pl.ds(pl.multiple_of(t * s_rows, s_rows), s_rows), :],
                                      rsem).start()
        return carry

    lax.fori_loop(0, tt // group, body, 0)

    @pl.when((i > 0) & (i + 1 < n_steps))
    def _():
        idx_copy(i + 1, 1 - buf).start()

    for _ in range(TOP_K):
        pltpu.make_async_copy(hp_ref, xs_hbm.at[pl.ds(0, tt * s_rows), :], rsem).wait()


def _dispatch(hp, slots, pad_end, padded, n_used, n_slots, bm):
    rows, _ = hp.shape
    n_steps, per_tile = slots.shape
    tt = per_tile // TOP_K
    s_rows = rows // (n_steps * tt)
    n_exp = pad_end.shape[0]
    grid_spec = pltpu.PrefetchScalarGridSpec(
        num_scalar_prefetch=3,
        grid=(n_steps,),
        in_specs=[pl.BlockSpec(memory_space=pl.ANY),
                  pl.BlockSpec((tt * s_rows, LANES), lambda i, pe, pd, nu: (i, 0))],
        out_specs=pl.BlockSpec(memory_space=pl.ANY),
        scratch_shapes=[pltpu.SMEM((2 * per_tile,), jnp.int32),
                        pltpu.VMEM((bm * s_rows, LANES), BF16),
                        pltpu.SemaphoreType.DMA((2,)),
                        pltpu.SemaphoreType.DMA(()),
                        pltpu.SemaphoreType.DMA(())])
    return pl.pallas_call(
        functools.partial(_dispatch_kernel, n_steps=n_steps, n_exp=n_exp, n_blocks=n_slots // bm,
                          s_rows=s_rows, bm=bm),
        grid_spec=grid_spec,
        out_shape=jax.ShapeDtypeStruct((n_slots * s_rows, LANES), BF16),
        compiler_params=pltpu.CompilerParams(vmem_limit_bytes=VMEM_LIMIT, disable_bounds_checks=True),
    )(pad_end, padded, n_used, slots, hp)


def _moe_kernel(be_ref, nu_ref, x_ref, wg_ref, wu_ref, wd_ref, y_ref, wgb, wub, wdb, *scr_refs, s_rows):
    i = pl.program_id(0)
    nu = nu_ref[0]
    parts = len(scr_refs)
    pm = x_ref.shape[0] // s_rows // parts

    @pl.when(i < nu)
    def _():
        @pl.when((i == 0) | (be_ref[i] != be_ref[jnp.maximum(i - 1, 0)]))
        def _():
            wgb[...] = wg_ref[0, 0].astype(BF16)
            wub[...] = wu_ref[0, 0].astype(BF16)
            wdb[...] = wd_ref[0, 0].astype(BF16)

        for p, scr_ref in enumerate(scr_refs):
            rows_p = slice(p * pm * s_rows, (p + 1) * pm * s_rows)
            scr_ref[...] = x_ref[rows_p, :].astype(F32)
            gate = up = None
            for c in range(s_rows // 2):
                xc = jnp.concatenate([_from_token_rows(scr_ref, s_rows, pm, 2 * c),
                                      _from_token_rows(scr_ref, s_rows, pm, 2 * c + 1)], axis=1).astype(BF16)
                rows = slice(2 * c * LANES, (2 * c + 2) * LANES)
                gc = jnp.dot(xc, wgb[rows, :], preferred_element_type=F32)
                uc = jnp.dot(xc, wub[rows, :], preferred_element_type=F32)
                gate = gc if gate is None else gate + gc
                up = uc if up is None else up + uc
            a = (_silu(gate) * up).astype(BF16)
            y = jnp.dot(a, wdb[...], preferred_element_type=F32)
            blocks = jnp.stack([y[:, s * LANES:(s + 1) * LANES] for s in range(s_rows)], axis=0)
            y_ref[rows_p, :] = pltpu.einshape("stl->tsl", blocks).reshape(pm * s_rows, LANES).astype(BF16)

    @pl.when(i >= nu)
    def _():
        y_ref[...] = jnp.zeros_like(y_ref)


def _moe_experts(xs, blk_expert, n_used, wg, wu, wd, layer, bm):
    _, n_exp, d, de = wg.shape
    s_rows = d // LANES
    nb = xs.shape[0] // (bm * s_rows)
    blk = pl.BlockSpec((bm * s_rows, LANES), lambda i, be, nu: (jnp.minimum(i, nu[0] - 1), 0))
    grid_spec = pltpu.PrefetchScalarGridSpec(
        num_scalar_prefetch=2,
        grid=(nb,),
        in_specs=[blk,
                  pl.BlockSpec((1, 1, d, de), lambda i, be, nu: (layer, be[i], 0, 0)),
                  pl.BlockSpec((1, 1, d, de), lambda i, be, nu: (layer, be[i], 0, 0)),
                  pl.BlockSpec((1, 1, de, d), lambda i, be, nu: (layer, be[i], 0, 0))],
        out_specs=pl.BlockSpec((bm * s_rows, LANES), lambda i, be, nu: (i, 0)),
        scratch_shapes=[pltpu.VMEM((d, de), BF16), pltpu.VMEM((d, de), BF16), pltpu.VMEM((de, d), BF16)]
                       + [pltpu.VMEM((bm // MOE_PARTS * s_rows, LANES), F32)] * MOE_PARTS)
    return pl.pallas_call(
        functools.partial(_moe_kernel, s_rows=s_rows),
        grid_spec=grid_spec,
        out_shape=jax.ShapeDtypeStruct(xs.shape, BF16),
        compiler_params=pltpu.CompilerParams(vmem_limit_bytes=VMEM_LIMIT),
    )(blk_expert, n_used, xs, wg, wu, wd)


def _combine_kernel(slot_hbm, y_hbm, gate_ref, h2_ref, x1_ref, sg_ref, su_ref, sd_ref, g2_ref, lng_ref, lnb_ref,
                    out_ref, idx_ref, ybuf, f_ref, frow_ref, isem, rsem, *, order, n_steps, s_rows):
    i = pl.program_id(0)
    tt, d = h2_ref.shape
    n_rows = TOP_K * tt

    def idx_copy(tile, buf):
        return pltpu.make_async_copy(slot_hbm.at[tile], idx_ref.at[pl.ds(buf * n_rows, n_rows)], isem.at[buf])

    group = 8
    assert tt % group == 0

    def issue_token(buf, r0, u):
        base = pl.multiple_of(r0 * s_rows, group * s_rows)
        ibase = buf * n_rows + r0
        for k in range(TOP_K):
            t = idx_ref[ibase + (k * tt + u)]
            pltpu.make_async_copy(y_hbm.at[pl.ds(pl.multiple_of(t * s_rows, s_rows), s_rows), :],
                                  ybuf.at[buf, pl.ds(base + (k * tt + u) * s_rows, s_rows), :],
                                  rsem.at[buf]).start()

    @pl.when(i == 0)
    def _():
        idx_copy(0, 0).start()
        idx_copy(0, 0).wait()

        def first(gi, carry):
            for u in range(group):
                issue_token(0, gi * group, u)
            return carry
        lax.fori_loop(0, tt // group, first, 0)
        if n_steps > 1:
            idx_copy(1, 1).start()

    buf = i % 2
    nxt = 1 - buf
    has_next = i + 1 < n_steps

    @pl.when(has_next)
    def _():
        idx_copy(i + 1, nxt).wait()

    hb = h2_ref[...].astype(BF16)
    a = (_silu(jnp.dot(hb, sg_ref[...], preferred_element_type=F32))
         * jnp.dot(hb, su_ref[...], preferred_element_type=F32))
    f_ref[...] = jnp.dot(a.astype(BF16), sd_ref[...], preferred_element_type=F32)
    pltpu.make_async_copy(y_hbm.at[pl.ds(0, n_rows * s_rows), :], ybuf.at[buf], rsem.at[buf]).wait()

    def token_loop(prefetch):
        def body(gi, carry):
            r0 = gi * group
            base = pl.multiple_of(r0 * s_rows, group * s_rows)
            gbase = pl.multiple_of(r0 * TOP_K, group * TOP_K)
            for u in range(group):
                if prefetch:
                    issue_token(nxt, r0, u)
                gates = gate_ref[pl.ds(gbase + u * TOP_K, TOP_K), :]
                acc = jnp.zeros((s_rows, LANES), F32)
                for k in range(TOP_K):
                    rows = pl.ds(base + (k * tt + u) * s_rows, s_rows)
                    acc = acc + jnp.broadcast_to(gates[k:k + 1, :], (s_rows, LANES)) * ybuf[buf, rows, :].astype(F32)
                frow_ref[pl.ds(base + u * s_rows, s_rows), :] = acc
            return carry
        lax.fori_loop(0, tt // group, body, 0)

    @pl.when(has_next)
    def _():
        token_loop(True)

    @pl.when(jnp.logical_not(has_next))
    def _():
        token_loop(False)

    @pl.when(i + 2 < n_steps)
    def _():
        idx_copy(i + 2, buf).start()

    for s in range(s_rows):
        sl = slice(s * LANES, (s + 1) * LANES)
        f_ref[:, sl] = f_ref[:, sl] + _from_token_rows(frow_ref, s_rows, tt, s)
    g = order.group(i)
    r = DEEPNORM_ALPHA * x1_ref[...] + g2_ref[0, pl.ds(g, 1), :] * f_ref[...]
    out_ref[...] = _layer_norm(r, lng_ref[...], lnb_ref[...])


def _combine(slots, y_sorted, gate, h2, x1, sg, su, sd, mod, ln_g, ln_b, layer, order):
    n_tok, d = h2.shape
    tt = order.tile
    n_steps = order.n_tiles
    s_rows = d // LANES
    ds_ = sg.shape[1]
    row = pl.BlockSpec((1, d), lambda i: (0, 0))
    tok = pl.BlockSpec((tt, d), lambda i: (i, 0))
    return pl.pallas_call(
        functools.partial(_combine_kernel, order=order, n_steps=n_steps, s_rows=s_rows),
        grid=(n_steps,),
        in_specs=[pl.BlockSpec(memory_space=pl.ANY),
                  pl.BlockSpec(memory_space=pl.ANY),
                  pl.BlockSpec((tt * TOP_K, LANES), lambda i: (i, 0)),
                  tok, tok,
                  pl.BlockSpec((d, ds_), lambda i: (0, 0)),
                  pl.BlockSpec((d, ds_), lambda i: (0, 0)),
                  pl.BlockSpec((ds_, d), lambda i: (0, 0)),
                  _mod_spec(layer, MOD_G2, d, 1), row, row],
        out_specs=tok,
        out_shape=jax.ShapeDtypeStruct((n_tok, d), F32),
        scratch_shapes=[pltpu.SMEM((2 * TOP_K * tt,), jnp.int32),
                        pltpu.VMEM((2, TOP_K * tt * s_rows, LANES), BF16),
                        pltpu.VMEM((tt, d), F32),
                        pltpu.VMEM((tt * s_rows, LANES), F32),
                        pltpu.SemaphoreType.DMA((2,)),
                        pltpu.SemaphoreType.DMA((2,))],
        compiler_params=pltpu.CompilerParams(vmem_limit_bytes=VMEM_LIMIT, disable_bounds_checks=True),
    )(slots, y_sorted, gate, h2, x1, sg, su, sd, mod, ln_g.reshape(1, d), ln_b.reshape(1, d))


def _moe_layer(produced, layer, mod, order, router_w, router_b, wg, wu, wd, sg, su, sd, ln_g, ln_b):
    x1, h2, hp = produced
    n_tok, d = h2.shape
    n_exp = router_w.shape[1]
    bm = MOE_BLOCK
    tt = order.tile
    eidx_t, gate_t, rank_t, cnt = _router(h2, router_w, router_b)

    counts = cnt[:, 0].astype(jnp.int32)
    padded = (counts + bm - 1) // bm * bm
    pad_end = jnp.cumsum(padded).astype(jnp.int32)
    pad_start = pad_end - padded
    experts = jnp.arange(n_exp, dtype=jnp.int32)
    slot_t = rank_t + jnp.sum(jnp.where(eidx_t[:, :, None] == experts, pad_start, 0), axis=-1)
    n_slots = n_tok * TOP_K + n_exp * bm
    nb = n_slots // bm
    n_used = pad_end[-1:] // bm
    blk_expert = jnp.minimum(jnp.sum(pad_end[None, :] <= (jnp.arange(nb, dtype=jnp.int32) * bm)[:, None], axis=1),
                             n_exp - 1).astype(jnp.int32)
    n_tiles = n_tok // tt
    slots_tok = slot_t.T.reshape(n_tiles, tt * TOP_K)
    slots_k = slot_t.reshape(TOP_K, n_tiles, tt).transpose(1, 0, 2).reshape(n_tiles, TOP_K * tt)

    xs = _dispatch(hp, slots_tok, pad_end, padded, n_used, n_slots, bm)
    ys = _moe_experts(xs, blk_expert, n_used, wg, wu, wd, layer, bm)
    gates = jnp.broadcast_to(gate_t.T.reshape(n_tok * TOP_K, 1), (n_tok * TOP_K, LANES))
    return _combine(slots_k, ys, gates, h2, x1, sg[layer].astype(BF16), su[layer].astype(BF16),
                    sd[layer].astype(BF16), mod, ln_g[layer], ln_b[layer], layer, order)


def kernel(x, c, ctx, c_ctx, ada_w, ada_b, ln_mix_g, ln_mix_b, ln_ffn_g, ln_ffn_b, ret_w_in, ret_w_out,
           ret_decay_fwd, ret_decay_bwd, pool_w, pool_scale, router_w, router_b, exp_w_gate, exp_w_up,
           exp_w_down, sh_w_gate, sh_w_up, sh_w_down):
    batch, seq, d = x.shape
    ctx_len = ctx.shape[1]
    heads = RET_HEADS
    qk_dim = d
    head_dim = qk_dim // heads
    v_dim = ret_w_out.shape[1]
    assert ada_w.shape[0] == DEPTH == 2 and batch + 1 <= MOD_ROWS

    cvec = jnp.zeros((MOD_ROWS, d), F32).at[:batch].set(c.astype(F32)).at[batch].set(c_ctx.astype(F32))
    mod = _modulation(cvec, ada_w.astype(F32), ada_b.astype(F32))

    x2d = x.reshape(batch * seq, d)
    ctx2d = ctx.reshape(batch * ctx_len, d)
    order0 = _Order(batch, seq, ctx_len, TOKEN_TILE)

    w_in = ret_w_in[0].astype(F32)
    tabs = _rope_tables(seq, head_dim, TOKEN_TILE)
    qk = _inproj(x2d, ctx2d, mod, w_in, 0, 2 * qk_dim, order0, BF16, rope_tabs=tabs, k_col0=qk_dim,
                 head_dim=head_dim)
    v = _inproj(x2d, ctx2d, mod, w_in, 2 * qk_dim, v_dim, order0, BF16)
    g = _inproj(x2d, ctx2d, mod, w_in, 2 * qk_dim + v_dim, v_dim, order0, F32)
    og = _retention(qk, v, g, ret_decay_fwd[0], ret_decay_bwd[0], batch, ctx_len + seq, ctx_len)
    produced = _outproj(og, x2d, ctx2d, ret_w_out[0].astype(BF16), mod, ln_mix_g[0], ln_mix_b[0], order0)
    xa = _moe_layer(produced, 0, mod, order0, router_w[0], router_b[0], exp_w_gate, exp_w_up, exp_w_down,
                    sh_w_gate, sh_w_up, sh_w_down, ln_ffn_g, ln_ffn_b)

    order1 = _Order(batch, seq, 0, TOKEN_TILE)
    produced = _pool_layer(xa, order0, pool_w[0], pool_scale[0], mod, ln_mix_g[1], ln_mix_b[1], 1, batch, seq)
    xb = _moe_layer(produced, 1, mod, order1, router_w[1], router_b[1], exp_w_gate, exp_w_up, exp_w_down,
                    sh_w_gate, sh_w_up, sh_w_down, ln_ffn_g, ln_ffn_b)
    return xb.reshape(batch, seq, d)
```

```python
import functools

import jax
import jax.numpy as jnp
from jax import lax
from jax.experimental import pallas as pl
from jax.experimental.pallas import tpu as pltpu

F32 = jnp.float32
BF16 = jnp.bfloat16
U32 = jnp.uint32
HIGHEST = lax.Precision.HIGHEST

DEPTH = 2
GRID_W = 64
RET_HEADS = 8
ROPE_BASE = 10000.0
POOL_WINDOWS = (2, 4, 8, 16)
N_EXPERT_GROUPS = 8
TOPK_GROUPS = 4
TOP_K = 8
ROUTED_SCALE = 2.5
LN_EPS = 1e-5
N_MOD = 6
DEEPNORM_ALPHA = (2.0 * DEPTH) ** 0.25
MOD_SH1, MOD_SC1, MOD_G1, MOD_SH2, MOD_SC2, MOD_G2 = range(N_MOD)

LANES = 128
TOKEN_TILE = 256
RET_BLOCK = 2816
RET_CHUNK = 256
ROUTER_TILE = 512
MOE_BLOCK = 512
MOE_PARTS = 2
MOD_ROWS = 8
VMEM_LIMIT = 56 * 1024 * 1024


def _silu(x):
    return x * jax.nn.sigmoid(x)


def _layer_norm(r, g, b):
    mu = jnp.mean(r, axis=-1, keepdims=True)
    var = jnp.mean(jnp.square(r - mu), axis=-1, keepdims=True)
    return (r - mu) * lax.rsqrt(var + LN_EPS) * g + b


def _to_token_rows(scr_ref, h):
    tm, d = h.shape
    s_rows = d // LANES
    for s in range(s_rows):
        scr_ref[pl.ds(s, tm, stride=s_rows), :] = h[:, s * LANES:(s + 1) * LANES]
    return scr_ref[...].astype(BF16)


def _from_token_rows(scr_ref, rows, n, s):
    return scr_ref[pl.ds(s, n, stride=rows), :]


class _Order:
    def __init__(self, batch, seq, ctx_len, tile):
        assert seq % tile == 0 and ctx_len % tile == 0
        self.batch, self.seq, self.ctx_len, self.tile = batch, seq, ctx_len, tile
        self.ctx_tiles = ctx_len // tile
        self.lat_tiles = seq // tile
        self.per_batch = self.ctx_tiles + self.lat_tiles
        self.n_tiles = batch * self.per_batch
        self.n_tok = self.n_tiles * tile

    def split(self, i):
        b = i // self.per_batch
        r = i % self.per_batch
        return b, r, r < self.ctx_tiles

    def group(self, i):
        b, _, is_ctx = self.split(i)
        return jnp.where(is_ctx, self.batch, b)

    def lat_block(self, i):
        b, r, _ = self.split(i)
        return b * self.lat_tiles + jnp.maximum(r - self.ctx_tiles, 0)

    def ctx_block(self, i):
        b, r, _ = self.split(i)
        return b * self.ctx_tiles + jnp.minimum(r, max(self.ctx_tiles - 1, 0))


def _mod_kernel(c_ref, w_ref, b_ref, o_ref):
    s = _silu(c_ref[...])
    o_ref[0] = jnp.dot(s, w_ref[0], precision=HIGHEST, preferred_element_type=F32) + b_ref[0]


def _modulation(cvec, ada_w, ada_b):
    depth, d, n = ada_w.shape
    tn = 1024
    return pl.pallas_call(
        _mod_kernel,
        grid=(depth, n // tn),
        in_specs=[pl.BlockSpec((MOD_ROWS, d), lambda l, j: (0, 0)),
                  pl.BlockSpec((1, d, tn), lambda l, j: (l, 0, j)),
                  pl.BlockSpec((1, 1, tn), lambda l, j: (l, 0, j))],
        out_specs=pl.BlockSpec((1, MOD_ROWS, tn), lambda l, j: (l, 0, j)),
        out_shape=jax.ShapeDtypeStruct((depth, MOD_ROWS, n), F32),
        compiler_params=pltpu.CompilerParams(vmem_limit_bytes=VMEM_LIMIT),
    )(cvec, ada_w, ada_b.reshape(depth, 1, n))


def _mod_spec(layer, chunk, d, n_grid_args):
    if n_grid_args == 1:
        return pl.BlockSpec((1, MOD_ROWS, d), lambda i: (layer, 0, chunk))
    return pl.BlockSpec((1, MOD_ROWS, d), lambda j, i: (layer, 0, chunk))


def _inproj_kernel(x_ref, c_ref, sc_ref, sh_ref, w_ref, *rest, order, rope, k_col0, head_dim):
    i = pl.program_id(1)
    j = pl.program_id(0)
    if rope:
        cos_ref, sin_ref, o_ref, wb_ref = rest
    else:
        o_ref, wb_ref = rest

    @pl.when(i == 0)
    def _():
        wb_ref[...] = w_ref[...].astype(BF16)

    g = order.group(i)
    _, _, is_ctx = order.split(i)
    sc = sc_ref[0, pl.ds(g, 1), :]
    sh = sh_ref[0, pl.ds(g, 1), :]
    xin = jnp.where(is_ctx, c_ref[...], x_ref[...])
    a = (xin * (1.0 + sc) + sh).astype(BF16)
    acc = jnp.dot(a, wb_ref[...], preferred_element_type=F32)
    if not rope:
        o_ref[...] = acc.astype(o_ref.dtype)
        return
    tn = acc.shape[1]
    lane = lax.broadcasted_iota(jnp.int32, acc.shape, 1)
    partner = jnp.where((lane & 1) == 0, pltpu.roll(acc, tn - 1, 1), pltpu.roll(acc, 1, 1))
    cos = cos_ref[...]
    sin = sin_ref[...]
    scale = jnp.where(j * tn >= k_col0, head_dim ** -0.5, 1.0).astype(F32)
    for h in range(tn // head_dim):
        sl = slice(h * head_dim, (h + 1) * head_dim)
        o_ref[:, sl] = ((acc[:, sl] * cos + partner[:, sl] * sin) * scale).astype(o_ref.dtype)


def _inproj(x2d, ctx2d, mod, w_in, col0, n, order, out_dtype, rope_tabs=None, k_col0=0, head_dim=0):
    d = w_in.shape[0]
    tm = order.tile
    tn = min(n, 2048)
    assert col0 % tn == 0 and n % tn == 0
    rope = rope_tabs is not None
    in_specs = [pl.BlockSpec((tm, d), lambda j, i: (order.lat_block(i), 0)),
                pl.BlockSpec((tm, d), lambda j, i: (order.ctx_block(i), 0)),
                _mod_spec(0, MOD_SC1, d, 2), _mod_spec(0, MOD_SH1, d, 2),
                pl.BlockSpec((d, tn), lambda j, i: (0, col0 // tn + j), pipeline_mode=pl.Buffered(1))]
    args = [x2d, ctx2d, mod, mod, w_in]
    if rope:
        def tab_map(j, i):
            _, r, is_ctx = order.split(i)
            return (jnp.where(is_ctx, order.lat_tiles, r - order.ctx_tiles), 0)
        in_specs += [pl.BlockSpec((tm, head_dim), tab_map), pl.BlockSpec((tm, head_dim), tab_map)]
        args += list(rope_tabs)
    return pl.pallas_call(
        functools.partial(_inproj_kernel, order=order, rope=rope, k_col0=k_col0, head_dim=head_dim),
        grid=(n // tn, order.n_tiles),
        in_specs=in_specs,
        out_specs=pl.BlockSpec((tm, tn), lambda j, i: (i, j)),
        out_shape=jax.ShapeDtypeStruct((order.n_tok, n), out_dtype),
        scratch_shapes=[pltpu.VMEM((d, tn), BF16)],
        compiler_params=pltpu.CompilerParams(vmem_limit_bytes=VMEM_LIMIT),
    )(*args)


def _rope_tables(seq, head_dim, tile):
    n_freq = head_dim // 4
    pos = jnp.arange(seq, dtype=jnp.int32)
    row = (pos // GRID_W).astype(F32)
    col = (pos % GRID_W).astype(F32)
    freqs = ROPE_BASE ** (-jnp.arange(n_freq, dtype=F32) / n_freq)
    ang = jnp.concatenate([row[:, None] * freqs, col[:, None] * freqs], axis=-1)
    cos = jnp.repeat(jnp.cos(ang), 2, axis=-1)
    sin = jnp.stack([-jnp.sin(ang), jnp.sin(ang)], axis=-1).reshape(seq, head_dim)
    ident = jnp.zeros((tile, head_dim), F32)
    return jnp.concatenate([cos, ident + 1.0], axis=0), jnp.concatenate([sin, ident], axis=0)


def _ret_kernel(q_ref, k_ref, v_ref, g_ref, df_ref, db_ref, o_ref, sf_ref, sb_ref, obuf_ref, *,
                n_blocks, chunks_per_block, ctx_chunks):
    C = RET_CHUNK
    NB, CPB, NCC = n_blocks, chunks_per_block, ctx_chunks
    st = pl.program_id(2)
    phase = st // (NB + 1)
    s = st % (NB + 1)
    ldf = jax.nn.log_sigmoid(df_ref[0])[:, :1]
    ldb = jax.nn.log_sigmoid(db_ref[0])[:, :1]
    pos = lax.broadcasted_iota(jnp.int32, (C, 1), 0).astype(F32)
    tdims = (((0,), (0,)), ((), ()))

    @pl.when(phase == 0)
    def _backward_states():
        @pl.when(s == 0)
        def _():
            sb_ref[...] = jnp.zeros_like(sb_ref)

        q_dec = jnp.exp(ldb * (C - pos))
        k_dec = jnp.exp(ldb * pos)
        chunk_dec = jnp.exp(ldb * C)

        def chunk(blk, c):
            rows = pl.ds(c * C, C)
            q = q_ref[rows, :].astype(F32)
            k = k_ref[rows, :].astype(F32)
            v = v_ref[rows, :]
            state = sb_ref[...]
            goff = pl.multiple_of((blk * CPB + c) * C, C)
            obuf_ref[pl.ds(goff, C), :] = jnp.dot((q * q_dec).astype(BF16), state.astype(BF16),
                                                  preferred_element_type=F32)
            sb_ref[...] = state * chunk_dec + lax.dot_general((k * k_dec).astype(BF16), v, tdims,
                                                              preferred_element_type=F32)

        @pl.when(s == 0)
        def _():
            for c in reversed(range(NCC)):
                chunk(0, c)

        @pl.when((s > 0) & (s < NB))
        def _():
            for c in reversed(range(CPB)):
                chunk(NB - s, c)

        @pl.when(s == NB)
        def _():
            for c in reversed(range(NCC, CPB)):
                chunk(0, c)

    @pl.when((phase == 1) & (s < NB))
    def _forward_and_output():
        @pl.when(s == 0)
        def _():
            sf_ref[...] = jnp.zeros_like(sf_ref)

        q_dec = jnp.exp(ldf * (pos + 1.0))
        k_dec = jnp.exp(ldf * (C - 1.0 - pos))
        chunk_dec = jnp.exp(ldf * C)
        ii = lax.broadcasted_iota(jnp.int32, (C, C), 0)
        jj = lax.broadcasted_iota(jnp.int32, (C, C), 1)
        rel = (ii - jj).astype(F32)
        decay = jnp.where(rel >= 0, jnp.exp(ldf * jnp.maximum(rel, 0.0)), jnp.exp(ldb * jnp.maximum(-rel, 0.0)))

        for c in range(CPB):
            rows = pl.ds(c * C, C)
            qb = q_ref[rows, :]
            kb = k_ref[rows, :]
            v = v_ref[rows, :]
            q = qb.astype(F32)
            k = kb.astype(F32)
            state = sf_ref[...]
            scores = lax.dot_general(qb, kb, (((1,), (1,)), ((), ())), preferred_element_type=F32) * decay
            goff = pl.multiple_of((s * CPB + c) * C, C)
            o = (jnp.dot(scores.astype(BF16), v, preferred_element_type=F32)
                 + jnp.dot((q * q_dec).astype(BF16), state.astype(BF16), preferred_element_type=F32)
                 + obuf_ref[pl.ds(goff, C), :])
            sf_ref[...] = state * chunk_dec + lax.dot_general((k * k_dec).astype(BF16), v, tdims,
                                                              preferred_element_type=F32)
            mu = jnp.mean(o, axis=-1, keepdims=True)
            var = jnp.mean(jnp.square(o - mu), axis=-1, keepdims=True)
            on = (o - mu) * lax.rsqrt(var + LN_EPS)
            o_ref[rows, :] = (_silu(g_ref[rows, :]) * on).astype(o_ref.dtype)


def _retention(qk, v, g, raw_fwd, raw_bwd, batch, tokens_per_batch, ctx_len):
    n_tok, qk2 = qk.shape
    v_dim = v.shape[1]
    heads = RET_HEADS
    dk = qk2 // 2 // heads
    dv = v_dim // heads
    tb = next(t for t in range(RET_BLOCK, 0, -RET_CHUNK) if tokens_per_batch % t == 0)
    assert tb % RET_CHUNK == 0 and ctx_len % RET_CHUNK == 0 and ctx_len <= tb
    nb = tokens_per_batch // tb
    steps = 2 * (nb + 1)

    def blk(st):
        phase = st // (nb + 1)
        s = st % (nb + 1)
        back = jnp.where(s == 0, 0, nb - s)
        return jnp.where(phase == 0, back, jnp.minimum(s, nb - 1))

    def fwd_blk(st):
        phase = st // (nb + 1)
        s = st % (nb + 1)
        return jnp.where(phase == 0, 0, jnp.minimum(s, nb - 1))

    dec = lambda raw: jnp.broadcast_to(raw.astype(F32)[:, None, None], (heads, 1, LANES))
    return pl.pallas_call(
        functools.partial(_ret_kernel, n_blocks=nb, chunks_per_block=tb // RET_CHUNK,
                          ctx_chunks=ctx_len // RET_CHUNK),
        grid=(batch, heads, steps),
        in_specs=[pl.BlockSpec((tb, dk), lambda b, h, st: (b * nb + blk(st), h)),
                  pl.BlockSpec((tb, dk), lambda b, h, st: (b * nb + blk(st), heads + h)),
                  pl.BlockSpec((tb, dv), lambda b, h, st: (b * nb + blk(st), h)),
                  pl.BlockSpec((tb, dv), lambda b, h, st: (b * nb + fwd_blk(st), h)),
                  pl.BlockSpec((1, 1, LANES), lambda b, h, st: (h, 0, 0)),
                  pl.BlockSpec((1, 1, LANES), lambda b, h, st: (h, 0, 0))],
        out_specs=pl.BlockSpec((tb, dv), lambda b, h, st: (b * nb + fwd_blk(st), h)),
        out_shape=jax.ShapeDtypeStruct((n_tok, v_dim), BF16),
        scratch_shapes=[pltpu.VMEM((dk, dv), F32), pltpu.VMEM((dk, dv), F32),
                        pltpu.VMEM((tokens_per_batch, dv), F32)],
        compiler_params=pltpu.CompilerParams(vmem_limit_bytes=VMEM_LIMIT),
    )(qk, qk, v, g, dec(raw_fwd), dec(raw_bwd))


def _outproj_kernel(o_ref, x_ref, c_ref, w_ref, g1_ref, sc2_ref, sh2_ref, lng_ref, lnb_ref,
                    x1_ref, h2_ref, hp_ref, scr_ref, *, order):
    i = pl.program_id(0)
    g = order.group(i)
    _, _, is_ctx = order.split(i)
    y = jnp.dot(o_ref[...], w_ref[...], preferred_element_type=F32)
    xin = jnp.where(is_ctx, c_ref[...], x_ref[...])
    r = DEEPNORM_ALPHA * xin + g1_ref[0, pl.ds(g, 1), :] * y
    x1 = _layer_norm(r, lng_ref[...], lnb_ref[...])
    x1_ref[...] = x1
    h2 = x1 * (1.0 + sc2_ref[0, pl.ds(g, 1), :]) + sh2_ref[0, pl.ds(g, 1), :]
    h2_ref[...] = h2
    hp_ref[...] = _to_token_rows(scr_ref, h2)


def _outproj(og, x2d, ctx2d, w_out, mod, ln_g, ln_b, order):
    v_dim, d = w_out.shape
    tm = order.tile
    s_rows = d // LANES
    row = pl.BlockSpec((1, d), lambda i: (0, 0))
    tok = pl.BlockSpec((tm, d), lambda i: (i, 0))
    return pl.pallas_call(
        functools.partial(_outproj_kernel, order=order),
        grid=(order.n_tiles,),
        in_specs=[pl.BlockSpec((tm, v_dim), lambda i: (i, 0)),
                  pl.BlockSpec((tm, d), lambda i: (order.lat_block(i), 0)),
                  pl.BlockSpec((tm, d), lambda i: (order.ctx_block(i), 0)),
                  pl.BlockSpec((v_dim, d), lambda i: (0, 0), pipeline_mode=pl.Buffered(1)),
                  _mod_spec(0, MOD_G1, d, 1), _mod_spec(0, MOD_SC2, d, 1), _mod_spec(0, MOD_SH2, d, 1),
                  row, row],
        out_specs=[tok, tok, pl.BlockSpec((tm * s_rows, LANES), lambda i: (i, 0))],
        out_shape=[jax.ShapeDtypeStruct((order.n_tok, d), F32)] * 2
                  + [jax.ShapeDtypeStruct((order.n_tok * s_rows, LANES), BF16)],
        scratch_shapes=[pltpu.VMEM((tm * s_rows, LANES), F32)],
        compiler_params=pltpu.CompilerParams(vmem_limit_bytes=VMEM_LIMIT),
    )(og, x2d, ctx2d, w_out, mod, mod, mod, ln_g.reshape(1, d), ln_b.reshape(1, d))


def _pool_kernel(x_ref, pw_ref, ps_ref, sc1_ref, sh1_ref, g1_ref, sc2_ref, sh2_ref, lng_ref, lnb_ref,
                 x1_ref, h2_ref, hp_ref, y_ref, scr_ref, *, lat_tiles):
    i = pl.program_id(0)
    g = i // lat_tiles
    x = x_ref[...]
    tm, d = x.shape
    gd = d // len(POOL_WINDOWS)
    h = x * (1.0 + sc1_ref[0, pl.ds(g, 1), :]) + sh1_ref[0, pl.ds(g, 1), :]
    ni = lax.broadcasted_iota(jnp.int32, (tm, tm), 0)
    mi = lax.broadcasted_iota(jnp.int32, (tm, tm), 1)
    same_row = (ni // GRID_W) == (mi // GRID_W)
    delta = mi - ni
    col = lax.broadcasted_iota(jnp.int32, (tm, 1), 0) % GRID_W
    for gi, w in enumerate(POOL_WINDOWS):
        window = jnp.where(same_row & (delta >= -(w // 2)) & (delta < w // 2), 1.0, 0.0).astype(BF16)
        count = (jnp.minimum(col + w // 2, GRID_W) - jnp.maximum(col - w // 2, 0)).astype(F32)
        hg = h[:, gi * gd:(gi + 1) * gd]
        h_hi = hg.astype(BF16)
        r1 = hg - h_hi.astype(F32)
        h_mid = r1.astype(BF16)
        h_lo = (r1 - h_mid.astype(F32)).astype(BF16)
        total = (jnp.dot(window, h_hi, preferred_element_type=F32)
                 + jnp.dot(window, h_mid, preferred_element_type=F32)
                 + jnp.dot(window, h_lo, preferred_element_type=F32))
        p = total / count - hg
        y_ref[:, gi * gd:(gi + 1) * gd] = jnp.dot(p.astype(BF16), pw_ref[gi], preferred_element_type=F32)
    y = y_ref[...] * ps_ref[...]
    r = DEEPNORM_ALPHA * x + g1_ref[0, pl.ds(g, 1), :] * y
    x1 = _layer_norm(r, lng_ref[...], lnb_ref[...])
    x1_ref[...] = x1
    h2 = x1 * (1.0 + sc2_ref[0, pl.ds(g, 1), :]) + sh2_ref[0, pl.ds(g, 1), :]
    h2_ref[...] = h2
    hp_ref[...] = _to_token_rows(scr_ref, h2)


def _pool_layer(x_prev, order0, pool_w, pool_scale, mod, ln_g, ln_b, layer, batch, seq):
    d = x_prev.shape[1]
    tm = order0.tile
    s_rows = d // LANES
    lat_tiles = seq // tm
    n_tiles = batch * lat_tiles
    ng, gd, _ = pool_w.shape

    def src(i):
        b = i // lat_tiles
        return (b * order0.per_batch + order0.ctx_tiles + i % lat_tiles, 0)

    row = pl.BlockSpec((1, d), lambda i: (0, 0))
    tok = pl.BlockSpec((tm, d), lambda i: (i, 0))
    ms = lambda chunk: _mod_spec(layer, chunk, d, 1)
    return pl.pallas_call(
        functools.partial(_pool_kernel, lat_tiles=lat_tiles),
        grid=(n_tiles,),
        in_specs=[pl.BlockSpec((tm, d), src),
                  pl.BlockSpec((ng, gd, gd), lambda i: (0, 0, 0)),
                  row, ms(MOD_SC1), ms(MOD_SH1), ms(MOD_G1), ms(MOD_SC2), ms(MOD_SH2), row, row],
        out_specs=[tok, tok, pl.BlockSpec((tm * s_rows, LANES), lambda i: (i, 0))],
        out_shape=[jax.ShapeDtypeStruct((n_tiles * tm, d), F32)] * 2
                  + [jax.ShapeDtypeStruct((n_tiles * tm * s_rows, LANES), BF16)],
        scratch_shapes=[pltpu.VMEM((tm, d), F32), pltpu.VMEM((tm * s_rows, LANES), F32)],
        compiler_params=pltpu.CompilerParams(vmem_limit_bytes=VMEM_LIMIT),
    )(x_prev, pool_w.astype(BF16), pool_scale.reshape(1, d), mod, mod, mod, mod, mod,
      ln_g.reshape(1, d), ln_b.reshape(1, d))


def _route_tile(h, w_ref, b_ref, eidx_ref, gate_ref, rank_ref, cnt_ref, carry_ref):
    i = pl.program_id(0)
    n_exp = w_ref.shape[0]
    tr = h.shape[0]
    per = n_exp // N_EXPERT_GROUPS
    neg = -jnp.inf

    @pl.when(i == 0)
    def _():
        carry_ref[...] = jnp.zeros_like(carry_ref)

    logits = lax.dot_general(w_ref[...], h, (((1,), (1,)), ((), ())),
                             precision=HIGHEST, preferred_element_type=F32)
    scores = jax.nn.sigmoid(logits)
    biased = scores + b_ref[...]

    sub_idx = lax.broadcasted_iota(jnp.int32, (per, tr), 0)
    group_score = []
    for gi in range(N_EXPERT_GROUPS):
        sub = biased[gi * per:(gi + 1) * per, :]
        m1 = jnp.max(sub, axis=0, keepdims=True)
        first = jnp.min(jnp.where(sub == m1, sub_idx, per), axis=0, keepdims=True)
        m2 = jnp.max(jnp.where(sub_idx == first, neg, sub), axis=0, keepdims=True)
        group_score.append(m1 + m2)
    masked = []
    for gi in range(N_EXPERT_GROUPS):
        ahead = jnp.zeros((1, tr), jnp.int32)
        for gj in range(N_EXPERT_GROUPS):
            if gj == gi:
                continue
            wins = (group_score[gj] >= group_score[gi]) if gj < gi else (group_score[gj] > group_score[gi])
            ahead = ahead + wins.astype(jnp.int32)
        keep = ahead < TOPK_GROUPS
        masked.append(jnp.where(keep, biased[gi * per:(gi + 1) * per, :], neg))
    masked = jnp.concatenate(masked, axis=0)

    eid = lax.broadcasted_iota(jnp.int32, (n_exp, tr), 0)
    picks, gates = [], []
    onehot = jnp.zeros((n_exp, tr), F32)
    for _ in range(TOP_K):
        m = jnp.max(masked, axis=0, keepdims=True)
        first = jnp.min(jnp.where(masked == m, eid, n_exp), axis=0, keepdims=True)
        sel = eid == first
        gates.append(jnp.sum(jnp.where(sel, scores, 0.0), axis=0, keepdims=True))
        picks.append(first)
        masked = jnp.where(sel, neg, masked)
        onehot = onehot + sel.astype(F32)
    gate_sum = gates[0]
    for gk in gates[1:]:
        gate_sum = gate_sum + gk

    si = lax.broadcasted_iota(jnp.int32, (tr, tr), 0)
    ti = lax.broadcasted_iota(jnp.int32, (tr, tr), 1)
    before = jnp.where(si < ti, 1.0, 0.0).astype(BF16)
    prefix = jnp.dot(onehot.astype(BF16), before, preferred_element_type=F32) + carry_ref[:, :1]
    for k in range(TOP_K):
        eidx_ref[k:k + 1, :] = picks[k]
        gate_ref[k:k + 1, :] = gates[k] / gate_sum * ROUTED_SCALE
        rank_ref[k:k + 1, :] = jnp.sum(jnp.where(eid == picks[k], prefix, 0.0), axis=0,
                                       keepdims=True).astype(jnp.int32)
    carry_ref[...] = carry_ref[...] + jnp.sum(onehot, axis=1, keepdims=True)
    cnt_ref[...] = carry_ref[...]


def _router_kernel(h_ref, w_ref, b_ref, eidx_ref, gate_ref, rank_ref, cnt_ref, carry_ref):
    _route_tile(h_ref[...], w_ref, b_ref, eidx_ref, gate_ref, rank_ref, cnt_ref, carry_ref)


def _router(h2, router_w, router_b):
    n_tok, d = h2.shape
    n_exp = router_w.shape[1]
    tr = ROUTER_TILE
    assert n_tok % tr == 0
    sel = pl.BlockSpec((TOP_K, tr), lambda i: (0, i))
    return pl.pallas_call(
        _router_kernel,
        grid=(n_tok // tr,),
        in_specs=[pl.BlockSpec((tr, d), lambda i: (i, 0)),
                  pl.BlockSpec((n_exp, d), lambda i: (0, 0)),
                  pl.BlockSpec((n_exp, 1), lambda i: (0, 0))],
        out_specs=[sel, sel, sel, pl.BlockSpec((n_exp, LANES), lambda i: (0, 0))],
        out_shape=[jax.ShapeDtypeStruct((TOP_K, n_tok), jnp.int32),
                   jax.ShapeDtypeStruct((TOP_K, n_tok), F32),
                   jax.ShapeDtypeStruct((TOP_K, n_tok), jnp.int32),
                   jax.ShapeDtypeStruct((n_exp, LANES), F32)],
        scratch_shapes=[pltpu.VMEM((n_exp, LANES), F32)],
        compiler_params=pltpu.CompilerParams(vmem_limit_bytes=VMEM_LIMIT),
    )(h2, router_w.astype(F32).T, router_b.astype(F32).reshape(n_exp, 1))


def _dispatch_kernel(pe_ref, pd_ref, nu_ref, slot_hbm, hp_ref, xs_hbm, idx_ref, zero_ref, isem, zsem, rsem, *,
                     n_steps, n_exp, n_blocks, s_rows, bm):
    i = pl.program_id(0)
    tt = hp_ref.shape[0] // s_rows
    zrows = bm * s_rows
    per_tile = tt * TOP_K

    def idx_copy(tile, buf):
        return pltpu.make_async_copy(slot_hbm.at[tile], idx_ref.at[pl.ds(buf * per_tile, per_tile)], isem.at[buf])

    def zero_copy(row0):
        return pltpu.make_async_copy(zero_ref, xs_hbm.at[pl.ds(pl.multiple_of(row0, s_rows), zrows), :], zsem)

    @pl.when(i == 0)
    def _():
        zero_ref[...] = jnp.zeros_like(zero_ref)
        nu = nu_ref[0]

        def fill(fn):
            def per_expert(e, carry):
                @pl.when(pd_ref[e] > 0)
                def _():
                    fn(zero_copy((pe_ref[e] - bm) * s_rows))
                return carry
            lax.fori_loop(0, n_exp, per_expert, 0)

            def per_tail(b, carry):
                fn(zero_copy(b * zrows))
                return carry
            lax.fori_loop(nu, n_blocks, per_tail, 0)

        fill(lambda cp: cp.start())
        fill(lambda cp: cp.wait())
        idx_copy(0, 0).start()
        idx_copy(0, 0).wait()
        if n_steps > 1:
            idx_copy(1, 1).start()

    buf = i % 2

    @pl.when(i > 0)
    def _():
        idx_copy(i, buf).wait()

    group = 8
    assert tt % group == 0

    def body(gi, carry):
        base = pl.multiple_of(gi * (group * s_rows), group * s_rows)
        ibase = buf * per_tile + gi * (group * TOP_K)
        for u in range(group):
            src = hp_ref.at[pl.ds(base + u * s_rows, s_rows), :]
            for k in range(TOP_K):
                t = idx_ref[ibase + (u * TOP_K + k)]
                pltpu.make_async_copy(src, xs_hbm.at[pl.ds(pl.multiple_of(t * s_rows, s_rows), s_rows), :],
                                      rsem).start()
        return carry

    lax.fori_loop(0, tt // group, body, 0)

    @pl.when((i > 0) & (i + 1 < n_steps))
    def _():
        idx_copy(i + 1, 1 - buf).start()

    for _ in range(TOP_K):
        pltpu.make_async_copy(hp_ref, xs_hbm.at[pl.ds(0, tt * s_rows), :], rsem).wait()


def _dispatch(hp, slots, pad_end, padded, n_used, n_slots, bm):
    rows, _ = hp.shape
    n_steps, per_tile = slots.shape
    tt = per_tile // TOP_K
    s_rows = rows // (n_steps * tt)
    n_exp = pad_end.shape[0]
    grid_spec = pltpu.PrefetchScalarGridSpec(
        num_scalar_prefetch=3,
        grid=(n_steps,),
        in_specs=[pl.BlockSpec(memory_space=pl.ANY),
                  pl.BlockSpec((tt * s_rows, LANES), lambda i, pe, pd, nu: (i, 0))],
        out_specs=pl.BlockSpec(memory_space=pl.ANY),
        scratch_shapes=[pltpu.SMEM((2 * per_tile,), jnp.int32),
                        pltpu.VMEM((bm * s_rows, LANES), BF16),
                        pltpu.SemaphoreType.DMA((2,)),
                        pltpu.SemaphoreType.DMA(()),
                        pltpu.SemaphoreType.DMA(())])
    return pl.pallas_call(
        functools.partial(_dispatch_kernel, n_steps=n_steps, n_exp=n_exp, n_blocks=n_slots // bm,
                          s_rows=s_rows, bm=bm),
        grid_spec=grid_spec,
        out_shape=jax.ShapeDtypeStruct((n_slots * s_rows, LANES), BF16),
        compiler_params=pltpu.CompilerParams(vmem_limit_bytes=VMEM_LIMIT, disable_bounds_checks=True),
    )(pad_end, padded, n_used, slots, hp)


def _moe_kernel(be_ref, nu_ref, x_ref, wg_ref, wu_ref, wd_ref, y_ref, wgb, wub, wdb, *scr_refs, s_rows):
    i = pl.program_id(0)
    nu = nu_ref[0]
    parts = len(scr_refs)
    pm = x_ref.shape[0] // s_rows // parts

    @pl.when(i < nu)
    def _():
        @pl.when((i == 0) | (be_ref[i] != be_ref[jnp.maximum(i - 1, 0)]))
        def _():
            wgb[...] = wg_ref[0, 0].astype(BF16)
            wub[...] = wu_ref[0, 0].astype(BF16)
            wdb[...] = wd_ref[0, 0].astype(BF16)

        for p, scr_ref in enumerate(scr_refs):
            rows_p = slice(p * pm * s_rows, (p + 1) * pm * s_rows)
            scr_ref[...] = x_ref[rows_p, :].astype(F32)
            gate = up = None
            for c in range(s_rows // 2):
                xc = jnp.concatenate([_from_token_rows(scr_ref, s_rows, pm, 2 * c),
                                      _from_token_rows(scr_ref, s_rows, pm, 2 * c + 1)], axis=1).astype(BF16)
                rows = slice(2 * c * LANES, (2 * c + 2) * LANES)
                gc = jnp.dot(xc, wgb[rows, :], preferred_element_type=F32)
                uc = jnp.dot(xc, wub[rows, :], preferred_element_type=F32)
                gate = gc if gate is None else gate + gc
                up = uc if up is None else up + uc
            a = (_silu(gate) * up).astype(BF16)
            y = jnp.dot(a, wdb[...], preferred_element_type=F32)
            blocks = jnp.stack([y[:, s * LANES:(s + 1) * LANES] for s in range(s_rows)], axis=0)
            y_ref[rows_p, :] = jnp.swapaxes(blocks, 0, 1).reshape(pm * s_rows, LANES).astype(BF16)

    @pl.when(i >= nu)
    def _():
        y_ref[...] = jnp.zeros_like(y_ref)


def _moe_experts(xs, blk_expert, n_used, wg, wu, wd, layer, bm):
    _, n_exp, d, de = wg.shape
    s_rows = d // LANES
    nb = xs.shape[0] // (bm * s_rows)
    blk = pl.BlockSpec((bm * s_rows, LANES), lambda i, be, nu: (jnp.minimum(i, nu[0] - 1), 0))
    grid_spec = pltpu.PrefetchScalarGridSpec(
        num_scalar_prefetch=2,
        grid=(nb,),
        in_specs=[blk,
                  pl.BlockSpec((1, 1, d, de), lambda i, be, nu: (layer, be[i], 0, 0)),
                  pl.BlockSpec((1, 1, d, de), lambda i, be, nu: (layer, be[i], 0, 0)),
                  pl.BlockSpec((1, 1, de, d), lambda i, be, nu: (layer, be[i], 0, 0))],
        out_specs=pl.BlockSpec((bm * s_rows, LANES), lambda i, be, nu: (i, 0)),
        scratch_shapes=[pltpu.VMEM((d, de), BF16), pltpu.VMEM((d, de), BF16), pltpu.VMEM((de, d), BF16)]
                       + [pltpu.VMEM((bm // MOE_PARTS * s_rows, LANES), F32)] * MOE_PARTS)
    return pl.pallas_call(
        functools.partial(_moe_kernel, s_rows=s_rows),
        grid_spec=grid_spec,
        out_shape=jax.ShapeDtypeStruct(xs.shape, BF16),
        compiler_params=pltpu.CompilerParams(vmem_limit_bytes=VMEM_LIMIT),
    )(blk_expert, n_used, xs, wg, wu, wd)


def _combine_kernel(slot_hbm, y_hbm, gate_ref, h2_ref, x1_ref, sg_ref, su_ref, sd_ref, g2_ref, lng_ref, lnb_ref,
                    out_ref, idx_ref, ybuf, f_ref, frow_ref, isem, rsem, *, order, n_steps, s_rows):
    i = pl.program_id(0)
    tt, d = h2_ref.shape
    n_rows = TOP_K * tt

    def idx_copy(tile, buf):
        return pltpu.make_async_copy(slot_hbm.at[tile], idx_ref.at[pl.ds(buf * n_rows, n_rows)], isem.at[buf])

    group = 8
    assert tt % group == 0

    def issue_token(buf, r0, u):
        base = pl.multiple_of(r0 * s_rows, group * s_rows)
        ibase = buf * n_rows + r0
        for k in range(TOP_K):
            t = idx_ref[ibase + (k * tt + u)]
            pltpu.make_async_copy(y_hbm.at[pl.ds(pl.multiple_of(t * s_rows, s_rows), s_rows), :],
                                  ybuf.at[buf, pl.ds(base + (k * tt + u) * s_rows, s_rows), :],
                                  rsem.at[buf]).start()

    @pl.when(i == 0)
    def _():
        idx_copy(0, 0).start()
        idx_copy(0, 0).wait()

        def first(gi, carry):
            for u in range(group):
                issue_token(0, gi * group, u)
            return carry
        lax.fori_loop(0, tt // group, first, 0)
        if n_steps > 1:
            idx_copy(1, 1).start()

    buf = i % 2
    nxt = 1 - buf
    has_next = i + 1 < n_steps

    @pl.when(has_next)
    def _():
        idx_copy(i + 1, nxt).wait()

    hb = h2_ref[...].astype(BF16)
    a = (_silu(jnp.dot(hb, sg_ref[...], preferred_element_type=F32))
         * jnp.dot(hb, su_ref[...], preferred_element_type=F32))
    f_ref[...] = jnp.dot(a.astype(BF16), sd_ref[...], preferred_element_type=F32)

    def token_loop(cur, prefetch):
        pltpu.make_async_copy(y_hbm.at[pl.ds(0, n_rows * s_rows), :], ybuf.at[cur], rsem.at[cur]).wait()

        def body(gi, carry):
            r0 = gi * group
            base = pl.multiple_of(r0 * s_rows, group * s_rows)
            gbase = pl.multiple_of(r0 * TOP_K, group * TOP_K)
            for u in range(group):
                if prefetch:
                    issue_token(1 - cur, r0, u)
                gates = gate_ref[pl.ds(gbase + u * TOP_K, TOP_K), :]
                acc = jnp.zeros((s_rows, LANES), F32)
                for k in range(TOP_K):
                    rows = pl.ds(base + (k * tt + u) * s_rows, s_rows)
                    acc = acc + jnp.broadcast_to(gates[k:k + 1, :], (s_rows, LANES)) * ybuf[cur, rows, :].astype(F32)
                frow_ref[pl.ds(base + u * s_rows, s_rows), :] = acc
            return carry
        lax.fori_loop(0, tt // group, body, 0)

    for cur in (0, 1):
        @pl.when((buf == cur) & has_next)
        def _():
            token_loop(cur, True)

        @pl.when((buf == cur) & jnp.logical_not(has_next))
        def _():
            token_loop(cur, False)

    @pl.when(i + 2 < n_steps)
    def _():
        idx_copy(i + 2, buf).start()

    for s in range(s_rows):
        sl = slice(s * LANES, (s + 1) * LANES)
        f_ref[:, sl] = f_ref[:, sl] + _from_token_rows(frow_ref, s_rows, tt, s)
    g = order.group(i)
    r = DEEPNORM_ALPHA * x1_ref[...] + g2_ref[0, pl.ds(g, 1), :] * f_ref[...]
    out_ref[...] = _layer_norm(r, lng_ref[...], lnb_ref[...])


def _combine(slots, y_sorted, gate, h2, x1, sg, su, sd, mod, ln_g, ln_b, layer, order):
    n_tok, d = h2.shape
    tt = order.tile
    n_steps = order.n_tiles
    s_rows = d // LANES
    ds_ = sg.shape[1]
    row = pl.BlockSpec((1, d), lambda i: (0, 0))
    tok = pl.BlockSpec((tt, d), lambda i: (i, 0))
    return pl.pallas_call(
        functools.partial(_combine_kernel, order=order, n_steps=n_steps, s_rows=s_rows),
        grid=(n_steps,),
        in_specs=[pl.BlockSpec(memory_space=pl.ANY),
                  pl.BlockSpec(memory_space=pl.ANY),
                  pl.BlockSpec((tt * TOP_K, LANES), lambda i: (i, 0)),
                  tok, tok,
                  pl.BlockSpec((d, ds_), lambda i: (0, 0)),
                  pl.BlockSpec((d, ds_), lambda i: (0, 0)),
                  pl.BlockSpec((ds_, d), lambda i: (0, 0)),
                  _mod_spec(layer, MOD_G2, d, 1), row, row],
        out_specs=tok,
        out_shape=jax.ShapeDtypeStruct((n_tok, d), F32),
        scratch_shapes=[pltpu.SMEM((2 * TOP_K * tt,), jnp.int32),
                        pltpu.VMEM((2, TOP_K * tt * s_rows, LANES), BF16),
                        pltpu.VMEM((tt, d), F32),
                        pltpu.VMEM((tt * s_rows, LANES), F32),
                        pltpu.SemaphoreType.DMA((2,)),
                        pltpu.SemaphoreType.DMA((2,))],
        compiler_params=pltpu.CompilerParams(vmem_limit_bytes=VMEM_LIMIT, disable_bounds_checks=True),
    )(slots, y_sorted, gate, h2, x1, sg, su, sd, mod, ln_g.reshape(1, d), ln_b.reshape(1, d))


def _moe_layer(produced, layer, mod, order, router_w, router_b, wg, wu, wd, sg, su, sd, ln_g, ln_b):
    x1, h2, hp = produced
    n_tok, d = h2.shape
    n_exp = router_w.shape[1]
    bm = MOE_BLOCK
    tt = order.tile
    eidx_t, gate_t, rank_t, cnt = _router(h2, router_w, router_b)

    counts = cnt[:, 0].astype(jnp.int32)
    padded = (counts + bm - 1) // bm * bm
    pad_end = jnp.cumsum(padded).astype(jnp.int32)
    pad_start = pad_end - padded
    experts = jnp.arange(n_exp, dtype=jnp.int32)
    slot_t = rank_t + jnp.sum(jnp.where(eidx_t[:, :, None] == experts, pad_start, 0), axis=-1)
    n_slots = n_tok * TOP_K + n_exp * bm
    nb = n_slots // bm
    n_used = pad_end[-1:] // bm
    blk_expert = jnp.minimum(jnp.sum(pad_end[None, :] <= (jnp.arange(nb, dtype=jnp.int32) * bm)[:, None], axis=1),
                             n_exp - 1).astype(jnp.int32)
    n_tiles = n_tok // tt
    slots_tok = slot_t.T.reshape(n_tiles, tt * TOP_K)
    slots_k = slot_t.reshape(TOP_K, n_tiles, tt).transpose(1, 0, 2).reshape(n_tiles, TOP_K * tt)

    xs = _dispatch(hp, slots_tok, pad_end, padded, n_used, n_slots, bm)
    ys = _moe_experts(xs, blk_expert, n_used, wg, wu, wd, layer, bm)
    gates = jnp.broadcast_to(gate_t.T.reshape(n_tok * TOP_K, 1), (n_tok * TOP_K, LANES))
    return _combine(slots_k, ys, gates, h2, x1, sg[layer].astype(BF16), su[layer].astype(BF16),
                    sd[layer].astype(BF16), mod, ln_g[layer], ln_b[layer], layer, order)


def kernel(x, c, ctx, c_ctx, ada_w, ada_b, ln_mix_g, ln_mix_b, ln_ffn_g, ln_ffn_b, ret_w_in, ret_w_out,
           ret_decay_fwd, ret_decay_bwd, pool_w, pool_scale, router_w, router_b, exp_w_gate, exp_w_up,
           exp_w_down, sh_w_gate, sh_w_up, sh_w_down):
    batch, seq, d = x.shape
    ctx_len = ctx.shape[1]
    heads = RET_HEADS
    qk_dim = d
    head_dim = qk_dim // heads
    v_dim = ret_w_out.shape[1]
    assert ada_w.shape[0] == DEPTH == 2 and batch + 1 <= MOD_ROWS

    cvec = jnp.zeros((MOD_ROWS, d), F32).at[:batch].set(c.astype(F32)).at[batch].set(c_ctx.astype(F32))
    mod = _modulation(cvec, ada_w.astype(F32), ada_b.astype(F32))

    x2d = x.reshape(batch * seq, d)
    ctx2d = ctx.reshape(batch * ctx_len, d)
    order0 = _Order(batch, seq, ctx_len, TOKEN_TILE)

    w_in = ret_w_in[0].astype(F32)
    tabs = _rope_tables(seq, head_dim, TOKEN_TILE)
    qk = _inproj(x2d, ctx2d, mod, w_in, 0, 2 * qk_dim, order0, BF16, rope_tabs=tabs, k_col0=qk_dim,
                 head_dim=head_dim)
    v = _inproj(x2d, ctx2d, mod, w_in, 2 * qk_dim, v_dim, order0, BF16)
    g = _inproj(x2d, ctx2d, mod, w_in, 2 * qk_dim + v_dim, v_dim, order0, F32)
    og = _retention(qk, v, g, ret_decay_fwd[0], ret_decay_bwd[0], batch, ctx_len + seq, ctx_len)
    produced = _outproj(og, x2d, ctx2d, ret_w_out[0].astype(BF16), mod, ln_mix_g[0], ln_mix_b[0], order0)
    xa = _moe_layer(produced, 0, mod, order0, router_w[0], router_b[0], exp_w_gate, exp_w_up, exp_w_down,
                    sh_w_gate, sh_w_up, sh_w_down, ln_ffn_g, ln_ffn_b)

    order1 = _Order(batch, seq, 0, TOKEN_TILE)
    produced = _pool_layer(xa, order0, pool_w[0], pool_scale[0], mod, ln_mix_g[1], ln_mix_b[1], 1, batch, seq)
    xb = _moe_layer(produced, 1, mod, order1, router_w[1], router_b[1], exp_w_gate, exp_w_up, exp_w_down,
                    sh_w_gate, sh_w_up, sh_w_down, ln_ffn_g, ln_ffn_b)
    return xb.reshape(batch, seq, d)
```

```python
import functools

import jax
import jax.numpy as jnp
from jax import lax
from jax.experimental import pallas as pl
from jax.experimental.pallas import tpu as pltpu

F32 = jnp.float32
BF16 = jnp.bfloat16
U32 = jnp.uint32
HIGHEST = lax.Precision.HIGHEST

DEPTH = 2
GRID_W = 64
RET_HEADS = 8
ROPE_BASE = 10000.0
POOL_WINDOWS = (2, 4, 8, 16)
N_EXPERT_GROUPS = 8
TOPK_GROUPS = 4
TOP_K = 8
ROUTED_SCALE = 2.5
LN_EPS = 1e-5
N_MOD = 6
DEEPNORM_ALPHA = (2.0 * DEPTH) ** 0.25
MOD_SH1, MOD_SC1, MOD_G1, MOD_SH2, MOD_SC2, MOD_G2 = range(N_MOD)

LANES = 128
TOKEN_TILE = 256
RET_BLOCK = 2816
RET_CHUNK = 256
ROUTER_TILE = 512
MOE_BLOCK = 512
MOE_PARTS = 2
MOD_ROWS = 8
VMEM_LIMIT = 56 * 1024 * 1024


def _silu(x):
    return x * jax.nn.sigmoid(x)


def _layer_norm(r, g, b):
    mu = jnp.mean(r, axis=-1, keepdims=True)
    var = jnp.mean(jnp.square(r - mu), axis=-1, keepdims=True)
    return (r - mu) * lax.rsqrt(var + LN_EPS) * g + b


def _to_token_rows(h):
    tm, d = h.shape
    s_rows = d // LANES
    blocks = jnp.stack([h[:, s * LANES:(s + 1) * LANES] for s in range(s_rows)], axis=0)
    return jnp.swapaxes(blocks, 0, 1).reshape(tm * s_rows, LANES)


def _from_token_rows(rows, s_rows):
    return jnp.swapaxes(rows.reshape(rows.shape[0] // s_rows, s_rows, LANES), 0, 1)


class _Order:
    def __init__(self, batch, seq, ctx_len, tile):
        assert seq % tile == 0 and ctx_len % tile == 0
        self.batch, self.seq, self.ctx_len, self.tile = batch, seq, ctx_len, tile
        self.ctx_tiles = ctx_len // tile
        self.lat_tiles = seq // tile
        self.per_batch = self.ctx_tiles + self.lat_tiles
        self.n_tiles = batch * self.per_batch
        self.n_tok = self.n_tiles * tile

    def split(self, i):
        b = i // self.per_batch
        r = i % self.per_batch
        return b, r, r < self.ctx_tiles

    def group(self, i):
        b, _, is_ctx = self.split(i)
        return jnp.where(is_ctx, self.batch, b)

    def lat_block(self, i):
        b, r, _ = self.split(i)
        return b * self.lat_tiles + jnp.maximum(r - self.ctx_tiles, 0)

    def ctx_block(self, i):
        b, r, _ = self.split(i)
        return b * self.ctx_tiles + jnp.minimum(r, max(self.ctx_tiles - 1, 0))


def _mod_kernel(c_ref, w_ref, b_ref, o_ref):
    s = _silu(c_ref[...])
    o_ref[0] = jnp.dot(s, w_ref[0], precision=HIGHEST, preferred_element_type=F32) + b_ref[0]


def _modulation(cvec, ada_w, ada_b):
    depth, d, n = ada_w.shape
    tn = 2048 if n % 2048 == 0 else 1024
    return pl.pallas_call(
        _mod_kernel,
        grid=(depth, n // tn),
        in_specs=[pl.BlockSpec((MOD_ROWS, d), lambda l, j: (0, 0)),
                  pl.BlockSpec((1, d, tn), lambda l, j: (l, 0, j)),
                  pl.BlockSpec((1, 1, tn), lambda l, j: (l, 0, j))],
        out_specs=pl.BlockSpec((1, MOD_ROWS, tn), lambda l, j: (l, 0, j)),
        out_shape=jax.ShapeDtypeStruct((depth, MOD_ROWS, n), F32),
        compiler_params=pltpu.CompilerParams(vmem_limit_bytes=VMEM_LIMIT),
    )(cvec, ada_w, ada_b.reshape(depth, 1, n))


def _mod_spec(layer, chunk, d, n_grid_args):
    if n_grid_args == 1:
        return pl.BlockSpec((1, MOD_ROWS, d), lambda i: (layer, 0, chunk))
    return pl.BlockSpec((1, MOD_ROWS, d), lambda j, i: (layer, 0, chunk))


def _inproj_kernel(x_ref, c_ref, sc_ref, sh_ref, w_ref, *rest, order, rope, k_col0, head_dim):
    i = pl.program_id(1)
    j = pl.program_id(0)
    if rope:
        cos_ref, sin_ref, o_ref, wb_ref = rest
    else:
        o_ref, wb_ref = rest

    @pl.when(i == 0)
    def _():
        wb_ref[...] = w_ref[...].astype(BF16)

    g = order.group(i)
    _, _, is_ctx = order.split(i)
    sc = sc_ref[0, pl.ds(g, 1), :]
    sh = sh_ref[0, pl.ds(g, 1), :]
    xin = jnp.where(is_ctx, c_ref[...], x_ref[...])
    a = (xin * (1.0 + sc) + sh).astype(BF16)
    acc = jnp.dot(a, wb_ref[...], preferred_element_type=F32)
    if not rope:
        o_ref[...] = acc.astype(o_ref.dtype)
        return
    tn = acc.shape[1]
    lane = lax.broadcasted_iota(jnp.int32, acc.shape, 1)
    partner = jnp.where((lane & 1) == 0, pltpu.roll(acc, tn - 1, 1), pltpu.roll(acc, 1, 1))
    cos = cos_ref[...]
    sin = sin_ref[...]
    scale = jnp.where(j * tn >= k_col0, head_dim ** -0.5, 1.0).astype(F32)
    for h in range(tn // head_dim):
        sl = slice(h * head_dim, (h + 1) * head_dim)
        o_ref[:, sl] = ((acc[:, sl] * cos + partner[:, sl] * sin) * scale).astype(o_ref.dtype)


def _inproj(x2d, ctx2d, mod, w_in, col0, n, order, out_dtype, rope_tabs=None, k_col0=0, head_dim=0):
    d = w_in.shape[0]
    tm = order.tile
    tn = min(n, 2048)
    assert col0 % tn == 0 and n % tn == 0
    rope = rope_tabs is not None
    in_specs = [pl.BlockSpec((tm, d), lambda j, i: (order.lat_block(i), 0)),
                pl.BlockSpec((tm, d), lambda j, i: (order.ctx_block(i), 0)),
                _mod_spec(0, MOD_SC1, d, 2), _mod_spec(0, MOD_SH1, d, 2),
                pl.BlockSpec((d, tn), lambda j, i: (0, col0 // tn + j), pipeline_mode=pl.Buffered(1))]
    args = [x2d, ctx2d, mod, mod, w_in]
    if rope:
        def tab_map(j, i):
            _, r, is_ctx = order.split(i)
            return (jnp.where(is_ctx, order.lat_tiles, r - order.ctx_tiles), 0)
        in_specs += [pl.BlockSpec((tm, head_dim), tab_map), pl.BlockSpec((tm, head_dim), tab_map)]
        args += list(rope_tabs)
    return pl.pallas_call(
        functools.partial(_inproj_kernel, order=order, rope=rope, k_col0=k_col0, head_dim=head_dim),
        grid=(n // tn, order.n_tiles),
        in_specs=in_specs,
        out_specs=pl.BlockSpec((tm, tn), lambda j, i: (i, j)),
        out_shape=jax.ShapeDtypeStruct((order.n_tok, n), out_dtype),
        scratch_shapes=[pltpu.VMEM((d, tn), BF16)],
        compiler_params=pltpu.CompilerParams(vmem_limit_bytes=VMEM_LIMIT),
    )(*args)


def _rope_tables(seq, head_dim, tile):
    n_freq = head_dim // 4
    pos = jnp.arange(seq, dtype=jnp.int32)
    row = (pos // GRID_W).astype(F32)
    col = (pos % GRID_W).astype(F32)
    freqs = ROPE_BASE ** (-jnp.arange(n_freq, dtype=F32) / n_freq)
    ang = jnp.concatenate([row[:, None] * freqs, col[:, None] * freqs], axis=-1)
    cos = jnp.repeat(jnp.cos(ang), 2, axis=-1)
    sin = jnp.stack([-jnp.sin(ang), jnp.sin(ang)], axis=-1).reshape(seq, head_dim)
    ident = jnp.zeros((tile, head_dim), F32)
    return jnp.concatenate([cos, ident + 1.0], axis=0), jnp.concatenate([sin, ident], axis=0)


def _ret_kernel(q_ref, k_ref, v_ref, g_ref, df_ref, db_ref, o_ref, sf_ref, sb_ref, obuf_ref, *,
                n_blocks, chunks_per_block, ctx_chunks):
    C = RET_CHUNK
    NB, CPB, NCC = n_blocks, chunks_per_block, ctx_chunks
    st = pl.program_id(2)
    phase = st // (NB + 1)
    s = st % (NB + 1)
    ldf = jax.nn.log_sigmoid(df_ref[0])[:, :1]
    ldb = jax.nn.log_sigmoid(db_ref[0])[:, :1]
    pos = lax.broadcasted_iota(jnp.int32, (C, 1), 0).astype(F32)
    tdims = (((0,), (0,)), ((), ()))

    @pl.when(phase == 0)
    def _backward_states():
        @pl.when(s == 0)
        def _():
            sb_ref[...] = jnp.zeros_like(sb_ref)

        q_dec = jnp.exp(ldb * (C - pos))
        k_dec = jnp.exp(ldb * pos)
        chunk_dec = jnp.exp(ldb * C)

        def chunk(blk, c):
            rows = pl.ds(c * C, C)
            q = q_ref[rows, :].astype(F32)
            k = k_ref[rows, :].astype(F32)
            v = v_ref[rows, :]
            state = sb_ref[...]
            goff = pl.multiple_of((blk * CPB + c) * C, C)
            obuf_ref[pl.ds(goff, C), :] = jnp.dot((q * q_dec).astype(BF16), state.astype(BF16),
                                                  preferred_element_type=F32)
            sb_ref[...] = state * chunk_dec + lax.dot_general((k * k_dec).astype(BF16), v, tdims,
                                                              preferred_element_type=F32)

        @pl.when(s == 0)
        def _():
            for c in reversed(range(NCC)):
                chunk(0, c)

        @pl.when((s > 0) & (s < NB))
        def _():
            for c in reversed(range(CPB)):
                chunk(NB - s, c)

        @pl.when(s == NB)
        def _():
            for c in reversed(range(NCC, CPB)):
                chunk(0, c)

    @pl.when((phase == 1) & (s < NB))
    def _forward_and_output():
        @pl.when(s == 0)
        def _():
            sf_ref[...] = jnp.zeros_like(sf_ref)

        q_dec = jnp.exp(ldf * (pos + 1.0))
        k_dec = jnp.exp(ldf * (C - 1.0 - pos))
        chunk_dec = jnp.exp(ldf * C)
        ii = lax.broadcasted_iota(jnp.int32, (C, C), 0)
        jj = lax.broadcasted_iota(jnp.int32, (C, C), 1)
        rel = (ii - jj).astype(F32)
        decay = jnp.where(rel >= 0, jnp.exp(ldf * jnp.maximum(rel, 0.0)), jnp.exp(ldb * jnp.maximum(-rel, 0.0)))

        for c in range(CPB):
            rows = pl.ds(c * C, C)
            qb = q_ref[rows, :]
            kb = k_ref[rows, :]
            v = v_ref[rows, :]
            q = qb.astype(F32)
            k = kb.astype(F32)
            state = sf_ref[...]
            scores = lax.dot_general(qb, kb, (((1,), (1,)), ((), ())), preferred_element_type=F32) * decay
            goff = pl.multiple_of((s * CPB + c) * C, C)
            o = (jnp.dot(scores.astype(BF16), v, preferred_element_type=F32)
                 + jnp.dot((q * q_dec).astype(BF16), state.astype(BF16), preferred_element_type=F32)
                 + obuf_ref[pl.ds(goff, C), :])
            sf_ref[...] = state * chunk_dec + lax.dot_general((k * k_dec).astype(BF16), v, tdims,
                                                              preferred_element_type=F32)
            mu = jnp.mean(o, axis=-1, keepdims=True)
            var = jnp.mean(jnp.square(o - mu), axis=-1, keepdims=True)
            on = (o - mu) * lax.rsqrt(var + LN_EPS)
            o_ref[rows, :] = (_silu(g_ref[rows, :]) * on).astype(o_ref.dtype)


def _retention(qk, v, g, raw_fwd, raw_bwd, batch, tokens_per_batch, ctx_len):
    n_tok, qk2 = qk.shape
    v_dim = v.shape[1]
    heads = RET_HEADS
    dk = qk2 // 2 // heads
    dv = v_dim // heads
    tb = next(t for t in range(RET_BLOCK, 0, -RET_CHUNK) if tokens_per_batch % t == 0)
    assert tb % RET_CHUNK == 0 and ctx_len % RET_CHUNK == 0 and ctx_len <= tb
    nb = tokens_per_batch // tb
    steps = 2 * (nb + 1)

    def blk(st):
        phase = st // (nb + 1)
        s = st % (nb + 1)
        back = jnp.where(s == 0, 0, nb - s)
        return jnp.where(phase == 0, back, jnp.minimum(s, nb - 1))

    def fwd_blk(st):
        phase = st // (nb + 1)
        s = st % (nb + 1)
        return jnp.where(phase == 0, 0, jnp.minimum(s, nb - 1))

    dec = lambda raw: jnp.broadcast_to(raw.astype(F32)[:, None, None], (heads, 1, LANES))
    return pl.pallas_call(
        functools.partial(_ret_kernel, n_blocks=nb, chunks_per_block=tb // RET_CHUNK,
                          ctx_chunks=ctx_len // RET_CHUNK),
        grid=(batch, heads, steps),
        in_specs=[pl.BlockSpec((tb, dk), lambda b, h, st: (b * nb + blk(st), h)),
                  pl.BlockSpec((tb, dk), lambda b, h, st: (b * nb + blk(st), heads + h)),
                  pl.BlockSpec((tb, dv), lambda b, h, st: (b * nb + blk(st), h)),
                  pl.BlockSpec((tb, dv), lambda b, h, st: (b * nb + fwd_blk(st), h)),
                  pl.BlockSpec((1, 1, LANES), lambda b, h, st: (h, 0, 0)),
                  pl.BlockSpec((1, 1, LANES), lambda b, h, st: (h, 0, 0))],
        out_specs=pl.BlockSpec((tb, dv), lambda b, h, st: (b * nb + fwd_blk(st), h)),
        out_shape=jax.ShapeDtypeStruct((n_tok, v_dim), BF16),
        scratch_shapes=[pltpu.VMEM((dk, dv), F32), pltpu.VMEM((dk, dv), F32),
                        pltpu.VMEM((tokens_per_batch, dv), F32)],
        compiler_params=pltpu.CompilerParams(vmem_limit_bytes=VMEM_LIMIT),
    )(qk, qk, v, g, dec(raw_fwd), dec(raw_bwd))


def _outproj_kernel(o_ref, x_ref, c_ref, w_ref, g1_ref, sc2_ref, sh2_ref, lng_ref, lnb_ref,
                    x1_ref, h2_ref, hp_ref, *, order):
    i = pl.program_id(0)
    g = order.group(i)
    _, _, is_ctx = order.split(i)
    y = jnp.dot(o_ref[...], w_ref[...], preferred_element_type=F32)
    xin = jnp.where(is_ctx, c_ref[...], x_ref[...])
    r = DEEPNORM_ALPHA * xin + g1_ref[0, pl.ds(g, 1), :] * y
    x1 = _layer_norm(r, lng_ref[...], lnb_ref[...])
    x1_ref[...] = x1
    h2 = x1 * (1.0 + sc2_ref[0, pl.ds(g, 1), :]) + sh2_ref[0, pl.ds(g, 1), :]
    h2_ref[...] = h2
    hp_ref[...] = _to_token_rows(h2).astype(BF16)


def _outproj(og, x2d, ctx2d, w_out, mod, ln_g, ln_b, order):
    v_dim, d = w_out.shape
    tm = order.tile
    s_rows = d // LANES
    row = pl.BlockSpec((1, d), lambda i: (0, 0))
    tok = pl.BlockSpec((tm, d), lambda i: (i, 0))
    return pl.pallas_call(
        functools.partial(_outproj_kernel, order=order),
        grid=(order.n_tiles,),
        in_specs=[pl.BlockSpec((tm, v_dim), lambda i: (i, 0)),
                  pl.BlockSpec((tm, d), lambda i: (order.lat_block(i), 0)),
                  pl.BlockSpec((tm, d), lambda i: (order.ctx_block(i), 0)),
                  pl.BlockSpec((v_dim, d), lambda i: (0, 0), pipeline_mode=pl.Buffered(1)),
                  _mod_spec(0, MOD_G1, d, 1), _mod_spec(0, MOD_SC2, d, 1), _mod_spec(0, MOD_SH2, d, 1),
                  row, row],
        out_specs=[tok, tok, pl.BlockSpec((tm * s_rows, LANES), lambda i: (i, 0))],
        out_shape=[jax.ShapeDtypeStruct((order.n_tok, d), F32)] * 2
                  + [jax.ShapeDtypeStruct((order.n_tok * s_rows, LANES), BF16)],
        compiler_params=pltpu.CompilerParams(vmem_limit_bytes=VMEM_LIMIT),
    )(og, x2d, ctx2d, w_out, mod, mod, mod, ln_g.reshape(1, d), ln_b.reshape(1, d))


def _pool_kernel(x_ref, pw_ref, ps_ref, sc1_ref, sh1_ref, g1_ref, sc2_ref, sh2_ref, lng_ref, lnb_ref,
                 x1_ref, h2_ref, hp_ref, y_ref, *, lat_tiles):
    i = pl.program_id(0)
    g = i // lat_tiles
    x = x_ref[...]
    tm, d = x.shape
    gd = d // len(POOL_WINDOWS)
    h = x * (1.0 + sc1_ref[0, pl.ds(g, 1), :]) + sh1_ref[0, pl.ds(g, 1), :]
    ni = lax.broadcasted_iota(jnp.int32, (tm, tm), 0)
    mi = lax.broadcasted_iota(jnp.int32, (tm, tm), 1)
    same_row = (ni // GRID_W) == (mi // GRID_W)
    delta = mi - ni
    col = lax.broadcasted_iota(jnp.int32, (tm, 1), 0) % GRID_W
    for gi, w in enumerate(POOL_WINDOWS):
        window = jnp.where(same_row & (delta >= -(w // 2)) & (delta < w // 2), 1.0, 0.0).astype(BF16)
        count = (jnp.minimum(col + w // 2, GRID_W) - jnp.maximum(col - w // 2, 0)).astype(F32)
        hg = h[:, gi * gd:(gi + 1) * gd]
        h_hi = hg.astype(BF16)
        r1 = hg - h_hi.astype(F32)
        h_mid = r1.astype(BF16)
        h_lo = (r1 - h_mid.astype(F32)).astype(BF16)
        total = (jnp.dot(window, h_hi, preferred_element_type=F32)
                 + jnp.dot(window, h_mid, preferred_element_type=F32)
                 + jnp.dot(window, h_lo, preferred_element_type=F32))
        p = total / count - hg
        y_ref[:, gi * gd:(gi + 1) * gd] = jnp.dot(p.astype(BF16), pw_ref[gi], preferred_element_type=F32)
    y = y_ref[...] * ps_ref[...]
    r = DEEPNORM_ALPHA * x + g1_ref[0, pl.ds(g, 1), :] * y
    x1 = _layer_norm(r, lng_ref[...], lnb_ref[...])
    x1_ref[...] = x1
    h2 = x1 * (1.0 + sc2_ref[0, pl.ds(g, 1), :]) + sh2_ref[0, pl.ds(g, 1), :]
    h2_ref[...] = h2
    hp_ref[...] = _to_token_rows(h2).astype(BF16)


def _pool_layer(x_prev, order0, pool_w, pool_scale, mod, ln_g, ln_b, layer, batch, seq):
    d = x_prev.shape[1]
    tm = order0.tile
    s_rows = d // LANES
    lat_tiles = seq // tm
    n_tiles = batch * lat_tiles
    ng, gd, _ = pool_w.shape

    def src(i):
        b = i // lat_tiles
        return (b * order0.per_batch + order0.ctx_tiles + i % lat_tiles, 0)

    row = pl.BlockSpec((1, d), lambda i: (0, 0))
    tok = pl.BlockSpec((tm, d), lambda i: (i, 0))
    ms = lambda chunk: _mod_spec(layer, chunk, d, 1)
    return pl.pallas_call(
        functools.partial(_pool_kernel, lat_tiles=lat_tiles),
        grid=(n_tiles,),
        in_specs=[pl.BlockSpec((tm, d), src),
                  pl.BlockSpec((ng, gd, gd), lambda i: (0, 0, 0)),
                  row, ms(MOD_SC1), ms(MOD_SH1), ms(MOD_G1), ms(MOD_SC2), ms(MOD_SH2), row, row],
        out_specs=[tok, tok, pl.BlockSpec((tm * s_rows, LANES), lambda i: (i, 0))],
        out_shape=[jax.ShapeDtypeStruct((n_tiles * tm, d), F32)] * 2
                  + [jax.ShapeDtypeStruct((n_tiles * tm * s_rows, LANES), BF16)],
        scratch_shapes=[pltpu.VMEM((tm, d), F32)],
        compiler_params=pltpu.CompilerParams(vmem_limit_bytes=VMEM_LIMIT),
    )(x_prev, pool_w.astype(BF16), pool_scale.reshape(1, d), mod, mod, mod, mod, mod,
      ln_g.reshape(1, d), ln_b.reshape(1, d))


def _route_tile(h, w_ref, b_ref, eidx_ref, gate_ref, rank_ref, cnt_ref, carry_ref):
    i = pl.program_id(0)
    n_exp = w_ref.shape[0]
    tr = h.shape[0]
    per = n_exp // N_EXPERT_GROUPS
    neg = -jnp.inf

    @pl.when(i == 0)
    def _():
        carry_ref[...] = jnp.zeros_like(carry_ref)

    logits = lax.dot_general(w_ref[...], h, (((1,), (1,)), ((), ())),
                             precision=HIGHEST, preferred_element_type=F32)
    scores = jax.nn.sigmoid(logits)
    biased = scores + b_ref[...]

    sub_idx = lax.broadcasted_iota(jnp.int32, (per, tr), 0)
    group_score = []
    for gi in range(N_EXPERT_GROUPS):
        sub = biased[gi * per:(gi + 1) * per, :]
        m1 = jnp.max(sub, axis=0, keepdims=True)
        first = jnp.min(jnp.where(sub == m1, sub_idx, per), axis=0, keepdims=True)
        m2 = jnp.max(jnp.where(sub_idx == first, neg, sub), axis=0, keepdims=True)
        group_score.append(m1 + m2)
    masked = []
    for gi in range(N_EXPERT_GROUPS):
        ahead = jnp.zeros((1, tr), jnp.int32)
        for gj in range(N_EXPERT_GROUPS):
            if gj == gi:
                continue
            wins = (group_score[gj] >= group_score[gi]) if gj < gi else (group_score[gj] > group_score[gi])
            ahead = ahead + wins.astype(jnp.int32)
        keep = ahead < TOPK_GROUPS
        masked.append(jnp.where(keep, biased[gi * per:(gi + 1) * per, :], neg))
    masked = jnp.concatenate(masked, axis=0)

    eid = lax.broadcasted_iota(jnp.int32, (n_exp, tr), 0)
    picks, gates = [], []
    onehot = jnp.zeros((n_exp, tr), F32)
    for _ in range(TOP_K):
        m = jnp.max(masked, axis=0, keepdims=True)
        first = jnp.min(jnp.where(masked == m, eid, n_exp), axis=0, keepdims=True)
        sel = eid == first
        gates.append(jnp.sum(jnp.where(sel, scores, 0.0), axis=0, keepdims=True))
        picks.append(first)
        masked = jnp.where(sel, neg, masked)
        onehot = onehot + sel.astype(F32)
    gate_sum = gates[0]
    for gk in gates[1:]:
        gate_sum = gate_sum + gk

    si = lax.broadcasted_iota(jnp.int32, (tr, tr), 0)
    ti = lax.broadcasted_iota(jnp.int32, (tr, tr), 1)
    before = jnp.where(si < ti, 1.0, 0.0).astype(BF16)
    prefix = jnp.dot(onehot.astype(BF16), before, preferred_element_type=F32) + carry_ref[:, :1]
    for k in range(TOP_K):
        eidx_ref[k:k + 1, :] = picks[k]
        gate_ref[k:k + 1, :] = gates[k] / gate_sum * ROUTED_SCALE
        rank_ref[k:k + 1, :] = jnp.sum(jnp.where(eid == picks[k], prefix, 0.0), axis=0,
                                       keepdims=True).astype(jnp.int32)
    carry_ref[...] = carry_ref[...] + jnp.sum(onehot, axis=1, keepdims=True)
    cnt_ref[...] = carry_ref[...]


def _router_kernel(h_ref, w_ref, b_ref, eidx_ref, gate_ref, rank_ref, cnt_ref, carry_ref):
    _route_tile(h_ref[...], w_ref, b_ref, eidx_ref, gate_ref, rank_ref, cnt_ref, carry_ref)


def _router(h2, router_w, router_b):
    n_tok, d = h2.shape
    n_exp = router_w.shape[1]
    tr = ROUTER_TILE
    assert n_tok % tr == 0
    sel = pl.BlockSpec((TOP_K, tr), lambda i: (0, i))
    return pl.pallas_call(
        _router_kernel,
        grid=(n_tok // tr,),
        in_specs=[pl.BlockSpec((tr, d), lambda i: (i, 0)),
                  pl.BlockSpec((n_exp, d), lambda i: (0, 0)),
                  pl.BlockSpec((n_exp, 1), lambda i: (0, 0))],
        out_specs=[sel, sel, sel, pl.BlockSpec((n_exp, LANES), lambda i: (0, 0))],
        out_shape=[jax.ShapeDtypeStruct((TOP_K, n_tok), jnp.int32),
                   jax.ShapeDtypeStruct((TOP_K, n_tok), F32),
                   jax.ShapeDtypeStruct((TOP_K, n_tok), jnp.int32),
                   jax.ShapeDtypeStruct((n_exp, LANES), F32)],
        scratch_shapes=[pltpu.VMEM((n_exp, LANES), F32)],
        compiler_params=pltpu.CompilerParams(vmem_limit_bytes=VMEM_LIMIT),
    )(h2, router_w.astype(F32).T, router_b.astype(F32).reshape(n_exp, 1))


def _dispatch_kernel(pe_ref, pd_ref, nu_ref, slot_hbm, hp_ref, xs_hbm, idx_ref, zero_ref, isem, zsem, rsem, *,
                     n_steps, n_exp, n_blocks, s_rows, bm):
    i = pl.program_id(0)
    tt = hp_ref.shape[0] // s_rows
    zrows = bm * s_rows
    per_tile = tt * TOP_K

    def idx_copy(tile, buf):
        return pltpu.make_async_copy(slot_hbm.at[tile], idx_ref.at[pl.ds(buf * per_tile, per_tile)], isem.at[buf])

    def zero_copy(row0):
        return pltpu.make_async_copy(zero_ref, xs_hbm.at[pl.ds(pl.multiple_of(row0, s_rows), zrows), :], zsem)

    @pl.when(i == 0)
    def _():
        zero_ref[...] = jnp.zeros_like(zero_ref)
        nu = nu_ref[0]

        def fill(fn):
            def per_expert(e, carry):
                @pl.when(pd_ref[e] > 0)
                def _():
                    fn(zero_copy((pe_ref[e] - bm) * s_rows))
                return carry
            lax.fori_loop(0, n_exp, per_expert, 0)

            def per_tail(b, carry):
                fn(zero_copy(b * zrows))
                return carry
            lax.fori_loop(nu, n_blocks, per_tail, 0)

        fill(lambda cp: cp.start())
        fill(lambda cp: cp.wait())
        idx_copy(0, 0).start()
        idx_copy(0, 0).wait()
        if n_steps > 1:
            idx_copy(1, 1).start()

    buf = i % 2

    @pl.when(i > 0)
    def _():
        idx_copy(i, buf).wait()

    group = 8
    assert tt % group == 0

    def body(gi, carry):
        base = pl.multiple_of(gi * (group * s_rows), group * s_rows)
        ibase = buf * per_tile + gi * (group * TOP_K)
        for u in range(group):
            src = hp_ref.at[pl.ds(base + u * s_rows, s_rows), :]
            for k in range(TOP_K):
                t = idx_ref[ibase + (u * TOP_K + k)]
                pltpu.make_async_copy(src, xs_hbm.at[pl.ds(pl.multiple_of(t * s_rows, s_rows), s_rows), :],
                                      rsem).start()
        return carry

    lax.fori_loop(0, tt // group, body, 0)

    @pl.when((i > 0) & (i + 1 < n_steps))
    def _():
        idx_copy(i + 1, 1 - buf).start()

    for _ in range(TOP_K):
        pltpu.make_async_copy(hp_ref, xs_hbm.at[pl.ds(0, tt * s_rows), :], rsem).wait()


def _dispatch(hp, slots, pad_end, padded, n_used, n_slots, bm):
    rows, _ = hp.shape
    n_steps, per_tile = slots.shape
    tt = per_tile // TOP_K
    s_rows = rows // (n_steps * tt)
    n_exp = pad_end.shape[0]
    grid_spec = pltpu.PrefetchScalarGridSpec(
        num_scalar_prefetch=3,
        grid=(n_steps,),
        in_specs=[pl.BlockSpec(memory_space=pl.ANY),
                  pl.BlockSpec((tt * s_rows, LANES), lambda i, pe, pd, nu: (i, 0))],
        out_specs=pl.BlockSpec(memory_space=pl.ANY),
        scratch_shapes=[pltpu.SMEM((2 * per_tile,), jnp.int32),
                        pltpu.VMEM((bm * s_rows, LANES), BF16),
                        pltpu.SemaphoreType.DMA((2,)),
                        pltpu.SemaphoreType.DMA(()),
                        pltpu.SemaphoreType.DMA(())])
    return pl.pallas_call(
        functools.partial(_dispatch_kernel, n_steps=n_steps, n_exp=n_exp, n_blocks=n_slots // bm,
                          s_rows=s_rows, bm=bm),
        grid_spec=grid_spec,
        out_shape=jax.ShapeDtypeStruct((n_slots * s_rows, LANES), BF16),
        compiler_params=pltpu.CompilerParams(vmem_limit_bytes=VMEM_LIMIT, disable_bounds_checks=True),
    )(pad_end, padded, n_used, slots, hp)


def _moe_kernel(be_ref, nu_ref, x_ref, wg_ref, wu_ref, wd_ref, y_ref, wgb, wub, wdb, *, s_rows):
    i = pl.program_id(0)
    nu = nu_ref[0]
    parts = MOE_PARTS
    pm = x_ref.shape[0] // s_rows // parts

    @pl.when(i < nu)
    def _():
        @pl.when((i == 0) | (be_ref[i] != be_ref[jnp.maximum(i - 1, 0)]))
        def _():
            wgb[...] = wg_ref[0, 0].astype(BF16)
            wub[...] = wu_ref[0, 0].astype(BF16)
            wdb[...] = wd_ref[0, 0].astype(BF16)

        rows_of = [slice(p * pm * s_rows, (p + 1) * pm * s_rows) for p in range(parts)]
        hidden = []
        for p in range(parts):
            xt = _from_token_rows(x_ref[rows_of[p], :].astype(F32), s_rows)
            gate = up = None
            for c in range(s_rows // 2):
                xc = jnp.concatenate([xt[2 * c], xt[2 * c + 1]], axis=1).astype(BF16)
                rows = slice(2 * c * LANES, (2 * c + 2) * LANES)
                gc = jnp.dot(xc, wgb[rows, :], preferred_element_type=F32)
                uc = jnp.dot(xc, wub[rows, :], preferred_element_type=F32)
                gate = gc if gate is None else gate + gc
                up = uc if up is None else up + uc
            hidden.append((gate, up))
        outs = []
        for gate, up in hidden:
            a = (_silu(gate) * up).astype(BF16)
            outs.append(jnp.dot(a, wdb[...], preferred_element_type=F32))
        for p, y in enumerate(outs):
            y_ref[rows_of[p], :] = _to_token_rows(y).astype(BF16)

    @pl.when(i >= nu)
    def _():
        y_ref[...] = jnp.zeros_like(y_ref)


def _moe_experts(xs, blk_expert, n_used, wg, wu, wd, layer, bm):
    _, n_exp, d, de = wg.shape
    s_rows = d // LANES
    nb = xs.shape[0] // (bm * s_rows)
    blk = pl.BlockSpec((bm * s_rows, LANES), lambda i, be, nu: (jnp.minimum(i, nu[0] - 1), 0))
    grid_spec = pltpu.PrefetchScalarGridSpec(
        num_scalar_prefetch=2,
        grid=(nb,),
        in_specs=[blk,
                  pl.BlockSpec((1, 1, d, de), lambda i, be, nu: (layer, be[i], 0, 0)),
                  pl.BlockSpec((1, 1, d, de), lambda i, be, nu: (layer, be[i], 0, 0)),
                  pl.BlockSpec((1, 1, de, d), lambda i, be, nu: (layer, be[i], 0, 0))],
        out_specs=pl.BlockSpec((bm * s_rows, LANES), lambda i, be, nu: (i, 0)),
        scratch_shapes=[pltpu.VMEM((d, de), BF16), pltpu.VMEM((d, de), BF16), pltpu.VMEM((de, d), BF16)])
    return pl.pallas_call(
        functools.partial(_moe_kernel, s_rows=s_rows),
        grid_spec=grid_spec,
        out_shape=jax.ShapeDtypeStruct(xs.shape, BF16),
        compiler_params=pltpu.CompilerParams(vmem_limit_bytes=VMEM_LIMIT),
    )(blk_expert, n_used, xs, wg, wu, wd)


def _combine_kernel(slot_hbm, y_hbm, gate_ref, h2_ref, x1_ref, sg_ref, su_ref, sd_ref, g2_ref, lng_ref, lnb_ref,
                    out_ref, idx_ref, ybuf, f_ref, frow_ref, isem, rsem, *, order, n_steps, s_rows):
    i = pl.program_id(0)
    tt, d = h2_ref.shape
    n_rows = TOP_K * tt

    def idx_copy(tile, buf):
        return pltpu.make_async_copy(slot_hbm.at[tile], idx_ref.at[pl.ds(buf * n_rows, n_rows)], isem.at[buf])

    group = 8
    assert tt % group == 0

    def issue_token(buf, r0, u):
        base = pl.multiple_of(r0 * s_rows, group * s_rows)
        ibase = buf * n_rows + r0
        for k in range(TOP_K):
            t = idx_ref[ibase + (k * tt + u)]
            pltpu.make_async_copy(y_hbm.at[pl.ds(pl.multiple_of(t * s_rows, s_rows), s_rows), :],
                                  ybuf.at[buf, pl.ds(base + (k * tt + u) * s_rows, s_rows), :],
                                  rsem.at[buf]).start()

    @pl.when(i == 0)
    def _():
        idx_copy(0, 0).start()
        idx_copy(0, 0).wait()

        def first(gi, carry):
            for u in range(group):
                issue_token(0, gi * group, u)
            return carry
        lax.fori_loop(0, tt // group, first, 0)
        if n_steps > 1:
            idx_copy(1, 1).start()

    buf = i % 2
    nxt = 1 - buf
    has_next = i + 1 < n_steps

    @pl.when(has_next)
    def _():
        idx_copy(i + 1, nxt).wait()

    hb = h2_ref[...].astype(BF16)
    a = (_silu(jnp.dot(hb, sg_ref[...], preferred_element_type=F32))
         * jnp.dot(hb, su_ref[...], preferred_element_type=F32))
    f_ref[...] = jnp.dot(a.astype(BF16), sd_ref[...], preferred_element_type=F32)

    def token_loop(cur, prefetch):
        pltpu.make_async_copy(y_hbm.at[pl.ds(0, n_rows * s_rows), :], ybuf.at[cur], rsem.at[cur]).wait()

        def body(gi, carry):
            r0 = gi * group
            base = pl.multiple_of(r0 * s_rows, group * s_rows)
            gbase = pl.multiple_of(r0 * TOP_K, group * TOP_K)
            for u in range(group):
                if prefetch:
                    issue_token(1 - cur, r0, u)
                gates = gate_ref[pl.ds(gbase + u * TOP_K, TOP_K), :]
                acc = jnp.zeros((s_rows, LANES), F32)
                for k in range(TOP_K):
                    rows = pl.ds(base + (k * tt + u) * s_rows, s_rows)
                    acc = acc + jnp.broadcast_to(gates[k:k + 1, :], (s_rows, LANES)) * ybuf[cur, rows, :].astype(F32)
                frow_ref[pl.ds(base + u * s_rows, s_rows), :] = acc
            return carry
        lax.fori_loop(0, tt // group, body, 0)

    for cur in (0, 1):
        @pl.when((buf == cur) & has_next)
        def _():
            token_loop(cur, True)

        @pl.when((buf == cur) & jnp.logical_not(has_next))
        def _():
            token_loop(cur, False)

    @pl.when(i + 2 < n_steps)
    def _():
        idx_copy(i + 2, buf).start()

    routed = _from_token_rows(frow_ref[...], s_rows)
    for s in range(s_rows):
        sl = slice(s * LANES, (s + 1) * LANES)
        f_ref[:, sl] = f_ref[:, sl] + routed[s]
    g = order.group(i)
    r = DEEPNORM_ALPHA * x1_ref[...] + g2_ref[0, pl.ds(g, 1), :] * f_ref[...]
    out_ref[...] = _layer_norm(r, lng_ref[...], lnb_ref[...])


def _combine(slots, y_sorted, gate, h2, x1, sg, su, sd, mod, ln_g, ln_b, layer, order):
    n_tok, d = h2.shape
    tt = order.tile
    n_steps = order.n_tiles
    s_rows = d // LANES
    ds_ = sg.shape[1]
    row = pl.BlockSpec((1, d), lambda i: (0, 0))
    tok = pl.BlockSpec((tt, d), lambda i: (i, 0))
    return pl.pallas_call(
        functools.partial(_combine_kernel, order=order, n_steps=n_steps, s_rows=s_rows),
        grid=(n_steps,),
        in_specs=[pl.BlockSpec(memory_space=pl.ANY),
                  pl.BlockSpec(memory_space=pl.ANY),
                  pl.BlockSpec((tt * TOP_K, LANES), lambda i: (i, 0)),
                  tok, tok,
                  pl.BlockSpec((d, ds_), lambda i: (0, 0)),
                  pl.BlockSpec((d, ds_), lambda i: (0, 0)),
                  pl.BlockSpec((ds_, d), lambda i: (0, 0)),
                  _mod_spec(layer, MOD_G2, d, 1), row, row],
        out_specs=tok,
        out_shape=jax.ShapeDtypeStruct((n_tok, d), F32),
        scratch_shapes=[pltpu.SMEM((2 * TOP_K * tt,), jnp.int32),
                        pltpu.VMEM((2, TOP_K * tt * s_rows, LANES), BF16),
                        pltpu.VMEM((tt, d), F32),
                        pltpu.VMEM((tt * s_rows, LANES), F32),
                        pltpu.SemaphoreType.DMA((2,)),
                        pltpu.SemaphoreType.DMA((2,))],
        compiler_params=pltpu.CompilerParams(vmem_limit_bytes=VMEM_LIMIT, disable_bounds_checks=True),
    )(slots, y_sorted, gate, h2, x1, sg, su, sd, mod, ln_g.reshape(1, d), ln_b.reshape(1, d))


def _moe_layer(produced, layer, mod, order, router_w, router_b, wg, wu, wd, sg, su, sd, ln_g, ln_b):
    x1, h2, hp = produced
    n_tok, d = h2.shape
    n_exp = router_w.shape[1]
    bm = MOE_BLOCK
    tt = order.tile
    eidx_t, gate_t, rank_t, cnt = _router(h2, router_w, router_b)

    counts = cnt[:, 0].astype(jnp.int32)
    padded = (counts + bm - 1) // bm * bm
    pad_end = jnp.cumsum(padded).astype(jnp.int32)
    pad_start = pad_end - padded
    experts = jnp.arange(n_exp, dtype=jnp.int32)
    slot_t = rank_t + jnp.sum(jnp.where(eidx_t[:, :, None] == experts, pad_start, 0), axis=-1)
    n_slots = n_tok * TOP_K + n_exp * bm
    nb = n_slots // bm
    n_used = pad_end[-1:] // bm
    blk_expert = jnp.minimum(jnp.sum(pad_end[None, :] <= (jnp.arange(nb, dtype=jnp.int32) * bm)[:, None], axis=1),
                             n_exp - 1).astype(jnp.int32)
    n_tiles = n_tok // tt
    slots_tok = slot_t.T.reshape(n_tiles, tt * TOP_K)
    slots_k = slot_t.reshape(TOP_K, n_tiles, tt).transpose(1, 0, 2).reshape(n_tiles, TOP_K * tt)

    xs = _dispatch(hp, slots_tok, pad_end, padded, n_used, n_slots, bm)
    ys = _moe_experts(xs, blk_expert, n_used, wg, wu, wd, layer, bm)
    gates = jnp.broadcast_to(gate_t.T.reshape(n_tok * TOP_K, 1), (n_tok * TOP_K, LANES))
    return _combine(slots_k, ys, gates, h2, x1, sg[layer].astype(BF16), su[layer].astype(BF16),
                    sd[layer].astype(BF16), mod, ln_g[layer], ln_b[layer], layer, order)


def kernel(x, c, ctx, c_ctx, ada_w, ada_b, ln_mix_g, ln_mix_b, ln_ffn_g, ln_ffn_b, ret_w_in, ret_w_out,
           ret_decay_fwd, ret_decay_bwd, pool_w, pool_scale, router_w, router_b, exp_w_gate, exp_w_up,
           exp_w_down, sh_w_gate, sh_w_up, sh_w_down):
    batch, seq, d = x.shape
    ctx_len = ctx.shape[1]
    heads = RET_HEADS
    qk_dim = d
    head_dim = qk_dim // heads
    v_dim = ret_w_out.shape[1]
    assert ada_w.shape[0] == DEPTH == 2 and batch + 1 <= MOD_ROWS

    cvec = jnp.zeros((MOD_ROWS, d), F32).at[:batch].set(c.astype(F32)).at[batch].set(c_ctx.astype(F32))
    mod = _modulation(cvec, ada_w.astype(F32), ada_b.astype(F32))

    x2d = x.reshape(batch * seq, d)
    ctx2d = ctx.reshape(batch * ctx_len, d)
    order0 = _Order(batch, seq, ctx_len, TOKEN_TILE)

    w_in = ret_w_in[0].astype(F32)
    tabs = _rope_tables(seq, head_dim, TOKEN_TILE)
    qk = _inproj(x2d, ctx2d, mod, w_in, 0, 2 * qk_dim, order0, BF16, rope_tabs=tabs, k_col0=qk_dim,
                 head_dim=head_dim)
    v = _inproj(x2d, ctx2d, mod, w_in, 2 * qk_dim, v_dim, order0, BF16)
    g = _inproj(x2d, ctx2d, mod, w_in, 2 * qk_dim + v_dim, v_dim, order0, F32)
    og = _retention(qk, v, g, ret_decay_fwd[0], ret_decay_bwd[0], batch, ctx_len + seq, ctx_len)
    produced = _outproj(og, x2d, ctx2d, ret_w_out[0].astype(BF16), mod, ln_mix_g[0], ln_mix_b[0], order0)
    xa = _moe_layer(produced, 0, mod, order0, router_w[0], router_b[0], exp_w_gate, exp_w_up, exp_w_down,
                    sh_w_gate, sh_w_up, sh_w_down, ln_ffn_g, ln_ffn_b)

    order1 = _Order(batch, seq, 0, TOKEN_TILE)
    produced = _pool_layer(xa, order0, pool_w[0], pool_scale[0], mod, ln_mix_g[1], ln_mix_b[1], 1, batch, seq)
    xb = _moe_layer(produced, 1, mod, order1, router_w[1], router_b[1], exp_w_gate, exp_w_up, exp_w_down,
                    sh_w_gate, sh_w_up, sh_w_down, ln_ffn_g, ln_ffn_b)
    return xb.reshape(batch, seq, d)
```

```python
import functools

import jax
import jax.numpy as jnp
from jax import lax
from jax.experimental import pallas as pl
from jax.experimental.pallas import tpu as pltpu

F32 = jnp.float32
BF16 = jnp.bfloat16
U32 = jnp.uint32
HIGHEST = lax.Precision.HIGHEST

DEPTH = 2
GRID_W = 64
RET_HEADS = 8
ROPE_BASE = 10000.0
POOL_WINDOWS = (2, 4, 8, 16)
N_EXPERT_GROUPS = 8
TOPK_GROUPS = 4
TOP_K = 8
ROUTED_SCALE = 2.5
LN_EPS = 1e-5
N_MOD = 6
DEEPNORM_ALPHA = (2.0 * DEPTH) ** 0.25
MOD_SH1, MOD_SC1, MOD_G1, MOD_SH2, MOD_SC2, MOD_G2 = range(N_MOD)

LANES = 128
TOKEN_TILE = 256
RET_BLOCK = 2816
RET_CHUNK = 256
ROUTER_TILE = 512
MOE_BLOCK = 512
MOE_PARTS = 2
MOD_ROWS = 8
VMEM_LIMIT = 56 * 1024 * 1024


def _silu(x):
    return x * jax.nn.sigmoid(x)


def _layer_norm(r, g, b):
    mu = jnp.mean(r, axis=-1, keepdims=True)
    var = jnp.mean(jnp.square(r - mu), axis=-1, keepdims=True)
    return (r - mu) * lax.rsqrt(var + LN_EPS) * g + b


def _to_token_rows(h):
    tm, d = h.shape
    s_rows = d // LANES
    blocks = jnp.stack([h[:, s * LANES:(s + 1) * LANES] for s in range(s_rows)], axis=0)
    return jnp.swapaxes(blocks, 0, 1).reshape(tm * s_rows, LANES)


def _from_token_rows(rows, s_rows):
    return jnp.swapaxes(rows.reshape(rows.shape[0] // s_rows, s_rows, LANES), 0, 1)


class _Order:
    def __init__(self, batch, seq, ctx_len, tile):
        assert seq % tile == 0 and ctx_len % tile == 0
        self.batch, self.seq, self.ctx_len, self.tile = batch, seq, ctx_len, tile
        self.ctx_tiles = ctx_len // tile
        self.lat_tiles = seq // tile
        self.per_batch = self.ctx_tiles + self.lat_tiles
        self.n_tiles = batch * self.per_batch
        self.n_tok = self.n_tiles * tile

    def split(self, i):
        b = i // self.per_batch
        r = i % self.per_batch
        return b, r, r < self.ctx_tiles

    def group(self, i):
        b, _, is_ctx = self.split(i)
        return jnp.where(is_ctx, self.batch, b)

    def lat_block(self, i):
        b, r, _ = self.split(i)
        return b * self.lat_tiles + jnp.maximum(r - self.ctx_tiles, 0)

    def ctx_block(self, i):
        b, r, _ = self.split(i)
        return b * self.ctx_tiles + jnp.minimum(r, max(self.ctx_tiles - 1, 0))


def _mod_kernel(c_ref, w_ref, b_ref, o_ref):
    s = _silu(c_ref[...])
    o_ref[0] = jnp.dot(s, w_ref[0], precision=HIGHEST, preferred_element_type=F32) + b_ref[0]


def _modulation(cvec, ada_w, ada_b):
    depth, d, n = ada_w.shape
    tn = 2048 if n % 2048 == 0 else 1024
    return pl.pallas_call(
        _mod_kernel,
        grid=(depth, n // tn),
        in_specs=[pl.BlockSpec((MOD_ROWS, d), lambda l, j: (0, 0)),
                  pl.BlockSpec((1, d, tn), lambda l, j: (l, 0, j)),
                  pl.BlockSpec((1, 1, tn), lambda l, j: (l, 0, j))],
        out_specs=pl.BlockSpec((1, MOD_ROWS, tn), lambda l, j: (l, 0, j)),
        out_shape=jax.ShapeDtypeStruct((depth, MOD_ROWS, n), F32),
        compiler_params=pltpu.CompilerParams(vmem_limit_bytes=VMEM_LIMIT),
    )(cvec, ada_w, ada_b.reshape(depth, 1, n))


def _mod_spec(layer, chunk, d, n_grid_args):
    if n_grid_args == 1:
        return pl.BlockSpec((1, MOD_ROWS, d), lambda i: (layer, 0, chunk))
    return pl.BlockSpec((1, MOD_ROWS, d), lambda j, i: (layer, 0, chunk))


def _inproj_kernel(x_ref, c_ref, sc_ref, sh_ref, w_ref, *rest, order, rope, k_col0, head_dim):
    i = pl.program_id(1)
    j = pl.program_id(0)
    if rope:
        cos_ref, sin_ref, o_ref, wb_ref = rest
    else:
        o_ref, wb_ref = rest

    @pl.when(i == 0)
    def _():
        wb_ref[...] = w_ref[...].astype(BF16)

    g = order.group(i)
    _, _, is_ctx = order.split(i)
    sc = sc_ref[0, pl.ds(g, 1), :]
    sh = sh_ref[0, pl.ds(g, 1), :]
    xin = jnp.where(is_ctx, c_ref[...], x_ref[...])
    a = (xin * (1.0 + sc) + sh).astype(BF16)
    acc = jnp.dot(a, wb_ref[...], preferred_element_type=F32)
    if not rope:
        o_ref[...] = acc.astype(o_ref.dtype)
        return
    tn = acc.shape[1]
    lane = lax.broadcasted_iota(jnp.int32, acc.shape, 1)
    partner = jnp.where((lane & 1) == 0, pltpu.roll(acc, tn - 1, 1), pltpu.roll(acc, 1, 1))
    cos = cos_ref[...]
    sin = sin_ref[...]
    scale = jnp.where(j * tn >= k_col0, head_dim ** -0.5, 1.0).astype(F32)
    for h in range(tn // head_dim):
        sl = slice(h * head_dim, (h + 1) * head_dim)
        o_ref[:, sl] = ((acc[:, sl] * cos + partner[:, sl] * sin) * scale).astype(o_ref.dtype)


def _inproj(x2d, ctx2d, mod, w_in, col0, n, order, out_dtype, rope_tabs=None, k_col0=0, head_dim=0):
    d = w_in.shape[0]
    tm = order.tile
    tn = min(n, 2048)
    assert col0 % tn == 0 and n % tn == 0
    rope = rope_tabs is not None
    in_specs = [pl.BlockSpec((tm, d), lambda j, i: (order.lat_block(i), 0)),
                pl.BlockSpec((tm, d), lambda j, i: (order.ctx_block(i), 0)),
                _mod_spec(0, MOD_SC1, d, 2), _mod_spec(0, MOD_SH1, d, 2),
                pl.BlockSpec((d, tn), lambda j, i: (0, col0 // tn + j), pipeline_mode=pl.Buffered(1))]
    args = [x2d, ctx2d, mod, mod, w_in]
    if rope:
        def tab_map(j, i):
            _, r, is_ctx = order.split(i)
            return (jnp.where(is_ctx, order.lat_tiles, r - order.ctx_tiles), 0)
        in_specs += [pl.BlockSpec((tm, head_dim), tab_map), pl.BlockSpec((tm, head_dim), tab_map)]
        args += list(rope_tabs)
    return pl.pallas_call(
        functools.partial(_inproj_kernel, order=order, rope=rope, k_col0=k_col0, head_dim=head_dim),
        grid=(n // tn, order.n_tiles),
        in_specs=in_specs,
        out_specs=pl.BlockSpec((tm, tn), lambda j, i: (i, j)),
        out_shape=jax.ShapeDtypeStruct((order.n_tok, n), out_dtype),
        scratch_shapes=[pltpu.VMEM((d, tn), BF16)],
        compiler_params=pltpu.CompilerParams(vmem_limit_bytes=VMEM_LIMIT),
    )(*args)


def _rope_tables(seq, head_dim, tile):
    n_freq = head_dim // 4
    pos = jnp.arange(seq, dtype=jnp.int32)
    row = (pos // GRID_W).astype(F32)
    col = (pos % GRID_W).astype(F32)
    freqs = ROPE_BASE ** (-jnp.arange(n_freq, dtype=F32) / n_freq)
    ang = jnp.concatenate([row[:, None] * freqs, col[:, None] * freqs], axis=-1)
    cos = jnp.repeat(jnp.cos(ang), 2, axis=-1)
    sin = jnp.stack([-jnp.sin(ang), jnp.sin(ang)], axis=-1).reshape(seq, head_dim)
    ident = jnp.zeros((tile, head_dim), F32)
    return jnp.concatenate([cos, ident + 1.0], axis=0), jnp.concatenate([sin, ident], axis=0)


def _ret_kernel(q_ref, k_ref, v_ref, g_ref, df_ref, db_ref, o_ref, sf_ref, sb_ref, obuf_ref, *,
                n_blocks, chunks_per_block, ctx_chunks):
    C = RET_CHUNK
    NB, CPB, NCC = n_blocks, chunks_per_block, ctx_chunks
    st = pl.program_id(2)
    phase = st // (NB + 1)
    s = st % (NB + 1)
    ldf = jax.nn.log_sigmoid(df_ref[0])[:, :1]
    ldb = jax.nn.log_sigmoid(db_ref[0])[:, :1]
    pos = lax.broadcasted_iota(jnp.int32, (C, 1), 0).astype(F32)
    tdims = (((0,), (0,)), ((), ()))

    @pl.when(phase == 0)
    def _backward_states():
        @pl.when(s == 0)
        def _():
            sb_ref[...] = jnp.zeros_like(sb_ref)

        q_dec = jnp.exp(ldb * (C - pos))
        k_dec = jnp.exp(ldb * pos)
        chunk_dec = jnp.exp(ldb * C)

        def chunk(blk, c):
            rows = pl.ds(c * C, C)
            q = q_ref[rows, :].astype(F32)
            k = k_ref[rows, :].astype(F32)
            v = v_ref[rows, :]
            state = sb_ref[...]
            goff = pl.multiple_of((blk * CPB + c) * C, C)
            obuf_ref[pl.ds(goff, C), :] = jnp.dot((q * q_dec).astype(BF16), state.astype(BF16),
                                                  preferred_element_type=F32)
            sb_ref[...] = state * chunk_dec + lax.dot_general((k * k_dec).astype(BF16), v, tdims,
                                                              preferred_element_type=F32)

        @pl.when(s == 0)
        def _():
            for c in reversed(range(NCC)):
                chunk(0, c)

        @pl.when((s > 0) & (s < NB))
        def _():
            for c in reversed(range(CPB)):
                chunk(NB - s, c)

        @pl.when(s == NB)
        def _():
            for c in reversed(range(NCC, CPB)):
                chunk(0, c)

    @pl.when((phase == 1) & (s < NB))
    def _forward_and_output():
        @pl.when(s == 0)
        def _():
            sf_ref[...] = jnp.zeros_like(sf_ref)

        q_dec = jnp.exp(ldf * (pos + 1.0))
        k_dec = jnp.exp(ldf * (C - 1.0 - pos))
        chunk_dec = jnp.exp(ldf * C)
        ii = lax.broadcasted_iota(jnp.int32, (C, C), 0)
        jj = lax.broadcasted_iota(jnp.int32, (C, C), 1)
        rel = (ii - jj).astype(F32)
        decay = jnp.where(rel >= 0, jnp.exp(ldf * jnp.maximum(rel, 0.0)), jnp.exp(ldb * jnp.maximum(-rel, 0.0)))

        for c in range(CPB):
            rows = pl.ds(c * C, C)
            qb = q_ref[rows, :]
            kb = k_ref[rows, :]
            v = v_ref[rows, :]
            q = qb.astype(F32)
            k = kb.astype(F32)
            state = sf_ref[...]
            scores = lax.dot_general(qb, kb, (((1,), (1,)), ((), ())), preferred_element_type=F32) * decay
            goff = pl.multiple_of((s * CPB + c) * C, C)
            o = (jnp.dot(scores.astype(BF16), v, preferred_element_type=F32)
                 + jnp.dot((q * q_dec).astype(BF16), state.astype(BF16), preferred_element_type=F32)
                 + obuf_ref[pl.ds(goff, C), :])
            sf_ref[...] = state * chunk_dec + lax.dot_general((k * k_dec).astype(BF16), v, tdims,
                                                              preferred_element_type=F32)
            mu = jnp.mean(o, axis=-1, keepdims=True)
            var = jnp.mean(jnp.square(o - mu), axis=-1, keepdims=True)
            on = (o - mu) * lax.rsqrt(var + LN_EPS)
            o_ref[rows, :] = (_silu(g_ref[rows, :]) * on).astype(o_ref.dtype)


def _retention(qk, v, g, raw_fwd, raw_bwd, batch, tokens_per_batch, ctx_len):
    n_tok, qk2 = qk.shape
    v_dim = v.shape[1]
    heads = RET_HEADS
    dk = qk2 // 2 // heads
    dv = v_dim // heads
    tb = next(t for t in range(RET_BLOCK, 0, -RET_CHUNK) if tokens_per_batch % t == 0)
    assert tb % RET_CHUNK == 0 and ctx_len % RET_CHUNK == 0 and ctx_len <= tb
    nb = tokens_per_batch // tb
    steps = 2 * (nb + 1)

    def blk(st):
        phase = st // (nb + 1)
        s = st % (nb + 1)
        back = jnp.where(s == 0, 0, nb - s)
        return jnp.where(phase == 0, back, jnp.minimum(s, nb - 1))

    def fwd_blk(st):
        phase = st // (nb + 1)
        s = st % (nb + 1)
        return jnp.where(phase == 0, 0, jnp.minimum(s, nb - 1))

    dec = lambda raw: jnp.broadcast_to(raw.astype(F32)[:, None, None], (heads, 1, LANES))
    return pl.pallas_call(
        functools.partial(_ret_kernel, n_blocks=nb, chunks_per_block=tb // RET_CHUNK,
                          ctx_chunks=ctx_len // RET_CHUNK),
        grid=(batch, heads, steps),
        in_specs=[pl.BlockSpec((tb, dk), lambda b, h, st: (b * nb + blk(st), h)),
                  pl.BlockSpec((tb, dk), lambda b, h, st: (b * nb + blk(st), heads + h)),
                  pl.BlockSpec((tb, dv), lambda b, h, st: (b * nb + blk(st), h)),
                  pl.BlockSpec((tb, dv), lambda b, h, st: (b * nb + fwd_blk(st), h)),
                  pl.BlockSpec((1, 1, LANES), lambda b, h, st: (h, 0, 0)),
                  pl.BlockSpec((1, 1, LANES), lambda b, h, st: (h, 0, 0))],
        out_specs=pl.BlockSpec((tb, dv), lambda b, h, st: (b * nb + fwd_blk(st), h)),
        out_shape=jax.ShapeDtypeStruct((n_tok, v_dim), BF16),
        scratch_shapes=[pltpu.VMEM((dk, dv), F32), pltpu.VMEM((dk, dv), F32),
                        pltpu.VMEM((tokens_per_batch, dv), F32)],
        compiler_params=pltpu.CompilerParams(vmem_limit_bytes=VMEM_LIMIT),
    )(qk, qk, v, g, dec(raw_fwd), dec(raw_bwd))


def _outproj_kernel(o_ref, x_ref, c_ref, w_ref, g1_ref, sc2_ref, sh2_ref, lng_ref, lnb_ref,
                    x1_ref, h2_ref, hp_ref, *, order):
    i = pl.program_id(0)
    g = order.group(i)
    _, _, is_ctx = order.split(i)
    y = jnp.dot(o_ref[...], w_ref[...], preferred_element_type=F32)
    xin = jnp.where(is_ctx, c_ref[...], x_ref[...])
    r = DEEPNORM_ALPHA * xin + g1_ref[0, pl.ds(g, 1), :] * y
    x1 = _layer_norm(r, lng_ref[...], lnb_ref[...])
    x1_ref[...] = x1
    h2 = x1 * (1.0 + sc2_ref[0, pl.ds(g, 1), :]) + sh2_ref[0, pl.ds(g, 1), :]
    h2_ref[...] = h2
    hp_ref[...] = _to_token_rows(h2).astype(BF16)


def _outproj(og, x2d, ctx2d, w_out, mod, ln_g, ln_b, order):
    v_dim, d = w_out.shape
    tm = order.tile
    s_rows = d // LANES
    row = pl.BlockSpec((1, d), lambda i: (0, 0))
    tok = pl.BlockSpec((tm, d), lambda i: (i, 0))
    return pl.pallas_call(
        functools.partial(_outproj_kernel, order=order),
        grid=(order.n_tiles,),
        in_specs=[pl.BlockSpec((tm, v_dim), lambda i: (i, 0)),
                  pl.BlockSpec((tm, d), lambda i: (order.lat_block(i), 0)),
                  pl.BlockSpec((tm, d), lambda i: (order.ctx_block(i), 0)),
                  pl.BlockSpec((v_dim, d), lambda i: (0, 0), pipeline_mode=pl.Buffered(1)),
                  _mod_spec(0, MOD_G1, d, 1), _mod_spec(0, MOD_SC2, d, 1), _mod_spec(0, MOD_SH2, d, 1),
                  row, row],
        out_specs=[tok, tok, pl.BlockSpec((tm * s_rows, LANES), lambda i: (i, 0))],
        out_shape=[jax.ShapeDtypeStruct((order.n_tok, d), F32)] * 2
                  + [jax.ShapeDtypeStruct((order.n_tok * s_rows, LANES), BF16)],
        compiler_params=pltpu.CompilerParams(vmem_limit_bytes=VMEM_LIMIT),
    )(og, x2d, ctx2d, w_out, mod, mod, mod, ln_g.reshape(1, d), ln_b.reshape(1, d))


def _pool_kernel(x_ref, pw_ref, ps_ref, sc1_ref, sh1_ref, g1_ref, sc2_ref, sh2_ref, lng_ref, lnb_ref,
                 x1_ref, h2_ref, hp_ref, y_ref, *, lat_tiles):
    i = pl.program_id(0)
    g = i // lat_tiles
    x = x_ref[...]
    tm, d = x.shape
    gd = d // len(POOL_WINDOWS)
    h = x * (1.0 + sc1_ref[0, pl.ds(g, 1), :]) + sh1_ref[0, pl.ds(g, 1), :]
    ni = lax.broadcasted_iota(jnp.int32, (tm, tm), 0)
    mi = lax.broadcasted_iota(jnp.int32, (tm, tm), 1)
    same_row = (ni // GRID_W) == (mi // GRID_W)
    delta = mi - ni
    col = lax.broadcasted_iota(jnp.int32, (tm, 1), 0) % GRID_W
    for gi, w in enumerate(POOL_WINDOWS):
        window = jnp.where(same_row & (delta >= -(w // 2)) & (delta < w // 2), 1.0, 0.0).astype(BF16)
        count = (jnp.minimum(col + w // 2, GRID_W) - jnp.maximum(col - w // 2, 0)).astype(F32)
        hg = h[:, gi * gd:(gi + 1) * gd]
        h_hi = hg.astype(BF16)
        r1 = hg - h_hi.astype(F32)
        h_mid = r1.astype(BF16)
        h_lo = (r1 - h_mid.astype(F32)).astype(BF16)
        total = (jnp.dot(window, h_hi, preferred_element_type=F32)
                 + jnp.dot(window, h_mid, preferred_element_type=F32)
                 + jnp.dot(window, h_lo, preferred_element_type=F32))
        p = total / count - hg
        y_ref[:, gi * gd:(gi + 1) * gd] = jnp.dot(p.astype(BF16), pw_ref[gi], preferred_element_type=F32)
    y = y_ref[...] * ps_ref[...]
    r = DEEPNORM_ALPHA * x + g1_ref[0, pl.ds(g, 1), :] * y
    x1 = _layer_norm(r, lng_ref[...], lnb_ref[...])
    x1_ref[...] = x1
    h2 = x1 * (1.0 + sc2_ref[0, pl.ds(g, 1), :]) + sh2_ref[0, pl.ds(g, 1), :]
    h2_ref[...] = h2
    hp_ref[...] = _to_token_rows(h2).astype(BF16)


def _pool_layer(x_prev, order0, pool_w, pool_scale, mod, ln_g, ln_b, layer, batch, seq):
    d = x_prev.shape[1]
    tm = order0.tile
    s_rows = d // LANES
    lat_tiles = seq // tm
    n_tiles = batch * lat_tiles
    ng, gd, _ = pool_w.shape

    def src(i):
        b = i // lat_tiles
        return (b * order0.per_batch + order0.ctx_tiles + i % lat_tiles, 0)

    row = pl.BlockSpec((1, d), lambda i: (0, 0))
    tok = pl.BlockSpec((tm, d), lambda i: (i, 0))
    ms = lambda chunk: _mod_spec(layer, chunk, d, 1)
    return pl.pallas_call(
        functools.partial(_pool_kernel, lat_tiles=lat_tiles),
        grid=(n_tiles,),
        in_specs=[pl.BlockSpec((tm, d), src),
                  pl.BlockSpec((ng, gd, gd), lambda i: (0, 0, 0)),
                  row, ms(MOD_SC1), ms(MOD_SH1), ms(MOD_G1), ms(MOD_SC2), ms(MOD_SH2), row, row],
        out_specs=[tok, tok, pl.BlockSpec((tm * s_rows, LANES), lambda i: (i, 0))],
        out_shape=[jax.ShapeDtypeStruct((n_tiles * tm, d), F32)] * 2
                  + [jax.ShapeDtypeStruct((n_tiles * tm * s_rows, LANES), BF16)],
        scratch_shapes=[pltpu.VMEM((tm, d), F32)],
        compiler_params=pltpu.CompilerParams(vmem_limit_bytes=VMEM_LIMIT),
    )(x_prev, pool_w.astype(BF16), pool_scale.reshape(1, d), mod, mod, mod, mod, mod,
      ln_g.reshape(1, d), ln_b.reshape(1, d))


def _split_bf16(x):
    hi = x.astype(BF16)
    return hi, (x - hi.astype(F32)).astype(BF16)


def _route_logits(h_hi, h_mid, w_ref):
    nt = (((1,), (1,)), ((), ()))
    w_hi, w_mid = _split_bf16(w_ref[...])
    return (lax.dot_general(w_hi, h_hi, nt, preferred_element_type=F32)
            + (lax.dot_general(w_hi, h_mid, nt, preferred_element_type=F32)
               + lax.dot_general(w_mid, h_hi, nt, preferred_element_type=F32)))


def _route_tile(logits, b_ref, eidx_ref, gate_ref, rank_ref, cnt_ref, carry_ref):
    i = pl.program_id(0)
    n_exp, tr = logits.shape
    per = n_exp // N_EXPERT_GROUPS
    neg = -jnp.inf

    @pl.when(i == 0)
    def _():
        carry_ref[...] = jnp.zeros_like(carry_ref)

    scores = jax.nn.sigmoid(logits)
    biased = scores + b_ref[...]

    sub_idx = lax.broadcasted_iota(jnp.int32, (per, tr), 0)
    group_score = []
    for gi in range(N_EXPERT_GROUPS):
        sub = biased[gi * per:(gi + 1) * per, :]
        m1 = jnp.max(sub, axis=0, keepdims=True)
        first = jnp.min(jnp.where(sub == m1, sub_idx, per), axis=0, keepdims=True)
        m2 = jnp.max(jnp.where(sub_idx == first, neg, sub), axis=0, keepdims=True)
        group_score.append(m1 + m2)
    masked = []
    for gi in range(N_EXPERT_GROUPS):
        ahead = jnp.zeros((1, tr), jnp.int32)
        for gj in range(N_EXPERT_GROUPS):
            if gj == gi:
                continue
            wins = (group_score[gj] >= group_score[gi]) if gj < gi else (group_score[gj] > group_score[gi])
            ahead = ahead + wins.astype(jnp.int32)
        keep = ahead < TOPK_GROUPS
        masked.append(jnp.where(keep, biased[gi * per:(gi + 1) * per, :], neg))
    masked = jnp.concatenate(masked, axis=0)

    eid = lax.broadcasted_iota(jnp.int32, (n_exp, tr), 0)
    picks, gates = [], []
    onehot = jnp.zeros((n_exp, tr), F32)
    for _ in range(TOP_K):
        m = jnp.max(masked, axis=0, keepdims=True)
        first = jnp.min(jnp.where(masked == m, eid, n_exp), axis=0, keepdims=True)
        sel = eid == first
        gates.append(jnp.sum(jnp.where(sel, scores, 0.0), axis=0, keepdims=True))
        picks.append(first)
        masked = jnp.where(sel, neg, masked)
        onehot = onehot + sel.astype(F32)
    gate_sum = gates[0]
    for gk in gates[1:]:
        gate_sum = gate_sum + gk

    si = lax.broadcasted_iota(jnp.int32, (tr, tr), 0)
    ti = lax.broadcasted_iota(jnp.int32, (tr, tr), 1)
    before = jnp.where(si < ti, 1.0, 0.0).astype(BF16)
    prefix = jnp.dot(onehot.astype(BF16), before, preferred_element_type=F32) + carry_ref[:, :1]
    for k in range(TOP_K):
        eidx_ref[k:k + 1, :] = picks[k]
        gate_ref[k:k + 1, :] = gates[k] / gate_sum * ROUTED_SCALE
        rank_ref[k:k + 1, :] = jnp.sum(jnp.where(eid == picks[k], prefix, 0.0), axis=0,
                                       keepdims=True).astype(jnp.int32)
    carry_ref[...] = carry_ref[...] + jnp.sum(onehot, axis=1, keepdims=True)
    cnt_ref[...] = carry_ref[...]


def _router_kernel(h_ref, w_ref, b_ref, sg_ref, su_ref, sd_ref, eidx_ref, gate_ref, rank_ref, cnt_ref, fsh_ref,
                   carry_ref):
    hb, h_mid = _split_bf16(h_ref[...])
    logits = _route_logits(hb, h_mid, w_ref)
    a = (_silu(jnp.dot(hb, sg_ref[...], preferred_element_type=F32))
         * jnp.dot(hb, su_ref[...], preferred_element_type=F32))
    fsh_ref[...] = jnp.dot(a.astype(BF16), sd_ref[...], preferred_element_type=F32)
    _route_tile(logits, b_ref, eidx_ref, gate_ref, rank_ref, cnt_ref, carry_ref)


def _router(h2, router_w, router_b, sg, su, sd):
    n_tok, d = h2.shape
    n_exp = router_w.shape[1]
    ds_ = sg.shape[1]
    tr = ROUTER_TILE
    assert n_tok % tr == 0
    sel = pl.BlockSpec((TOP_K, tr), lambda i: (0, i))
    tok = pl.BlockSpec((tr, d), lambda i: (i, 0))
    return pl.pallas_call(
        _router_kernel,
        grid=(n_tok // tr,),
        in_specs=[tok,
                  pl.BlockSpec((n_exp, d), lambda i: (0, 0)),
                  pl.BlockSpec((n_exp, 1), lambda i: (0, 0)),
                  pl.BlockSpec((d, ds_), lambda i: (0, 0)),
                  pl.BlockSpec((d, ds_), lambda i: (0, 0)),
                  pl.BlockSpec((ds_, d), lambda i: (0, 0))],
        out_specs=[sel, sel, sel, pl.BlockSpec((n_exp, LANES), lambda i: (0, 0)), tok],
        out_shape=[jax.ShapeDtypeStruct((TOP_K, n_tok), jnp.int32),
                   jax.ShapeDtypeStruct((TOP_K, n_tok), F32),
                   jax.ShapeDtypeStruct((TOP_K, n_tok), jnp.int32),
                   jax.ShapeDtypeStruct((n_exp, LANES), F32),
                   jax.ShapeDtypeStruct((n_tok, d), F32)],
        scratch_shapes=[pltpu.VMEM((n_exp, LANES), F32)],
        compiler_params=pltpu.CompilerParams(vmem_limit_bytes=VMEM_LIMIT),
    )(h2, router_w.astype(F32).T, router_b.astype(F32).reshape(n_exp, 1), sg, su, sd)


def _dispatch_kernel(pe_ref, pd_ref, nu_ref, slot_hbm, hp_ref, xs_hbm, idx_ref, zero_ref, isem, zsem, rsem, *,
                     n_steps, n_exp, n_blocks, s_rows, bm):
    i = pl.program_id(0)
    tt = hp_ref.shape[0] // s_rows
    zrows = bm * s_rows
    per_tile = tt * TOP_K

    def idx_copy(tile, buf):
        return pltpu.make_async_copy(slot_hbm.at[tile], idx_ref.at[pl.ds(buf * per_tile, per_tile)], isem.at[buf])

    def zero_copy(row0):
        return pltpu.make_async_copy(zero_ref, xs_hbm.at[pl.ds(pl.multiple_of(row0, s_rows), zrows), :], zsem)

    @pl.when(i == 0)
    def _():
        zero_ref[...] = jnp.zeros_like(zero_ref)
        nu = nu_ref[0]

        def fill(fn):
            def per_expert(e, carry):
                @pl.when(pd_ref[e] > 0)
                def _():
                    fn(zero_copy((pe_ref[e] - bm) * s_rows))
                return carry
            lax.fori_loop(0, n_exp, per_expert, 0)

            def per_tail(b, carry):
                fn(zero_copy(b * zrows))
                return carry
            lax.fori_loop(nu, n_blocks, per_tail, 0)

        fill(lambda cp: cp.start())
        fill(lambda cp: cp.wait())
        idx_copy(0, 0).start()
        idx_copy(0, 0).wait()
        if n_steps > 1:
            idx_copy(1, 1).start()

    buf = i % 2

    @pl.when(i > 0)
    def _():
        idx_copy(i, buf).wait()

    group = 8
    assert tt % group == 0

    def body(gi, carry):
        base = pl.multiple_of(gi * (group * s_rows), group * s_rows)
        ibase = buf * per_tile + gi * (group * TOP_K)
        for u in range(group):
            src = hp_ref.at[pl.ds(base + u * s_rows, s_rows), :]
            for k in range(TOP_K):
                t = idx_ref[ibase + (u * TOP_K + k)]
                pltpu.make_async_copy(src, xs_hbm.at[pl.ds(pl.multiple_of(t * s_rows, s_rows), s_rows), :],
                                      rsem).start()
        return carry

    lax.fori_loop(0, tt // group, body, 0)

    @pl.when((i > 0) & (i + 1 < n_steps))
    def _():
        idx_copy(i + 1, 1 - buf).start()

    for _ in range(TOP_K):
        pltpu.make_async_copy(hp_ref, xs_hbm.at[pl.ds(0, tt * s_rows), :], rsem).wait()


def _dispatch(hp, slots, pad_end, padded, n_used, n_slots, bm):
    rows, _ = hp.shape
    n_steps, per_tile = slots.shape
    tt = per_tile // TOP_K
    s_rows = rows // (n_steps * tt)
    n_exp = pad_end.shape[0]
    grid_spec = pltpu.PrefetchScalarGridSpec(
        num_scalar_prefetch=3,
        grid=(n_steps,),
        in_specs=[pl.BlockSpec(memory_space=pl.ANY),
                  pl.BlockSpec((tt * s_rows, LANES), lambda i, pe, pd, nu: (i, 0))],
        out_specs=pl.BlockSpec(memory_space=pl.ANY),
        scratch_shapes=[pltpu.SMEM((2 * per_tile,), jnp.int32),
                        pltpu.VMEM((bm * s_rows, LANES), BF16),
                        pltpu.SemaphoreType.DMA((2,)),
                        pltpu.SemaphoreType.DMA(()),
                        pltpu.SemaphoreType.DMA(())])
    return pl.pallas_call(
        functools.partial(_dispatch_kernel, n_steps=n_steps, n_exp=n_exp, n_blocks=n_slots // bm,
                          s_rows=s_rows, bm=bm),
        grid_spec=grid_spec,
        out_shape=jax.ShapeDtypeStruct((n_slots * s_rows, LANES), BF16),
        compiler_params=pltpu.CompilerParams(vmem_limit_bytes=VMEM_LIMIT, disable_bounds_checks=True),
    )(pad_end, padded, n_used, slots, hp)


def _moe_kernel(be_ref, nu_ref, vb_ref, x_ref, wg_ref, wu_ref, wd_ref, y_ref, wgb, wub, wdb, *, s_rows):
    i = pl.program_id(0)
    nu = nu_ref[0]
    parts = MOE_PARTS
    pm = x_ref.shape[0] // s_rows // parts
    rows_of = [slice(p * pm * s_rows, (p + 1) * pm * s_rows) for p in range(parts)]

    def experts(active):
        hidden = []
        for p in active:
            xt = _from_token_rows(x_ref[rows_of[p], :].astype(F32), s_rows)
            gate = up = None
            for c in range(s_rows // 2):
                xc = jnp.concatenate([xt[2 * c], xt[2 * c + 1]], axis=1).astype(BF16)
                rows = slice(2 * c * LANES, (2 * c + 2) * LANES)
                gc = jnp.dot(xc, wgb[rows, :], preferred_element_type=F32)
                uc = jnp.dot(xc, wub[rows, :], preferred_element_type=F32)
                gate = gc if gate is None else gate + gc
                up = uc if up is None else up + uc
            hidden.append((gate, up))
        outs = []
        for gate, up in hidden:
            a = (_silu(gate) * up).astype(BF16)
            outs.append(jnp.dot(a, wdb[...], preferred_element_type=F32))
        for p, y in zip(active, outs):
            y_ref[rows_of[p], :] = _to_token_rows(y).astype(BF16)
        for p in range(parts):
            if p not in active:
                y_ref[rows_of[p], :] = jnp.zeros((pm * s_rows, LANES), BF16)

    @pl.when(i < nu)
    def _():
        @pl.when((i == 0) | (be_ref[i] != be_ref[jnp.maximum(i - 1, 0)]))
        def _():
            wgb[...] = wg_ref[0, 0].astype(BF16)
            wub[...] = wu_ref[0, 0].astype(BF16)
            wdb[...] = wd_ref[0, 0].astype(BF16)

        valid = vb_ref[i]
        for n_active in range(1, parts + 1):
            above = (valid > (n_active - 1) * pm) if n_active > 1 else True
            within = (valid <= n_active * pm) if n_active < parts else True

            @pl.when(above & within)
            def _():
                experts(list(range(n_active)))

    @pl.when(i >= nu)
    def _():
        y_ref[...] = jnp.zeros_like(y_ref)


def _moe_experts(xs, blk_expert, n_used, blk_valid, wg, wu, wd, layer, bm):
    _, n_exp, d, de = wg.shape
    s_rows = d // LANES
    nb = xs.shape[0] // (bm * s_rows)
    blk = pl.BlockSpec((bm * s_rows, LANES), lambda i, be, nu, vb: (jnp.minimum(i, nu[0] - 1), 0))
    grid_spec = pltpu.PrefetchScalarGridSpec(
        num_scalar_prefetch=3,
        grid=(nb,),
        in_specs=[blk,
                  pl.BlockSpec((1, 1, d, de), lambda i, be, nu, vb: (layer, be[i], 0, 0)),
                  pl.BlockSpec((1, 1, d, de), lambda i, be, nu, vb: (layer, be[i], 0, 0)),
                  pl.BlockSpec((1, 1, de, d), lambda i, be, nu, vb: (layer, be[i], 0, 0))],
        out_specs=pl.BlockSpec((bm * s_rows, LANES), lambda i, be, nu, vb: (i, 0)),
        scratch_shapes=[pltpu.VMEM((d, de), BF16), pltpu.VMEM((d, de), BF16), pltpu.VMEM((de, d), BF16)])
    return pl.pallas_call(
        functools.partial(_moe_kernel, s_rows=s_rows),
        grid_spec=grid_spec,
        out_shape=jax.ShapeDtypeStruct(xs.shape, BF16),
        compiler_params=pltpu.CompilerParams(vmem_limit_bytes=VMEM_LIMIT),
    )(blk_expert, n_used, blk_valid, xs, wg, wu, wd)


def _combine_kernel(slot_hbm, y_hbm, gate_ref, fsh_ref, x1_ref, g2_ref, lng_ref, lnb_ref,
                    out_ref, idx_ref, ybuf, frow_ref, isem, rsem, *, order, n_steps, s_rows):
    i = pl.program_id(0)
    tt, d = x1_ref.shape
    n_rows = TOP_K * tt

    def idx_copy(tile, buf):
        return pltpu.make_async_copy(slot_hbm.at[tile], idx_ref.at[pl.ds(buf * n_rows, n_rows)], isem.at[buf])

    group = 8
    assert tt % group == 0

    def issue_token(buf, r0, u):
        base = pl.multiple_of(r0 * s_rows, group * s_rows)
        ibase = buf * n_rows + r0
        for k in range(TOP_K):
            t = idx_ref[ibase + (k * tt + u)]
            pltpu.make_async_copy(y_hbm.at[pl.ds(pl.multiple_of(t * s_rows, s_rows), s_rows), :],
                                  ybuf.at[buf, pl.ds(base + (k * tt + u) * s_rows, s_rows), :],
                                  rsem.at[buf]).start()

    @pl.when(i == 0)
    def _():
        idx_copy(0, 0).start()
        idx_copy(0, 0).wait()

        def first(gi, carry):
            for u in range(group):
                issue_token(0, gi * group, u)
            return carry
        lax.fori_loop(0, tt // group, first, 0)
        if n_steps > 1:
            idx_copy(1, 1).start()

    buf = i % 2
    nxt = 1 - buf
    has_next = i + 1 < n_steps

    @pl.when(has_next)
    def _():
        idx_copy(i + 1, nxt).wait()

    def token_loop(cur, prefetch):
        pltpu.make_async_copy(y_hbm.at[pl.ds(0, n_rows * s_rows), :], ybuf.at[cur], rsem.at[cur]).wait()

        def body(gi, carry):
            r0 = gi * group
            base = pl.multiple_of(r0 * s_rows, group * s_rows)
            gbase = pl.multiple_of(r0 * TOP_K, group * TOP_K)
            for u in range(group):
                if prefetch:
                    issue_token(1 - cur, r0, u)
                gates = gate_ref[pl.ds(gbase + u * TOP_K, TOP_K), :]
                acc = jnp.zeros((s_rows, LANES), F32)
                for k in range(TOP_K):
                    rows = pl.ds(base + (k * tt + u) * s_rows, s_rows)
                    acc = acc + jnp.broadcast_to(gates[k:k + 1, :], (s_rows, LANES)) * ybuf[cur, rows, :].astype(F32)
                frow_ref[pl.ds(base + u * s_rows, s_rows), :] = acc
            return carry
        lax.fori_loop(0, tt // group, body, 0)

    for cur in (0, 1):
        @pl.when((buf == cur) & has_next)
        def _():
            token_loop(cur, True)

        @pl.when((buf == cur) & jnp.logical_not(has_next))
        def _():
            token_loop(cur, False)

    @pl.when(i + 2 < n_steps)
    def _():
        idx_copy(i + 2, buf).start()

    routed = _from_token_rows(frow_ref[...], s_rows)
    f = fsh_ref[...] + jnp.concatenate([routed[s] for s in range(s_rows)], axis=1)
    g = order.group(i)
    r = DEEPNORM_ALPHA * x1_ref[...] + g2_ref[0, pl.ds(g, 1), :] * f
    out_ref[...] = _layer_norm(r, lng_ref[...], lnb_ref[...])


def _combine(slots, y_sorted, gate, fsh, x1, mod, ln_g, ln_b, layer, order):
    n_tok, d = x1.shape
    tt = order.tile
    n_steps = order.n_tiles
    s_rows = d // LANES
    row = pl.BlockSpec((1, d), lambda i: (0, 0))
    tok = pl.BlockSpec((tt, d), lambda i: (i, 0))
    return pl.pallas_call(
        functools.partial(_combine_kernel, order=order, n_steps=n_steps, s_rows=s_rows),
        grid=(n_steps,),
        in_specs=[pl.BlockSpec(memory_space=pl.ANY),
                  pl.BlockSpec(memory_space=pl.ANY),
                  pl.BlockSpec((tt * TOP_K, LANES), lambda i: (i, 0)),
                  tok, tok,
                  _mod_spec(layer, MOD_G2, d, 1), row, row],
        out_specs=tok,
        out_shape=jax.ShapeDtypeStruct((n_tok, d), F32),
        scratch_shapes=[pltpu.SMEM((2 * TOP_K * tt,), jnp.int32),
                        pltpu.VMEM((2, TOP_K * tt * s_rows, LANES), BF16),
                        pltpu.VMEM((tt * s_rows, LANES), F32),
                        pltpu.SemaphoreType.DMA((2,)),
                        pltpu.SemaphoreType.DMA((2,))],
        compiler_params=pltpu.CompilerParams(vmem_limit_bytes=VMEM_LIMIT, disable_bounds_checks=True),
    )(slots, y_sorted, gate, fsh, x1, mod, ln_g.reshape(1, d), ln_b.reshape(1, d))


def _moe_layer(produced, layer, mod, order, router_w, router_b, wg, wu, wd, sg, su, sd, ln_g, ln_b):
    x1, h2, hp = produced
    n_tok, d = h2.shape
    n_exp = router_w.shape[1]
    bm = MOE_BLOCK
    tt = order.tile
    eidx_t, gate_t, rank_t, cnt, fsh = _router(h2, router_w, router_b, sg[layer].astype(BF16),
                                               su[layer].astype(BF16), sd[layer].astype(BF16))

    counts = cnt[:, 0].astype(jnp.int32)
    padded = (counts + bm - 1) // bm * bm
    pad_end = jnp.cumsum(padded).astype(jnp.int32)
    pad_start = pad_end - padded
    experts = jnp.arange(n_exp, dtype=jnp.int32)
    slot_t = rank_t + jnp.sum(jnp.where(eidx_t[:, :, None] == experts, pad_start, 0), axis=-1)
    n_slots = n_tok * TOP_K + n_exp * bm
    nb = n_slots // bm
    n_used = pad_end[-1:] // bm
    blk_expert = jnp.minimum(jnp.sum(pad_end[None, :] <= (jnp.arange(nb, dtype=jnp.int32) * bm)[:, None], axis=1),
                             n_exp - 1).astype(jnp.int32)
    of_block = blk_expert[:, None] == experts
    blk_first = jnp.sum(jnp.where(of_block, pad_start, 0), axis=1) // bm
    blk_valid = jnp.clip(jnp.sum(jnp.where(of_block, counts, 0), axis=1)
                         - (jnp.arange(nb, dtype=jnp.int32) - blk_first) * bm, 0, bm).astype(jnp.int32)
    n_tiles = n_tok // tt
    slots_tok = slot_t.T.reshape(n_tiles, tt * TOP_K)
    slots_k = slot_t.reshape(TOP_K, n_tiles, tt).transpose(1, 0, 2).reshape(n_tiles, TOP_K * tt)

    xs = _dispatch(hp, slots_tok, pad_end, padded, n_used, n_slots, bm)
    ys = _moe_experts(xs, blk_expert, n_used, blk_valid, wg, wu, wd, layer, bm)
    gates = jnp.broadcast_to(gate_t.T.reshape(n_tok * TOP_K, 1), (n_tok * TOP_K, LANES))
    return _combine(slots_k, ys, gates, fsh, x1, mod, ln_g[layer], ln_b[layer], layer, order)


def kernel(x, c, ctx, c_ctx, ada_w, ada_b, ln_mix_g, ln_mix_b, ln_ffn_g, ln_ffn_b, ret_w_in, ret_w_out,
           ret_decay_fwd, ret_decay_bwd, pool_w, pool_scale, router_w, router_b, exp_w_gate, exp_w_up,
           exp_w_down, sh_w_gate, sh_w_up, sh_w_down):
    batch, seq, d = x.shape
    ctx_len = ctx.shape[1]
    heads = RET_HEADS
    qk_dim = d
    head_dim = qk_dim // heads
    v_dim = ret_w_out.shape[1]
    assert ada_w.shape[0] == DEPTH == 2 and batch + 1 <= MOD_ROWS

    cvec = jnp.zeros((MOD_ROWS, d), F32).at[:batch].set(c.astype(F32)).at[batch].set(c_ctx.astype(F32))
    mod = _modulation(cvec, ada_w.astype(F32), ada_b.astype(F32))

    x2d = x.reshape(batch * seq, d)
    ctx2d = ctx.reshape(batch * ctx_len, d)
    order0 = _Order(batch, seq, ctx_len, TOKEN_TILE)

    w_in = ret_w_in[0].astype(F32)
    tabs = _rope_tables(seq, head_dim, TOKEN_TILE)
    qk = _inproj(x2d, ctx2d, mod, w_in, 0, 2 * qk_dim, order0, BF16, rope_tabs=tabs, k_col0=qk_dim,
                 head_dim=head_dim)
    v = _inproj(x2d, ctx2d, mod, w_in, 2 * qk_dim, v_dim, order0, BF16)
    g = _inproj(x2d, ctx2d, mod, w_in, 2 * qk_dim + v_dim, v_dim, order0, F32)
    og = _retention(qk, v, g, ret_decay_fwd[0], ret_decay_bwd[0], batch, ctx_len + seq, ctx_len)
    produced = _outproj(og, x2d, ctx2d, ret_w_out[0].astype(BF16), mod, ln_mix_g[0], ln_mix_b[0], order0)
    xa = _moe_layer(produced, 0, mod, order0, router_w[0], router_b[0], exp_w_gate, exp_w_up, exp_w_down,
                    sh_w_gate, sh_w_up, sh_w_down, ln_ffn_g, ln_ffn_b)

    order1 = _Order(batch, seq, 0, TOKEN_TILE)
    produced = _pool_layer(xa, order0, pool_w[0], pool_scale[0], mod, ln_mix_g[1], ln_mix_b[1], 1, batch, seq)
    xb = _moe_layer(produced, 1, mod, order1, router_w[1], router_b[1], exp_w_gate, exp_w_up, exp_w_down,
                    sh_w_gate, sh_w_up, sh_w_down, ln_ffn_g, ln_ffn_b)
    return xb.reshape(batch, seq, d)
```

```python
import functools

import jax
import jax.numpy as jnp
from jax import lax
from jax.experimental import pallas as pl
from jax.experimental.pallas import tpu as pltpu

F32 = jnp.float32
BF16 = jnp.bfloat16
U32 = jnp.uint32
HIGHEST = lax.Precision.HIGHEST

DEPTH = 2
GRID_W = 64
RET_HEADS = 8
ROPE_BASE = 10000.0
POOL_WINDOWS = (2, 4, 8, 16)
N_EXPERT_GROUPS = 8
TOPK_GROUPS = 4
TOP_K = 8
ROUTED_SCALE = 2.5
LN_EPS = 1e-5
N_MOD = 6
DEEPNORM_ALPHA = (2.0 * DEPTH) ** 0.25
MOD_SH1, MOD_SC1, MOD_G1, MOD_SH2, MOD_SC2, MOD_G2 = range(N_MOD)

LANES = 128
TOKEN_TILE = 256
RET_BLOCK = 2816
RET_CHUNK = 256
ROUTER_TILE = 512
MOE_BLOCK = 512
MOE_PARTS = 2
MOD_ROWS = 8
VMEM_LIMIT = 56 * 1024 * 1024


def _silu(x):
    return x * jax.nn.sigmoid(x)


def _layer_norm(r, g, b):
    mu = jnp.mean(r, axis=-1, keepdims=True)
    var = jnp.mean(jnp.square(r - mu), axis=-1, keepdims=True)
    return (r - mu) * lax.rsqrt(var + LN_EPS) * g + b


def _to_token_rows(h):
    tm, d = h.shape
    s_rows = d // LANES
    blocks = jnp.stack([h[:, s * LANES:(s + 1) * LANES] for s in range(s_rows)], axis=0)
    return jnp.swapaxes(blocks, 0, 1).reshape(tm * s_rows, LANES)


def _from_token_rows(rows, s_rows):
    return jnp.swapaxes(rows.reshape(rows.shape[0] // s_rows, s_rows, LANES), 0, 1)


class _Order:
    def __init__(self, batch, seq, ctx_len, tile):
        assert seq % tile == 0 and ctx_len % tile == 0
        self.batch, self.seq, self.ctx_len, self.tile = batch, seq, ctx_len, tile
        self.ctx_tiles = ctx_len // tile
        self.lat_tiles = seq // tile
        self.per_batch = self.ctx_tiles + self.lat_tiles
        self.n_tiles = batch * self.per_batch
        self.n_tok = self.n_tiles * tile

    def split(self, i):
        b = i // self.per_batch
        r = i % self.per_batch
        return b, r, r < self.ctx_tiles

    def group(self, i):
        b, _, is_ctx = self.split(i)
        return jnp.where(is_ctx, self.batch, b)

    def lat_block(self, i):
        b, r, _ = self.split(i)
        return b * self.lat_tiles + jnp.maximum(r - self.ctx_tiles, 0)

    def ctx_block(self, i):
        b, r, _ = self.split(i)
        return b * self.ctx_tiles + jnp.minimum(r, max(self.ctx_tiles - 1, 0))


def _mod_kernel(c_ref, w_ref, b_ref, o_ref):
    s = _silu(c_ref[...])
    o_ref[0] = jnp.dot(s, w_ref[0], precision=HIGHEST, preferred_element_type=F32) + b_ref[0]


def _modulation(cvec, ada_w, ada_b):
    depth, d, n = ada_w.shape
    tn = 2048 if n % 2048 == 0 else 1024
    return pl.pallas_call(
        _mod_kernel,
        grid=(depth, n // tn),
        in_specs=[pl.BlockSpec((MOD_ROWS, d), lambda l, j: (0, 0)),
                  pl.BlockSpec((1, d, tn), lambda l, j: (l, 0, j)),
                  pl.BlockSpec((1, 1, tn), lambda l, j: (l, 0, j))],
        out_specs=pl.BlockSpec((1, MOD_ROWS, tn), lambda l, j: (l, 0, j)),
        out_shape=jax.ShapeDtypeStruct((depth, MOD_ROWS, n), F32),
        compiler_params=pltpu.CompilerParams(vmem_limit_bytes=VMEM_LIMIT),
    )(cvec, ada_w, ada_b.reshape(depth, 1, n))


def _mod_spec(layer, chunk, d, n_grid_args):
    if n_grid_args == 1:
        return pl.BlockSpec((1, MOD_ROWS, d), lambda i: (layer, 0, chunk))
    return pl.BlockSpec((1, MOD_ROWS, d), lambda j, i: (layer, 0, chunk))


def _inproj_kernel(x_ref, c_ref, sc_ref, sh_ref, w_ref, *rest, order, rope, k_col0, head_dim):
    i = pl.program_id(1)
    j = pl.program_id(0)
    if rope:
        cos_ref, sin_ref, o_ref, wb_ref = rest
    else:
        o_ref, wb_ref = rest

    @pl.when(i == 0)
    def _():
        wb_ref[...] = w_ref[...].astype(BF16)

    g = order.group(i)
    _, _, is_ctx = order.split(i)
    sc = sc_ref[0, pl.ds(g, 1), :]
    sh = sh_ref[0, pl.ds(g, 1), :]
    xin = jnp.where(is_ctx, c_ref[...], x_ref[...])
    a = (xin * (1.0 + sc) + sh).astype(BF16)
    acc = jnp.dot(a, wb_ref[...], preferred_element_type=F32)
    if not rope:
        o_ref[...] = acc.astype(o_ref.dtype)
        return
    tn = acc.shape[1]
    lane = lax.broadcasted_iota(jnp.int32, acc.shape, 1)
    partner = jnp.where((lane & 1) == 0, pltpu.roll(acc, tn - 1, 1), pltpu.roll(acc, 1, 1))
    cos = cos_ref[...]
    sin = sin_ref[...]
    scale = jnp.where(j * tn >= k_col0, head_dim ** -0.5, 1.0).astype(F32)
    for h in range(tn // head_dim):
        sl = slice(h * head_dim, (h + 1) * head_dim)
        o_ref[:, sl] = ((acc[:, sl] * cos + partner[:, sl] * sin) * scale).astype(o_ref.dtype)


def _inproj(x2d, ctx2d, mod, w_in, col0, n, order, out_dtype, rope_tabs=None, k_col0=0, head_dim=0):
    d = w_in.shape[0]
    tm = order.tile
    tn = min(n, 2048)
    assert col0 % tn == 0 and n % tn == 0
    rope = rope_tabs is not None
    in_specs = [pl.BlockSpec((tm, d), lambda j, i: (order.lat_block(i), 0)),
                pl.BlockSpec((tm, d), lambda j, i: (order.ctx_block(i), 0)),
                _mod_spec(0, MOD_SC1, d, 2), _mod_spec(0, MOD_SH1, d, 2),
                pl.BlockSpec((d, tn), lambda j, i: (0, col0 // tn + j), pipeline_mode=pl.Buffered(1))]
    args = [x2d, ctx2d, mod, mod, w_in]
    if rope:
        def tab_map(j, i):
            _, r, is_ctx = order.split(i)
            return (jnp.where(is_ctx, order.lat_tiles, r - order.ctx_tiles), 0)
        in_specs += [pl.BlockSpec((tm, head_dim), tab_map), pl.BlockSpec((tm, head_dim), tab_map)]
        args += list(rope_tabs)
    return pl.pallas_call(
        functools.partial(_inproj_kernel, order=order, rope=rope, k_col0=k_col0, head_dim=head_dim),
        grid=(n // tn, order.n_tiles),
        in_specs=in_specs,
        out_specs=pl.BlockSpec((tm, tn), lambda j, i: (i, j)),
        out_shape=jax.ShapeDtypeStruct((order.n_tok, n), out_dtype),
        scratch_shapes=[pltpu.VMEM((d, tn), BF16)],
        compiler_params=pltpu.CompilerParams(vmem_limit_bytes=VMEM_LIMIT),
    )(*args)


def _rope_tables(seq, head_dim, tile):
    n_freq = head_dim // 4
    pos = jnp.arange(seq, dtype=jnp.int32)
    row = (pos // GRID_W).astype(F32)
    col = (pos % GRID_W).astype(F32)
    freqs = ROPE_BASE ** (-jnp.arange(n_freq, dtype=F32) / n_freq)
    ang = jnp.concatenate([row[:, None] * freqs, col[:, None] * freqs], axis=-1)
    cos = jnp.repeat(jnp.cos(ang), 2, axis=-1)
    sin = jnp.stack([-jnp.sin(ang), jnp.sin(ang)], axis=-1).reshape(seq, head_dim)
    ident = jnp.zeros((tile, head_dim), F32)
    return jnp.concatenate([cos, ident + 1.0], axis=0), jnp.concatenate([sin, ident], axis=0)


def _ret_kernel(q_ref, k_ref, v_ref, g_ref, df_ref, db_ref, o_ref, sf_ref, sb_ref, obuf_ref, *,
                n_blocks, chunks_per_block, ctx_chunks):
    C = RET_CHUNK
    NB, CPB, NCC = n_blocks, chunks_per_block, ctx_chunks
    st = pl.program_id(2)
    phase = st // (NB + 1)
    s = st % (NB + 1)
    ldf = jax.nn.log_sigmoid(df_ref[0])[:, :1]
    ldb = jax.nn.log_sigmoid(db_ref[0])[:, :1]
    pos = lax.broadcasted_iota(jnp.int32, (C, 1), 0).astype(F32)
    tdims = (((0,), (0,)), ((), ()))

    @pl.when(phase == 0)
    def _backward_states():
        @pl.when(s == 0)
        def _():
            sb_ref[...] = jnp.zeros_like(sb_ref)

        q_dec = jnp.exp(ldb * (C - pos))
        k_dec = jnp.exp(ldb * pos)
        chunk_dec = jnp.exp(ldb * C)

        def chunk(blk, c):
            rows = pl.ds(c * C, C)
            q = q_ref[rows, :].astype(F32)
            k = k_ref[rows, :].astype(F32)
            v = v_ref[rows, :]
            state = sb_ref[...]
            goff = pl.multiple_of((blk * CPB + c) * C, C)
            obuf_ref[pl.ds(goff, C), :] = jnp.dot((q * q_dec).astype(BF16), state.astype(BF16),
                                                  preferred_element_type=F32)
            sb_ref[...] = state * chunk_dec + lax.dot_general((k * k_dec).astype(BF16), v, tdims,
                                                              preferred_element_type=F32)

        @pl.when(s == 0)
        def _():
            for c in reversed(range(NCC)):
                chunk(0, c)

        @pl.when((s > 0) & (s < NB))
        def _():
            for c in reversed(range(CPB)):
                chunk(NB - s, c)

        @pl.when(s == NB)
        def _():
            for c in reversed(range(NCC, CPB)):
                chunk(0, c)

    @pl.when((phase == 1) & (s < NB))
    def _forward_and_output():
        @pl.when(s == 0)
        def _():
            sf_ref[...] = jnp.zeros_like(sf_ref)

        q_dec = jnp.exp(ldf * (pos + 1.0))
        k_dec = jnp.exp(ldf * (C - 1.0 - pos))
        chunk_dec = jnp.exp(ldf * C)
        ii = lax.broadcasted_iota(jnp.int32, (C, C), 0)
        jj = lax.broadcasted_iota(jnp.int32, (C, C), 1)
        rel = (ii - jj).astype(F32)
        decay = jnp.where(rel >= 0, jnp.exp(ldf * jnp.maximum(rel, 0.0)), jnp.exp(ldb * jnp.maximum(-rel, 0.0)))

        for c in range(CPB):
            rows = pl.ds(c * C, C)
            qb = q_ref[rows, :]
            kb = k_ref[rows, :]
            v = v_ref[rows, :]
            q = qb.astype(F32)
            k = kb.astype(F32)
            state = sf_ref[...]
            scores = lax.dot_general(qb, kb, (((1,), (1,)), ((), ())), preferred_element_type=F32) * decay
            goff = pl.multiple_of((s * CPB + c) * C, C)
            o = (jnp.dot(scores.astype(BF16), v, preferred_element_type=F32)
                 + jnp.dot((q * q_dec).astype(BF16), state.astype(BF16), preferred_element_type=F32)
                 + obuf_ref[pl.ds(goff, C), :])
            sf_ref[...] = state * chunk_dec + lax.dot_general((k * k_dec).astype(BF16), v, tdims,
                                                              preferred_element_type=F32)
            mu = jnp.mean(o, axis=-1, keepdims=True)
            var = jnp.mean(jnp.square(o - mu), axis=-1, keepdims=True)
            on = (o - mu) * lax.rsqrt(var + LN_EPS)
            o_ref[rows, :] = (_silu(g_ref[rows, :]) * on).astype(o_ref.dtype)


def _retention(qk, v, g, raw_fwd, raw_bwd, batch, tokens_per_batch, ctx_len):
    n_tok, qk2 = qk.shape
    v_dim = v.shape[1]
    heads = RET_HEADS
    dk = qk2 // 2 // heads
    dv = v_dim // heads
    tb = next(t for t in range(RET_BLOCK, 0, -RET_CHUNK) if tokens_per_batch % t == 0)
    assert tb % RET_CHUNK == 0 and ctx_len % RET_CHUNK == 0 and ctx_len <= tb
    nb = tokens_per_batch // tb
    steps = 2 * (nb + 1)

    def blk(st):
        phase = st // (nb + 1)
        s = st % (nb + 1)
        back = jnp.where(s == 0, 0, nb - s)
        return jnp.where(phase == 0, back, jnp.minimum(s, nb - 1))

    def fwd_blk(st):
        phase = st // (nb + 1)
        s = st % (nb + 1)
        return jnp.where(phase == 0, 0, jnp.minimum(s, nb - 1))

    dec = lambda raw: jnp.broadcast_to(raw.astype(F32)[:, None, None], (heads, 1, LANES))
    return pl.pallas_call(
        functools.partial(_ret_kernel, n_blocks=nb, chunks_per_block=tb // RET_CHUNK,
                          ctx_chunks=ctx_len // RET_CHUNK),
        grid=(batch, heads, steps),
        in_specs=[pl.BlockSpec((tb, dk), lambda b, h, st: (b * nb + blk(st), h)),
                  pl.BlockSpec((tb, dk), lambda b, h, st: (b * nb + blk(st), heads + h)),
                  pl.BlockSpec((tb, dv), lambda b, h, st: (b * nb + blk(st), h)),
                  pl.BlockSpec((tb, dv), lambda b, h, st: (b * nb + fwd_blk(st), h)),
                  pl.BlockSpec((1, 1, LANES), lambda b, h, st: (h, 0, 0)),
                  pl.BlockSpec((1, 1, LANES), lambda b, h, st: (h, 0, 0))],
        out_specs=pl.BlockSpec((tb, dv), lambda b, h, st: (b * nb + fwd_blk(st), h)),
        out_shape=jax.ShapeDtypeStruct((n_tok, v_dim), BF16),
        scratch_shapes=[pltpu.VMEM((dk, dv), F32), pltpu.VMEM((dk, dv), F32),
                        pltpu.VMEM((tokens_per_batch, dv), F32)],
        compiler_params=pltpu.CompilerParams(vmem_limit_bytes=VMEM_LIMIT),
    )(qk, qk, v, g, dec(raw_fwd), dec(raw_bwd))


def _outproj_kernel(o_ref, x_ref, c_ref, w_ref, g1_ref, sc2_ref, sh2_ref, lng_ref, lnb_ref,
                    x1_ref, h2_ref, hp_ref, *, order):
    i = pl.program_id(0)
    g = order.group(i)
    _, _, is_ctx = order.split(i)
    y = jnp.dot(o_ref[...], w_ref[...], preferred_element_type=F32)
    xin = jnp.where(is_ctx, c_ref[...], x_ref[...])
    r = DEEPNORM_ALPHA * xin + g1_ref[0, pl.ds(g, 1), :] * y
    x1 = _layer_norm(r, lng_ref[...], lnb_ref[...])
    x1_ref[...] = x1
    h2 = x1 * (1.0 + sc2_ref[0, pl.ds(g, 1), :]) + sh2_ref[0, pl.ds(g, 1), :]
    h2_ref[...] = h2
    hp_ref[...] = _to_token_rows(h2).astype(BF16)


def _outproj(og, x2d, ctx2d, w_out, mod, ln_g, ln_b, order):
    v_dim, d = w_out.shape
    tm = order.tile
    s_rows = d // LANES
    row = pl.BlockSpec((1, d), lambda i: (0, 0))
    tok = pl.BlockSpec((tm, d), lambda i: (i, 0))
    return pl.pallas_call(
        functools.partial(_outproj_kernel, order=order),
        grid=(order.n_tiles,),
        in_specs=[pl.BlockSpec((tm, v_dim), lambda i: (i, 0)),
                  pl.BlockSpec((tm, d), lambda i: (order.lat_block(i), 0)),
                  pl.BlockSpec((tm, d), lambda i: (order.ctx_block(i), 0)),
                  pl.BlockSpec((v_dim, d), lambda i: (0, 0), pipeline_mode=pl.Buffered(1)),
                  _mod_spec(0, MOD_G1, d, 1), _mod_spec(0, MOD_SC2, d, 1), _mod_spec(0, MOD_SH2, d, 1),
                  row, row],
        out_specs=[tok, tok, pl.BlockSpec((tm * s_rows, LANES), lambda i: (i, 0))],
        out_shape=[jax.ShapeDtypeStruct((order.n_tok, d), F32)] * 2
                  + [jax.ShapeDtypeStruct((order.n_tok * s_rows, LANES), BF16)],
        compiler_params=pltpu.CompilerParams(vmem_limit_bytes=VMEM_LIMIT),
    )(og, x2d, ctx2d, w_out, mod, mod, mod, ln_g.reshape(1, d), ln_b.reshape(1, d))


def _pool_kernel(x_ref, pw_ref, ps_ref, sc1_ref, sh1_ref, g1_ref, sc2_ref, sh2_ref, lng_ref, lnb_ref,
                 x1_ref, h2_ref, hp_ref, y_ref, *, lat_tiles):
    i = pl.program_id(0)
    g = i // lat_tiles
    x = x_ref[...]
    tm, d = x.shape
    gd = d // len(POOL_WINDOWS)
    h = x * (1.0 + sc1_ref[0, pl.ds(g, 1), :]) + sh1_ref[0, pl.ds(g, 1), :]
    ni = lax.broadcasted_iota(jnp.int32, (tm, tm), 0)
    mi = lax.broadcasted_iota(jnp.int32, (tm, tm), 1)
    same_row = (ni // GRID_W) == (mi // GRID_W)
    delta = mi - ni
    col = lax.broadcasted_iota(jnp.int32, (tm, 1), 0) % GRID_W
    for gi, w in enumerate(POOL_WINDOWS):
        window = jnp.where(same_row & (delta >= -(w // 2)) & (delta < w // 2), 1.0, 0.0).astype(BF16)
        count = (jnp.minimum(col + w // 2, GRID_W) - jnp.maximum(col - w // 2, 0)).astype(F32)
        hg = h[:, gi * gd:(gi + 1) * gd]
        h_hi = hg.astype(BF16)
        r1 = hg - h_hi.astype(F32)
        h_mid = r1.astype(BF16)
        h_lo = (r1 - h_mid.astype(F32)).astype(BF16)
        total = (jnp.dot(window, h_hi, preferred_element_type=F32)
                 + jnp.dot(window, h_mid, preferred_element_type=F32)
                 + jnp.dot(window, h_lo, preferred_element_type=F32))
        p = total / count - hg
        y_ref[:, gi * gd:(gi + 1) * gd] = jnp.dot(p.astype(BF16), pw_ref[gi], preferred_element_type=F32)
    y = y_ref[...] * ps_ref[...]
    r = DEEPNORM_ALPHA * x + g1_ref[0, pl.ds(g, 1), :] * y
    x1 = _layer_norm(r, lng_ref[...], lnb_ref[...])
    x1_ref[...] = x1
    h2 = x1 * (1.0 + sc2_ref[0, pl.ds(g, 1), :]) + sh2_ref[0, pl.ds(g, 1), :]
    h2_ref[...] = h2
    hp_ref[...] = _to_token_rows(h2).astype(BF16)


def _pool_layer(x_prev, order0, pool_w, pool_scale, mod, ln_g, ln_b, layer, batch, seq):
    d = x_prev.shape[1]
    tm = order0.tile
    s_rows = d // LANES
    lat_tiles = seq // tm
    n_tiles = batch * lat_tiles
    ng, gd, _ = pool_w.shape

    def src(i):
        b = i // lat_tiles
        return (b * order0.per_batch + order0.ctx_tiles + i % lat_tiles, 0)

    row = pl.BlockSpec((1, d), lambda i: (0, 0))
    tok = pl.BlockSpec((tm, d), lambda i: (i, 0))
    ms = lambda chunk: _mod_spec(layer, chunk, d, 1)
    return pl.pallas_call(
        functools.partial(_pool_kernel, lat_tiles=lat_tiles),
        grid=(n_tiles,),
        in_specs=[pl.BlockSpec((tm, d), src),
                  pl.BlockSpec((ng, gd, gd), lambda i: (0, 0, 0)),
                  row, ms(MOD_SC1), ms(MOD_SH1), ms(MOD_G1), ms(MOD_SC2), ms(MOD_SH2), row, row],
        out_specs=[tok, tok, pl.BlockSpec((tm * s_rows, LANES), lambda i: (i, 0))],
        out_shape=[jax.ShapeDtypeStruct((n_tiles * tm, d), F32)] * 2
                  + [jax.ShapeDtypeStruct((n_tiles * tm * s_rows, LANES), BF16)],
        scratch_shapes=[pltpu.VMEM((tm, d), F32)],
        compiler_params=pltpu.CompilerParams(vmem_limit_bytes=VMEM_LIMIT),
    )(x_prev, pool_w.astype(BF16), pool_scale.reshape(1, d), mod, mod, mod, mod, mod,
      ln_g.reshape(1, d), ln_b.reshape(1, d))


def _split_bf16(x):
    hi = x.astype(BF16)
    return hi, (x - hi.astype(F32)).astype(BF16)


def _route_logits(h_hi, h_mid, w_ref):
    nt = (((1,), (1,)), ((), ()))
    w_hi, w_mid = _split_bf16(w_ref[...])
    return (lax.dot_general(w_hi, h_hi, nt, preferred_element_type=F32)
            + (lax.dot_general(w_hi, h_mid, nt, preferred_element_type=F32)
               + lax.dot_general(w_mid, h_hi, nt, preferred_element_type=F32)))


def _route_tile(logits, b_ref, eidx_ref, gate_ref, rank_ref, cnt_ref, carry_ref):
    i = pl.program_id(0)
    n_exp, tr = logits.shape
    per = n_exp // N_EXPERT_GROUPS
    neg = -jnp.inf

    @pl.when(i == 0)
    def _():
        carry_ref[...] = jnp.zeros_like(carry_ref)

    scores = jax.nn.sigmoid(logits)
    biased = scores + b_ref[...]

    sub_idx = lax.broadcasted_iota(jnp.int32, (per, tr), 0)
    group_score = []
    for gi in range(N_EXPERT_GROUPS):
        sub = biased[gi * per:(gi + 1) * per, :]
        m1 = jnp.max(sub, axis=0, keepdims=True)
        first = jnp.min(jnp.where(sub == m1, sub_idx, per), axis=0, keepdims=True)
        m2 = jnp.max(jnp.where(sub_idx == first, neg, sub), axis=0, keepdims=True)
        group_score.append(m1 + m2)
    masked = []
    for gi in range(N_EXPERT_GROUPS):
        ahead = jnp.zeros((1, tr), jnp.int32)
        for gj in range(N_EXPERT_GROUPS):
            if gj == gi:
                continue
            wins = (group_score[gj] >= group_score[gi]) if gj < gi else (group_score[gj] > group_score[gi])
            ahead = ahead + wins.astype(jnp.int32)
        keep = ahead < TOPK_GROUPS
        masked.append(jnp.where(keep, biased[gi * per:(gi + 1) * per, :], neg))
    masked = jnp.concatenate(masked, axis=0)

    eid = lax.broadcasted_iota(jnp.int32, (n_exp, tr), 0)
    picks, gates = [], []
    onehot = jnp.zeros((n_exp, tr), F32)
    for _ in range(TOP_K):
        m = jnp.max(masked, axis=0, keepdims=True)
        first = jnp.min(jnp.where(masked == m, eid, n_exp), axis=0, keepdims=True)
        sel = eid == first
        gates.append(jnp.sum(jnp.where(sel, scores, 0.0), axis=0, keepdims=True))
        picks.append(first)
        masked = jnp.where(sel, neg, masked)
        onehot = onehot + sel.astype(F32)
    gate_sum = gates[0]
    for gk in gates[1:]:
        gate_sum = gate_sum + gk

    si = lax.broadcasted_iota(jnp.int32, (tr, tr), 0)
    ti = lax.broadcasted_iota(jnp.int32, (tr, tr), 1)
    before = jnp.where(si < ti, 1.0, 0.0).astype(BF16)
    prefix = jnp.dot(onehot.astype(BF16), before, preferred_element_type=F32) + carry_ref[:, :1]
    for k in range(TOP_K):
        eidx_ref[k:k + 1, :] = picks[k]
        gate_ref[k:k + 1, :] = gates[k] / gate_sum * ROUTED_SCALE
        rank_ref[k:k + 1, :] = jnp.sum(jnp.where(eid == picks[k], prefix, 0.0), axis=0,
                                       keepdims=True).astype(jnp.int32)
    carry_ref[...] = carry_ref[...] + jnp.sum(onehot, axis=1, keepdims=True)
    cnt_ref[...] = carry_ref[...]


def _router_kernel(h_ref, w_ref, b_ref, sg_ref, su_ref, sd_ref, eidx_ref, gate_ref, rank_ref, cnt_ref, fsh_ref,
                   carry_ref):
    hb, h_mid = _split_bf16(h_ref[...])
    logits = _route_logits(hb, h_mid, w_ref)
    a = (_silu(jnp.dot(hb, sg_ref[...], preferred_element_type=F32))
         * jnp.dot(hb, su_ref[...], preferred_element_type=F32))
    fsh_ref[...] = jnp.dot(a.astype(BF16), sd_ref[...], preferred_element_type=F32)
    _route_tile(logits, b_ref, eidx_ref, gate_ref, rank_ref, cnt_ref, carry_ref)


def _router(h2, router_w, router_b, sg, su, sd):
    n_tok, d = h2.shape
    n_exp = router_w.shape[1]
    ds_ = sg.shape[1]
    tr = ROUTER_TILE
    assert n_tok % tr == 0
    sel = pl.BlockSpec((TOP_K, tr), lambda i: (0, i))
    tok = pl.BlockSpec((tr, d), lambda i: (i, 0))
    return pl.pallas_call(
        _router_kernel,
        grid=(n_tok // tr,),
        in_specs=[tok,
                  pl.BlockSpec((n_exp, d), lambda i: (0, 0)),
                  pl.BlockSpec((n_exp, 1), lambda i: (0, 0)),
                  pl.BlockSpec((d, ds_), lambda i: (0, 0)),
                  pl.BlockSpec((d, ds_), lambda i: (0, 0)),
                  pl.BlockSpec((ds_, d), lambda i: (0, 0))],
        out_specs=[sel, sel, sel, pl.BlockSpec((n_exp, LANES), lambda i: (0, 0)), tok],
        out_shape=[jax.ShapeDtypeStruct((TOP_K, n_tok), jnp.int32),
                   jax.ShapeDtypeStruct((TOP_K, n_tok), F32),
                   jax.ShapeDtypeStruct((TOP_K, n_tok), jnp.int32),
                   jax.ShapeDtypeStruct((n_exp, LANES), F32),
                   jax.ShapeDtypeStruct((n_tok, d), F32)],
        scratch_shapes=[pltpu.VMEM((n_exp, LANES), F32)],
        compiler_params=pltpu.CompilerParams(vmem_limit_bytes=VMEM_LIMIT),
    )(h2, router_w.astype(F32).T, router_b.astype(F32).reshape(n_exp, 1), sg, su, sd)


def _dispatch_kernel(pe_ref, pd_ref, nu_ref, slot_hbm, hp_ref, xs_hbm, idx_ref, zero_ref, isem, zsem, rsem, *,
                     n_steps, n_exp, n_blocks, s_rows, bm):
    i = pl.program_id(0)
    tt = hp_ref.shape[0] // s_rows
    zrows = bm * s_rows
    per_tile = tt * TOP_K

    def idx_copy(tile, buf):
        return pltpu.make_async_copy(slot_hbm.at[tile], idx_ref.at[pl.ds(buf * per_tile, per_tile)], isem.at[buf])

    def zero_copy(row0):
        return pltpu.make_async_copy(zero_ref, xs_hbm.at[pl.ds(pl.multiple_of(row0, s_rows), zrows), :], zsem)

    @pl.when(i == 0)
    def _():
        zero_ref[...] = jnp.zeros_like(zero_ref)
        nu = nu_ref[0]

        def fill(fn):
            def per_expert(e, carry):
                @pl.when(pd_ref[e] > 0)
                def _():
                    fn(zero_copy((pe_ref[e] - bm) * s_rows))
                return carry
            lax.fori_loop(0, n_exp, per_expert, 0)

            def per_tail(b, carry):
                fn(zero_copy(b * zrows))
                return carry
            lax.fori_loop(nu, n_blocks, per_tail, 0)

        fill(lambda cp: cp.start())
        fill(lambda cp: cp.wait())
        idx_copy(0, 0).start()
        idx_copy(0, 0).wait()
        if n_steps > 1:
            idx_copy(1, 1).start()

    buf = i % 2

    @pl.when(i > 0)
    def _():
        idx_copy(i, buf).wait()

    group = 8
    assert tt % group == 0

    def body(gi, carry):
        base = pl.multiple_of(gi * (group * s_rows), group * s_rows)
        ibase = buf * per_tile + gi * (group * TOP_K)
        for u in range(group):
            src = hp_ref.at[pl.ds(base + u * s_rows, s_rows), :]
            for k in range(TOP_K):
                t = idx_ref[ibase + (u * TOP_K + k)]
                pltpu.make_async_copy(src, xs_hbm.at[pl.ds(pl.multiple_of(t * s_rows, s_rows), s_rows), :],
                                      rsem).start()
        return carry

    lax.fori_loop(0, tt // group, body, 0)

    @pl.when((i > 0) & (i + 1 < n_steps))
    def _():
        idx_copy(i + 1, 1 - buf).start()

    for _ in range(TOP_K):
        pltpu.make_async_copy(hp_ref, xs_hbm.at[pl.ds(0, tt * s_rows), :], rsem).wait()


def _dispatch(hp, slots, pad_end, padded, n_used, n_slots, bm):
    rows, _ = hp.shape
    n_steps, per_tile = slots.shape
    tt = per_tile // TOP_K
    s_rows = rows // (n_steps * tt)
    n_exp = pad_end.shape[0]
    grid_spec = pltpu.PrefetchScalarGridSpec(
        num_scalar_prefetch=3,
        grid=(n_steps,),
        in_specs=[pl.BlockSpec(memory_space=pl.ANY),
                  pl.BlockSpec((tt * s_rows, LANES), lambda i, pe, pd, nu: (i, 0))],
        out_specs=pl.BlockSpec(memory_space=pl.ANY),
        scratch_shapes=[pltpu.SMEM((2 * per_tile,), jnp.int32),
                        pltpu.VMEM((bm * s_rows, LANES), BF16),
                        pltpu.SemaphoreType.DMA((2,)),
                        pltpu.SemaphoreType.DMA(()),
                        pltpu.SemaphoreType.DMA(())])
    return pl.pallas_call(
        functools.partial(_dispatch_kernel, n_steps=n_steps, n_exp=n_exp, n_blocks=n_slots // bm,
                          s_rows=s_rows, bm=bm),
        grid_spec=grid_spec,
        out_shape=jax.ShapeDtypeStruct((n_slots * s_rows, LANES), BF16),
        compiler_params=pltpu.CompilerParams(vmem_limit_bytes=VMEM_LIMIT, disable_bounds_checks=True),
    )(pad_end, padded, n_used, slots, hp)


def _moe_kernel(be_ref, nu_ref, vb_ref, ne_ref, x_ref, wg_hbm, wu_hbm, wd_hbm, y_ref, wgb, wub, wdb,
                wg_stage, wu_stage, wd_stage, wsem, *, s_rows, layer, n_exp):
    i = pl.program_id(0)
    nu = nu_ref[0]
    parts = MOE_PARTS
    pm = x_ref.shape[0] // s_rows // parts
    rows_of = [slice(p * pm * s_rows, (p + 1) * pm * s_rows) for p in range(parts)]

    def experts(active):
        hidden = []
        for p in active:
            xt = _from_token_rows(x_ref[rows_of[p], :].astype(F32), s_rows)
            gate = up = None
            for c in range(s_rows // 2):
                xc = jnp.concatenate([xt[2 * c], xt[2 * c + 1]], axis=1).astype(BF16)
                rows = slice(2 * c * LANES, (2 * c + 2) * LANES)
                gc = jnp.dot(xc, wgb[rows, :], preferred_element_type=F32)
                uc = jnp.dot(xc, wub[rows, :], preferred_element_type=F32)
                gate = gc if gate is None else gate + gc
                up = uc if up is None else up + uc
            hidden.append((gate, up))
        outs = []
        for gate, up in hidden:
            a = (_silu(gate) * up).astype(BF16)
            outs.append(jnp.dot(a, wdb[...], preferred_element_type=F32))
        for p, y in zip(active, outs):
            y_ref[rows_of[p], :] = _to_token_rows(y).astype(BF16)
        for p in range(parts):
            if p not in active:
                y_ref[rows_of[p], :] = jnp.zeros((pm * s_rows, LANES), BF16)

    def weight_copies(e):
        return [pltpu.make_async_copy(src.at[layer, e], dst, wsem)
                for src, dst in ((wg_hbm, wg_stage), (wu_hbm, wu_stage), (wd_hbm, wd_stage))]

    @pl.when(i < nu)
    def _():
        e = be_ref[i]

        @pl.when(i == 0)
        def _():
            for cp in weight_copies(e):
                cp.start()

        @pl.when((i == 0) | (e != be_ref[jnp.maximum(i - 1, 0)]))
        def _():
            for cp in weight_copies(e):
                cp.wait()
            wgb[...] = wg_stage[...].astype(BF16)
            wub[...] = wu_stage[...].astype(BF16)
            wdb[...] = wd_stage[...].astype(BF16)
            nxt = ne_ref[e]

            @pl.when(nxt < n_exp)
            def _():
                for cp in weight_copies(nxt):
                    cp.start()

        valid = vb_ref[i]
        for n_active in range(1, parts + 1):
            above = (valid > (n_active - 1) * pm) if n_active > 1 else True
            within = (valid <= n_active * pm) if n_active < parts else True

            @pl.when(above & within)
            def _():
                experts(list(range(n_active)))

    @pl.when(i >= nu)
    def _():
        y_ref[...] = jnp.zeros_like(y_ref)


def _moe_experts(xs, blk_expert, n_used, blk_valid, next_expert, wg, wu, wd, layer, bm):
    _, n_exp, d, de = wg.shape
    s_rows = d // LANES
    nb = xs.shape[0] // (bm * s_rows)
    hbm = pl.BlockSpec(memory_space=pl.ANY)
    grid_spec = pltpu.PrefetchScalarGridSpec(
        num_scalar_prefetch=4,
        grid=(nb,),
        in_specs=[pl.BlockSpec((bm * s_rows, LANES), lambda i, be, nu, vb, ne: (jnp.minimum(i, nu[0] - 1), 0)),
                  hbm, hbm, hbm],
        out_specs=pl.BlockSpec((bm * s_rows, LANES), lambda i, be, nu, vb, ne: (i, 0)),
        scratch_shapes=[pltpu.VMEM((d, de), BF16), pltpu.VMEM((d, de), BF16), pltpu.VMEM((de, d), BF16),
                        pltpu.VMEM((d, de), F32), pltpu.VMEM((d, de), F32), pltpu.VMEM((de, d), F32),
                        pltpu.SemaphoreType.DMA(())])
    return pl.pallas_call(
        functools.partial(_moe_kernel, s_rows=s_rows, layer=layer, n_exp=n_exp),
        grid_spec=grid_spec,
        out_shape=jax.ShapeDtypeStruct(xs.shape, BF16),
        compiler_params=pltpu.CompilerParams(vmem_limit_bytes=VMEM_LIMIT),
    )(blk_expert, n_used, blk_valid, next_expert, xs, wg, wu, wd)


def _combine_kernel(slot_hbm, y_hbm, gate_ref, fsh_ref, x1_ref, g2_ref, lng_ref, lnb_ref,
                    out_ref, idx_ref, ybuf, frow_ref, isem, rsem, *, order, n_steps, s_rows):
    i = pl.program_id(0)
    tt, d = x1_ref.shape
    n_rows = TOP_K * tt

    def idx_copy(tile, buf):
        return pltpu.make_async_copy(slot_hbm.at[tile], idx_ref.at[pl.ds(buf * n_rows, n_rows)], isem.at[buf])

    group = 8
    assert tt % group == 0

    def issue_token(buf, r0, u):
        base = pl.multiple_of(r0 * s_rows, group * s_rows)
        ibase = buf * n_rows + r0
        for k in range(TOP_K):
            t = idx_ref[ibase + (k * tt + u)]
            pltpu.make_async_copy(y_hbm.at[pl.ds(pl.multiple_of(t * s_rows, s_rows), s_rows), :],
                                  ybuf.at[buf, pl.ds(base + (k * tt + u) * s_rows, s_rows), :],
                                  rsem.at[buf]).start()

    @pl.when(i == 0)
    def _():
        idx_copy(0, 0).start()
        idx_copy(0, 0).wait()

        def first(gi, carry):
            for u in range(group):
                issue_token(0, gi * group, u)
            return carry
        lax.fori_loop(0, tt // group, first, 0)
        if n_steps > 1:
            idx_copy(1, 1).start()

    buf = i % 2
    nxt = 1 - buf
    has_next = i + 1 < n_steps

    @pl.when(has_next)
    def _():
        idx_copy(i + 1, nxt).wait()

    def token_loop(cur, prefetch):
        pltpu.make_async_copy(y_hbm.at[pl.ds(0, n_rows * s_rows), :], ybuf.at[cur], rsem.at[cur]).wait()

        def body(gi, carry):
            r0 = gi * group
            base = pl.multiple_of(r0 * s_rows, group * s_rows)
            gbase = pl.multiple_of(r0 * TOP_K, group * TOP_K)
            for u in range(group):
                if prefetch:
                    issue_token(1 - cur, r0, u)
                gates = gate_ref[pl.ds(gbase + u * TOP_K, TOP_K), :]
                acc = jnp.zeros((s_rows, LANES), F32)
                for k in range(TOP_K):
                    rows = pl.ds(base + (k * tt + u) * s_rows, s_rows)
                    acc = acc + jnp.broadcast_to(gates[k:k + 1, :], (s_rows, LANES)) * ybuf[cur, rows, :].astype(F32)
                frow_ref[pl.ds(base + u * s_rows, s_rows), :] = acc
            return carry
        lax.fori_loop(0, tt // group, body, 0)

    for cur in (0, 1):
        @pl.when((buf == cur) & has_next)
        def _():
            token_loop(cur, True)

        @pl.when((buf == cur) & jnp.logical_not(has_next))
        def _():
            token_loop(cur, False)

    @pl.when(i + 2 < n_steps)
    def _():
        idx_copy(i + 2, buf).start()

    routed = _from_token_rows(frow_ref[...], s_rows)
    f = fsh_ref[...] + jnp.concatenate([routed[s] for s in range(s_rows)], axis=1)
    g = order.group(i)
    r = DEEPNORM_ALPHA * x1_ref[...] + g2_ref[0, pl.ds(g, 1), :] * f
    out_ref[...] = _layer_norm(r, lng_ref[...], lnb_ref[...])


def _combine(slots, y_sorted, gate, fsh, x1, mod, ln_g, ln_b, layer, order):
    n_tok, d = x1.shape
    tt = order.tile
    n_steps = order.n_tiles
    s_rows = d // LANES
    row = pl.BlockSpec((1, d), lambda i: (0, 0))
    tok = pl.BlockSpec((tt, d), lambda i: (i, 0))
    return pl.pallas_call(
        functools.partial(_combine_kernel, order=order, n_steps=n_steps, s_rows=s_rows),
        grid=(n_steps,),
        in_specs=[pl.BlockSpec(memory_space=pl.ANY),
                  pl.BlockSpec(memory_space=pl.ANY),
                  pl.BlockSpec((tt * TOP_K, LANES), lambda i: (i, 0)),
                  tok, tok,
                  _mod_spec(layer, MOD_G2, d, 1), row, row],
        out_specs=tok,
        out_shape=jax.ShapeDtypeStruct((n_tok, d), F32),
        scratch_shapes=[pltpu.SMEM((2 * TOP_K * tt,), jnp.int32),
                        pltpu.VMEM((2, TOP_K * tt * s_rows, LANES), BF16),
                        pltpu.VMEM((tt * s_rows, LANES), F32),
                        pltpu.SemaphoreType.DMA((2,)),
                        pltpu.SemaphoreType.DMA((2,))],
        compiler_params=pltpu.CompilerParams(vmem_limit_bytes=VMEM_LIMIT, disable_bounds_checks=True),
    )(slots, y_sorted, gate, fsh, x1, mod, ln_g.reshape(1, d), ln_b.reshape(1, d))


def _moe_layer(produced, layer, mod, order, router_w, router_b, wg, wu, wd, sg, su, sd, ln_g, ln_b):
    x1, h2, hp = produced
    n_tok, d = h2.shape
    n_exp = router_w.shape[1]
    bm = MOE_BLOCK
    tt = order.tile
    eidx_t, gate_t, rank_t, cnt, fsh = _router(h2, router_w, router_b, sg[layer].astype(BF16),
                                               su[layer].astype(BF16), sd[layer].astype(BF16))

    counts = cnt[:, 0].astype(jnp.int32)
    padded = (counts + bm - 1) // bm * bm
    pad_end = jnp.cumsum(padded).astype(jnp.int32)
    pad_start = pad_end - padded
    experts = jnp.arange(n_exp, dtype=jnp.int32)
    slot_t = rank_t + jnp.sum(jnp.where(eidx_t[:, :, None] == experts, pad_start, 0), axis=-1)
    n_slots = n_tok * TOP_K + n_exp * bm
    nb = n_slots // bm
    n_used = pad_end[-1:] // bm
    blk_expert = jnp.minimum(jnp.sum(pad_end[None, :] <= (jnp.arange(nb, dtype=jnp.int32) * bm)[:, None], axis=1),
                             n_exp - 1).astype(jnp.int32)
    of_block = blk_expert[:, None] == experts
    blk_first = jnp.sum(jnp.where(of_block, pad_start, 0), axis=1) // bm
    blk_valid = jnp.clip(jnp.sum(jnp.where(of_block, counts, 0), axis=1)
                         - (jnp.arange(nb, dtype=jnp.int32) - blk_first) * bm, 0, bm).astype(jnp.int32)
    later = (experts[None, :] > experts[:, None]) & (counts[None, :] > 0)
    next_expert = jnp.min(jnp.where(later, experts[None, :], n_exp), axis=1).astype(jnp.int32)
    n_tiles = n_tok // tt
    slots_tok = slot_t.T.reshape(n_tiles, tt * TOP_K)
    slots_k = slot_t.reshape(TOP_K, n_tiles, tt).transpose(1, 0, 2).reshape(n_tiles, TOP_K * tt)

    xs = _dispatch(hp, slots_tok, pad_end, padded, n_used, n_slots, bm)
    ys = _moe_experts(xs, blk_expert, n_used, blk_valid, next_expert, wg, wu, wd, layer, bm)
    gates = jnp.broadcast_to(gate_t.T.reshape(n_tok * TOP_K, 1), (n_tok * TOP_K, LANES))
    return _combine(slots_k, ys, gates, fsh, x1, mod, ln_g[layer], ln_b[layer], layer, order)


def kernel(x, c, ctx, c_ctx, ada_w, ada_b, ln_mix_g, ln_mix_b, ln_ffn_g, ln_ffn_b, ret_w_in, ret_w_out,
           ret_decay_fwd, ret_decay_bwd, pool_w, pool_scale, router_w, router_b, exp_w_gate, exp_w_up,
           exp_w_down, sh_w_gate, sh_w_up, sh_w_down):
    batch, seq, d = x.shape
    ctx_len = ctx.shape[1]
    heads = RET_HEADS
    qk_dim = d
    head_dim = qk_dim // heads
    v_dim = ret_w_out.shape[1]
    assert ada_w.shape[0] == DEPTH == 2 and batch + 1 <= MOD_ROWS

    cvec = jnp.zeros((MOD_ROWS, d), F32).at[:batch].set(c.astype(F32)).at[batch].set(c_ctx.astype(F32))
    mod = _modulation(cvec, ada_w.astype(F32), ada_b.astype(F32))

    x2d = x.reshape(batch * seq, d)
    ctx2d = ctx.reshape(batch * ctx_len, d)
    order0 = _Order(batch, seq, ctx_len, TOKEN_TILE)

    w_in = ret_w_in[0].astype(F32)
    tabs = _rope_tables(seq, head_dim, TOKEN_TILE)
    qk = _inproj(x2d, ctx2d, mod, w_in, 0, 2 * qk_dim, order0, BF16, rope_tabs=tabs, k_col0=qk_dim,
                 head_dim=head_dim)
    v = _inproj(x2d, ctx2d, mod, w_in, 2 * qk_dim, v_dim, order0, BF16)
    g = _inproj(x2d, ctx2d, mod, w_in, 2 * qk_dim + v_dim, v_dim, order0, F32)
    og = _retention(qk, v, g, ret_decay_fwd[0], ret_decay_bwd[0], batch, ctx_len + seq, ctx_len)
    produced = _outproj(og, x2d, ctx2d, ret_w_out[0].astype(BF16), mod, ln_mix_g[0], ln_mix_b[0], order0)
    xa = _moe_layer(produced, 0, mod, order0, router_w[0], router_b[0], exp_w_gate, exp_w_up, exp_w_down,
                    sh_w_gate, sh_w_up, sh_w_down, ln_ffn_g, ln_ffn_b)

    order1 = _Order(batch, seq, 0, TOKEN_TILE)
    produced = _pool_layer(xa, order0, pool_w[0], pool_scale[0], mod, ln_mix_g[1], ln_mix_b[1], 1, batch, seq)
    xb = _moe_layer(produced, 1, mod, order1, router_w[1], router_b[1], exp_w_gate, exp_w_up, exp_w_down,
                    sh_w_gate, sh_w_up, sh_w_down, ln_ffn_g, ln_ffn_b)
    return xb.reshape(batch, seq, d)
```

```python
import functools

import jax
import jax.numpy as jnp
from jax import lax
from jax.experimental import pallas as pl
from jax.experimental.pallas import tpu as pltpu

F32 = jnp.float32
BF16 = jnp.bfloat16
HIGHEST = lax.Precision.HIGHEST

DEPTH = 2
GRID_W = 64
RET_HEADS = 8
ROPE_BASE = 10000.0
POOL_WINDOWS = (2, 4, 8, 16)
N_EXPERT_GROUPS = 8
TOPK_GROUPS = 4
TOP_K = 8
ROUTED_SCALE = 2.5
LN_EPS = 1e-5
N_MOD = 6
DEEPNORM_ALPHA = (2.0 * DEPTH) ** 0.25
MOD_SH1, MOD_SC1, MOD_G1, MOD_SH2, MOD_SC2, MOD_G2 = range(N_MOD)

LANES = 128
TOKEN_TILE = 256
RET_BLOCK = 2816
RET_CHUNK = 256
ROUTER_TILE = 512
DISPATCH_TILE = 2048
MOE_BLOCK = 512
MOE_PARTS = 2
MOD_ROWS = 8
VMEM_LIMIT = 56 * 1024 * 1024


def _silu(x):
    return x * jax.nn.sigmoid(x)


def _layer_norm(r, g, b):
    mu = jnp.mean(r, axis=-1, keepdims=True)
    var = jnp.mean(jnp.square(r - mu), axis=-1, keepdims=True)
    return (r - mu) * lax.rsqrt(var + LN_EPS) * g + b


def _to_token_rows(h):
    tm, d = h.shape
    s_rows = d // LANES
    blocks = jnp.stack([h[:, s * LANES:(s + 1) * LANES] for s in range(s_rows)], axis=0)
    return jnp.swapaxes(blocks, 0, 1).reshape(tm * s_rows, LANES)


def _from_token_rows(rows, s_rows):
    return jnp.swapaxes(rows.reshape(rows.shape[0] // s_rows, s_rows, LANES), 0, 1)


class _Order:
    def __init__(self, batch, seq, ctx_len, tile):
        assert seq % tile == 0 and ctx_len % tile == 0
        self.batch, self.seq, self.ctx_len, self.tile = batch, seq, ctx_len, tile
        self.ctx_tiles = ctx_len // tile
        self.lat_tiles = seq // tile
        self.per_batch = self.ctx_tiles + self.lat_tiles
        self.n_tiles = batch * self.per_batch
        self.n_tok = self.n_tiles * tile

    def split(self, i):
        b = i // self.per_batch
        r = i % self.per_batch
        return b, r, r < self.ctx_tiles

    def group(self, i):
        b, _, is_ctx = self.split(i)
        return jnp.where(is_ctx, self.batch, b)

    def lat_block(self, i):
        b, r, _ = self.split(i)
        return b * self.lat_tiles + jnp.maximum(r - self.ctx_tiles, 0)

    def ctx_block(self, i):
        b, r, _ = self.split(i)
        return b * self.ctx_tiles + jnp.minimum(r, max(self.ctx_tiles - 1, 0))


def _mod_kernel(c_ref, w_ref, b_ref, o_ref):
    s = _silu(c_ref[...])
    o_ref[0] = jnp.dot(s, w_ref[0], precision=HIGHEST, preferred_element_type=F32) + b_ref[0]


def _modulation(cvec, ada_w, ada_b):
    depth, d, n = ada_w.shape
    tn = 2048 if n % 2048 == 0 else 1024
    return pl.pallas_call(
        _mod_kernel,
        grid=(depth, n // tn),
        in_specs=[pl.BlockSpec((MOD_ROWS, d), lambda l, j: (0, 0)),
                  pl.BlockSpec((1, d, tn), lambda l, j: (l, 0, j)),
                  pl.BlockSpec((1, 1, tn), lambda l, j: (l, 0, j))],
        out_specs=pl.BlockSpec((1, MOD_ROWS, tn), lambda l, j: (l, 0, j)),
        out_shape=jax.ShapeDtypeStruct((depth, MOD_ROWS, n), F32),
        compiler_params=pltpu.CompilerParams(vmem_limit_bytes=VMEM_LIMIT),
    )(cvec, ada_w, ada_b.reshape(depth, 1, n))


def _mod_spec(layer, chunk, d, n_grid_args):
    if n_grid_args == 1:
        return pl.BlockSpec((1, MOD_ROWS, d), lambda i: (layer, 0, chunk))
    return pl.BlockSpec((1, MOD_ROWS, d), lambda j, i: (layer, 0, chunk))


def _inproj_kernel(x_ref, c_ref, sc_ref, sh_ref, w_ref, *rest, order, rope, k_col0, head_dim):
    i = pl.program_id(1)
    j = pl.program_id(0)
    if rope:
        cos_ref, sin_ref, o_ref, wb_ref = rest
    else:
        o_ref, wb_ref = rest

    @pl.when(i == 0)
    def _():
        wb_ref[...] = w_ref[...].astype(BF16)

    g = order.group(i)
    _, _, is_ctx = order.split(i)
    sc = sc_ref[0, pl.ds(g, 1), :]
    sh = sh_ref[0, pl.ds(g, 1), :]
    xin = jnp.where(is_ctx, c_ref[...], x_ref[...])
    a = (xin * (1.0 + sc) + sh).astype(BF16)
    acc = jnp.dot(a, wb_ref[...], preferred_element_type=F32)
    if not rope:
        o_ref[...] = acc.astype(o_ref.dtype)
        return
    tn = acc.shape[1]
    lane = lax.broadcasted_iota(jnp.int32, acc.shape, 1)
    partner = jnp.where((lane & 1) == 0, pltpu.roll(acc, tn - 1, 1), pltpu.roll(acc, 1, 1))
    cos = cos_ref[...]
    sin = sin_ref[...]
    scale = jnp.where(j * tn >= k_col0, head_dim ** -0.5, 1.0).astype(F32)
    for h in range(tn // head_dim):
        sl = slice(h * head_dim, (h + 1) * head_dim)
        o_ref[:, sl] = ((acc[:, sl] * cos + partner[:, sl] * sin) * scale).astype(o_ref.dtype)


def _inproj(x2d, ctx2d, mod, w_in, col0, n, order, out_dtype, rope_tabs=None, k_col0=0, head_dim=0):
    d = w_in.shape[0]
    tm = order.tile
    tn = min(n, 2048)
    assert col0 % tn == 0 and n % tn == 0
    rope = rope_tabs is not None
    in_specs = [pl.BlockSpec((tm, d), lambda j, i: (order.lat_block(i), 0)),
                pl.BlockSpec((tm, d), lambda j, i: (order.ctx_block(i), 0)),
                _mod_spec(0, MOD_SC1, d, 2), _mod_spec(0, MOD_SH1, d, 2),
                pl.BlockSpec((d, tn), lambda j, i: (0, col0 // tn + j), pipeline_mode=pl.Buffered(1))]
    args = [x2d, ctx2d, mod, mod, w_in]
    if rope:
        def tab_map(j, i):
            _, r, is_ctx = order.split(i)
            return (jnp.where(is_ctx, order.lat_tiles, r - order.ctx_tiles), 0)
        in_specs += [pl.BlockSpec((tm, head_dim), tab_map), pl.BlockSpec((tm, head_dim), tab_map)]
        args += list(rope_tabs)
    return pl.pallas_call(
        functools.partial(_inproj_kernel, order=order, rope=rope, k_col0=k_col0, head_dim=head_dim),
        grid=(n // tn, order.n_tiles),
        in_specs=in_specs,
        out_specs=pl.BlockSpec((tm, tn), lambda j, i: (i, j)),
        out_shape=jax.ShapeDtypeStruct((order.n_tok, n), out_dtype),
        scratch_shapes=[pltpu.VMEM((d, tn), BF16)],
        compiler_params=pltpu.CompilerParams(vmem_limit_bytes=VMEM_LIMIT),
    )(*args)


def _rope_tables(seq, head_dim, tile):
    n_freq = head_dim // 4
    pos = jnp.arange(seq, dtype=jnp.int32)
    row = (pos // GRID_W).astype(F32)
    col = (pos % GRID_W).astype(F32)
    freqs = ROPE_BASE ** (-jnp.arange(n_freq, dtype=F32) / n_freq)
    ang = jnp.concatenate([row[:, None] * freqs, col[:, None] * freqs], axis=-1)
    cos = jnp.repeat(jnp.cos(ang), 2, axis=-1)
    sin = jnp.stack([-jnp.sin(ang), jnp.sin(ang)], axis=-1).reshape(seq, head_dim)
    ident = jnp.zeros((tile, head_dim), F32)
    return jnp.concatenate([cos, ident + 1.0], axis=0), jnp.concatenate([sin, ident], axis=0)


def _ret_kernel(q_ref, k_ref, v_ref, g_ref, df_ref, db_ref, o_ref, sf_ref, sb_ref, obuf_ref, *,
                n_blocks, chunks_per_block, ctx_chunks):
    C = RET_CHUNK
    NB, CPB, NCC = n_blocks, chunks_per_block, ctx_chunks
    st = pl.program_id(2)
    phase = st // (NB + 1)
    s = st % (NB + 1)
    ldf = jax.nn.log_sigmoid(df_ref[0])[:, :1]
    ldb = jax.nn.log_sigmoid(db_ref[0])[:, :1]
    pos = lax.broadcasted_iota(jnp.int32, (C, 1), 0).astype(F32)
    tdims = (((0,), (0,)), ((), ()))

    @pl.when(phase == 0)
    def _backward_states():
        @pl.when(s == 0)
        def _():
            sb_ref[...] = jnp.zeros_like(sb_ref)

        q_dec = jnp.exp(ldb * (C - pos))
        k_dec = jnp.exp(ldb * pos)
        chunk_dec = jnp.exp(ldb * C)

        def chunk(blk, c):
            rows = pl.ds(c * C, C)
            q = q_ref[rows, :].astype(F32)
            k = k_ref[rows, :].astype(F32)
            v = v_ref[rows, :]
            state = sb_ref[...]
            goff = pl.multiple_of((blk * CPB + c) * C, C)
            obuf_ref[pl.ds(goff, C), :] = jnp.dot((q * q_dec).astype(BF16), state.astype(BF16),
                                                  preferred_element_type=F32)
            sb_ref[...] = state * chunk_dec + lax.dot_general((k * k_dec).astype(BF16), v, tdims,
                                                              preferred_element_type=F32)

        @pl.when(s == 0)
        def _():
            for c in reversed(range(NCC)):
                chunk(0, c)

        @pl.when((s > 0) & (s < NB))
        def _():
            for c in reversed(range(CPB)):
                chunk(NB - s, c)

        @pl.when(s == NB)
        def _():
            for c in reversed(range(NCC, CPB)):
                chunk(0, c)

    @pl.when((phase == 1) & (s < NB))
    def _forward_and_output():
        @pl.when(s == 0)
        def _():
            sf_ref[...] = jnp.zeros_like(sf_ref)

        q_dec = jnp.exp(ldf * (pos + 1.0))
        k_dec = jnp.exp(ldf * (C - 1.0 - pos))
        chunk_dec = jnp.exp(ldf * C)
        ii = lax.broadcasted_iota(jnp.int32, (C, C), 0)
        jj = lax.broadcasted_iota(jnp.int32, (C, C), 1)
        rel = (ii - jj).astype(F32)
        decay = jnp.where(rel >= 0, jnp.exp(ldf * jnp.maximum(rel, 0.0)), jnp.exp(ldb * jnp.maximum(-rel, 0.0)))

        for c in range(CPB):
            rows = pl.ds(c * C, C)
            qb = q_ref[rows, :]
            kb = k_ref[rows, :]
            v = v_ref[rows, :]
            q = qb.astype(F32)
            k = kb.astype(F32)
            state = sf_ref[...]
            scores = lax.dot_general(qb, kb, (((1,), (1,)), ((), ())), preferred_element_type=F32) * decay
            goff = pl.multiple_of((s * CPB + c) * C, C)
            o = (jnp.dot(scores.astype(BF16), v, preferred_element_type=F32)
                 + jnp.dot((q * q_dec).astype(BF16), state.astype(BF16), preferred_element_type=F32)
                 + obuf_ref[pl.ds(goff, C), :])
            sf_ref[...] = state * chunk_dec + lax.dot_general((k * k_dec).astype(BF16), v, tdims,
                                                              preferred_element_type=F32)
            mu = jnp.mean(o, axis=-1, keepdims=True)
            var = jnp.mean(jnp.square(o - mu), axis=-1, keepdims=True)
            on = (o - mu) * lax.rsqrt(var + LN_EPS)
            o_ref[rows, :] = (_silu(g_ref[rows, :]) * on).astype(o_ref.dtype)


def _retention(qk, v, g, raw_fwd, raw_bwd, batch, tokens_per_batch, ctx_len):
    n_tok, qk2 = qk.shape
    v_dim = v.shape[1]
    heads = RET_HEADS
    dk = qk2 // 2 // heads
    dv = v_dim // heads
    tb = next(t for t in range(RET_BLOCK, 0, -RET_CHUNK) if tokens_per_batch % t == 0)
    assert tb % RET_CHUNK == 0 and ctx_len % RET_CHUNK == 0 and ctx_len <= tb
    nb = tokens_per_batch // tb
    steps = 2 * (nb + 1)

    def blk(st):
        phase = st // (nb + 1)
        s = st % (nb + 1)
        back = jnp.where(s == 0, 0, nb - s)
        return jnp.where(phase == 0, back, jnp.minimum(s, nb - 1))

    def fwd_blk(st):
        phase = st // (nb + 1)
        s = st % (nb + 1)
        return jnp.where(phase == 0, 0, jnp.minimum(s, nb - 1))

    dec = lambda raw: jnp.broadcast_to(raw.astype(F32)[:, None, None], (heads, 1, LANES))
    return pl.pallas_call(
        functools.partial(_ret_kernel, n_blocks=nb, chunks_per_block=tb // RET_CHUNK,
                          ctx_chunks=ctx_len // RET_CHUNK),
        grid=(batch, heads, steps),
        in_specs=[pl.BlockSpec((tb, dk), lambda b, h, st: (b * nb + blk(st), h)),
                  pl.BlockSpec((tb, dk), lambda b, h, st: (b * nb + blk(st), heads + h)),
                  pl.BlockSpec((tb, dv), lambda b, h, st: (b * nb + blk(st), h)),
                  pl.BlockSpec((tb, dv), lambda b, h, st: (b * nb + fwd_blk(st), h)),
                  pl.BlockSpec((1, 1, LANES), lambda b, h, st: (h, 0, 0)),
                  pl.BlockSpec((1, 1, LANES), lambda b, h, st: (h, 0, 0))],
        out_specs=pl.BlockSpec((tb, dv), lambda b, h, st: (b * nb + fwd_blk(st), h)),
        out_shape=jax.ShapeDtypeStruct((n_tok, v_dim), BF16),
        scratch_shapes=[pltpu.VMEM((dk, dv), F32), pltpu.VMEM((dk, dv), F32),
                        pltpu.VMEM((tokens_per_batch, dv), F32)],
        compiler_params=pltpu.CompilerParams(vmem_limit_bytes=VMEM_LIMIT),
    )(qk, qk, v, g, dec(raw_fwd), dec(raw_bwd))


def _outproj_kernel(o_ref, x_ref, c_ref, w_ref, g1_ref, sc2_ref, sh2_ref, lng_ref, lnb_ref,
                    x1_ref, h2_ref, hp_ref, *, order):
    i = pl.program_id(0)
    g = order.group(i)
    _, _, is_ctx = order.split(i)
    y = jnp.dot(o_ref[...], w_ref[...], preferred_element_type=F32)
    xin = jnp.where(is_ctx, c_ref[...], x_ref[...])
    r = DEEPNORM_ALPHA * xin + g1_ref[0, pl.ds(g, 1), :] * y
    x1 = _layer_norm(r, lng_ref[...], lnb_ref[...])
    x1_ref[...] = x1
    h2 = x1 * (1.0 + sc2_ref[0, pl.ds(g, 1), :]) + sh2_ref[0, pl.ds(g, 1), :]
    h2_ref[...] = h2
    hp_ref[...] = _to_token_rows(h2).astype(BF16)


def _outproj(og, x2d, ctx2d, w_out, mod, ln_g, ln_b, order):
    v_dim, d = w_out.shape
    tm = order.tile
    s_rows = d // LANES
    row = pl.BlockSpec((1, d), lambda i: (0, 0))
    tok = pl.BlockSpec((tm, d), lambda i: (i, 0))
    return pl.pallas_call(
        functools.partial(_outproj_kernel, order=order),
        grid=(order.n_tiles,),
        in_specs=[pl.BlockSpec((tm, v_dim), lambda i: (i, 0)),
                  pl.BlockSpec((tm, d), lambda i: (order.lat_block(i), 0)),
                  pl.BlockSpec((tm, d), lambda i: (order.ctx_block(i), 0)),
                  pl.BlockSpec((v_dim, d), lambda i: (0, 0), pipeline_mode=pl.Buffered(1)),
                  _mod_spec(0, MOD_G1, d, 1), _mod_spec(0, MOD_SC2, d, 1), _mod_spec(0, MOD_SH2, d, 1),
                  row, row],
        out_specs=[tok, tok, pl.BlockSpec((tm * s_rows, LANES), lambda i: (i, 0))],
        out_shape=[jax.ShapeDtypeStruct((order.n_tok, d), F32)] * 2
                  + [jax.ShapeDtypeStruct((order.n_tok * s_rows, LANES), BF16)],
        compiler_params=pltpu.CompilerParams(vmem_limit_bytes=VMEM_LIMIT),
    )(og, x2d, ctx2d, w_out, mod, mod, mod, ln_g.reshape(1, d), ln_b.reshape(1, d))


def _pool_kernel(x_ref, pw_ref, ps_ref, sc1_ref, sh1_ref, g1_ref, sc2_ref, sh2_ref, lng_ref, lnb_ref,
                 x1_ref, h2_ref, hp_ref, y_ref, *, lat_tiles):
    i = pl.program_id(0)
    g = i // lat_tiles
    x = x_ref[...]
    tm, d = x.shape
    gd = d // len(POOL_WINDOWS)
    h = x * (1.0 + sc1_ref[0, pl.ds(g, 1), :]) + sh1_ref[0, pl.ds(g, 1), :]
    ni = lax.broadcasted_iota(jnp.int32, (tm, tm), 0)
    mi = lax.broadcasted_iota(jnp.int32, (tm, tm), 1)
    same_row = (ni // GRID_W) == (mi // GRID_W)
    delta = mi - ni
    col = lax.broadcasted_iota(jnp.int32, (tm, 1), 0) % GRID_W
    for gi, w in enumerate(POOL_WINDOWS):
        window = jnp.where(same_row & (delta >= -(w // 2)) & (delta < w // 2), 1.0, 0.0).astype(BF16)
        count = (jnp.minimum(col + w // 2, GRID_W) - jnp.maximum(col - w // 2, 0)).astype(F32)
        hg = h[:, gi * gd:(gi + 1) * gd]
        h_hi = hg.astype(BF16)
        r1 = hg - h_hi.astype(F32)
        h_mid = r1.astype(BF16)
        h_lo = (r1 - h_mid.astype(F32)).astype(BF16)
        total = (jnp.dot(window, h_hi, preferred_element_type=F32)
                 + jnp.dot(window, h_mid, preferred_element_type=F32)
                 + jnp.dot(window, h_lo, preferred_element_type=F32))
        p = total / count - hg
        y_ref[:, gi * gd:(gi + 1) * gd] = jnp.dot(p.astype(BF16), pw_ref[gi], preferred_element_type=F32)
    y = y_ref[...] * ps_ref[...]
    r = DEEPNORM_ALPHA * x + g1_ref[0, pl.ds(g, 1), :] * y
    x1 = _layer_norm(r, lng_ref[...], lnb_ref[...])
    x1_ref[...] = x1
    h2 = x1 * (1.0 + sc2_ref[0, pl.ds(g, 1), :]) + sh2_ref[0, pl.ds(g, 1), :]
    h2_ref[...] = h2
    hp_ref[...] = _to_token_rows(h2).astype(BF16)


def _pool_layer(x_prev, order0, pool_w, pool_scale, mod, ln_g, ln_b, layer, batch, seq):
    d = x_prev.shape[1]
    tm = order0.tile
    s_rows = d // LANES
    lat_tiles = seq // tm
    n_tiles = batch * lat_tiles
    ng, gd, _ = pool_w.shape

    def src(i):
        b = i // lat_tiles
        return (b * order0.per_batch + order0.ctx_tiles + i % lat_tiles, 0)

    row = pl.BlockSpec((1, d), lambda i: (0, 0))
    tok = pl.BlockSpec((tm, d), lambda i: (i, 0))
    ms = lambda chunk: _mod_spec(layer, chunk, d, 1)
    return pl.pallas_call(
        functools.partial(_pool_kernel, lat_tiles=lat_tiles),
        grid=(n_tiles,),
        in_specs=[pl.BlockSpec((tm, d), src),
                  pl.BlockSpec((ng, gd, gd), lambda i: (0, 0, 0)),
                  row, ms(MOD_SC1), ms(MOD_SH1), ms(MOD_G1), ms(MOD_SC2), ms(MOD_SH2), row, row],
        out_specs=[tok, tok, pl.BlockSpec((tm * s_rows, LANES), lambda i: (i, 0))],
        out_shape=[jax.ShapeDtypeStruct((n_tiles * tm, d), F32)] * 2
                  + [jax.ShapeDtypeStruct((n_tiles * tm * s_rows, LANES), BF16)],
        scratch_shapes=[pltpu.VMEM((tm, d), F32)],
        compiler_params=pltpu.CompilerParams(vmem_limit_bytes=VMEM_LIMIT),
    )(x_prev, pool_w.astype(BF16), pool_scale.reshape(1, d), mod, mod, mod, mod, mod,
      ln_g.reshape(1, d), ln_b.reshape(1, d))


def _split_bf16(x):
    hi = x.astype(BF16)
    return hi, (x - hi.astype(F32)).astype(BF16)


def _route_logits(h_hi, h_mid, w_ref):
    nt = (((1,), (1,)), ((), ()))
    w_hi, w_mid = _split_bf16(w_ref[...])
    return (lax.dot_general(w_hi, h_hi, nt, preferred_element_type=F32)
            + (lax.dot_general(w_hi, h_mid, nt, preferred_element_type=F32)
               + lax.dot_general(w_mid, h_hi, nt, preferred_element_type=F32)))


def _route_tile(logits, b_ref, eidx_ref, gate_ref, rank_ref, cnt_ref, carry_ref):
    i = pl.program_id(0)
    n_exp, tr = logits.shape
    per = n_exp // N_EXPERT_GROUPS
    neg = -jnp.inf

    @pl.when(i == 0)
    def _():
        carry_ref[...] = jnp.zeros_like(carry_ref)

    scores = jax.nn.sigmoid(logits)
    biased = scores + b_ref[...]

    sub_idx = lax.broadcasted_iota(jnp.int32, (per, tr), 0)
    group_score = []
    for gi in range(N_EXPERT_GROUPS):
        sub = biased[gi * per:(gi + 1) * per, :]
        m1 = jnp.max(sub, axis=0, keepdims=True)
        first = jnp.min(jnp.where(sub == m1, sub_idx, per), axis=0, keepdims=True)
        m2 = jnp.max(jnp.where(sub_idx == first, neg, sub), axis=0, keepdims=True)
        group_score.append(m1 + m2)
    masked = []
    for gi in range(N_EXPERT_GROUPS):
        ahead = jnp.zeros((1, tr), jnp.int32)
        for gj in range(N_EXPERT_GROUPS):
            if gj == gi:
                continue
            wins = (group_score[gj] >= group_score[gi]) if gj < gi else (group_score[gj] > group_score[gi])
            ahead = ahead + wins.astype(jnp.int32)
        keep = ahead < TOPK_GROUPS
        masked.append(jnp.where(keep, biased[gi * per:(gi + 1) * per, :], neg))
    masked = jnp.concatenate(masked, axis=0)

    eid = lax.broadcasted_iota(jnp.int32, (n_exp, tr), 0)
    picks, gates = [], []
    onehot = jnp.zeros((n_exp, tr), F32)
    for _ in range(TOP_K):
        m = jnp.max(masked, axis=0, keepdims=True)
        first = jnp.min(jnp.where(masked == m, eid, n_exp), axis=0, keepdims=True)
        sel = eid == first
        gates.append(jnp.sum(jnp.where(sel, scores, 0.0), axis=0, keepdims=True))
        picks.append(first)
        masked = jnp.where(sel, neg, masked)
        onehot = onehot + sel.astype(F32)
    gate_sum = gates[0]
    for gk in gates[1:]:
        gate_sum = gate_sum + gk

    si = lax.broadcasted_iota(jnp.int32, (tr, tr), 0)
    ti = lax.broadcasted_iota(jnp.int32, (tr, tr), 1)
    before = jnp.where(si < ti, 1.0, 0.0).astype(BF16)
    prefix = jnp.dot(onehot.astype(BF16), before, preferred_element_type=F32) + carry_ref[:, :1]
    for k in range(TOP_K):
        eidx_ref[k:k + 1, :] = picks[k]
        gate_ref[k:k + 1, :] = gates[k] / gate_sum * ROUTED_SCALE
        rank_ref[k:k + 1, :] = jnp.sum(jnp.where(eid == picks[k], prefix, 0.0), axis=0,
                                       keepdims=True).astype(jnp.int32)
    carry_ref[...] = carry_ref[...] + jnp.sum(onehot, axis=1, keepdims=True)
    cnt_ref[...] = carry_ref[...]


def _router_kernel(h_ref, w_ref, b_ref, sg_ref, su_ref, sd_ref, eidx_ref, gate_ref, rank_ref, cnt_ref, fsh_ref,
                   carry_ref):
    hb, h_mid = _split_bf16(h_ref[...])
    logits = _route_logits(hb, h_mid, w_ref)
    a = (_silu(jnp.dot(hb, sg_ref[...], preferred_element_type=F32))
         * jnp.dot(hb, su_ref[...], preferred_element_type=F32))
    fsh_ref[...] = jnp.dot(a.astype(BF16), sd_ref[...], preferred_element_type=F32)
    _route_tile(logits, b_ref, eidx_ref, gate_ref, rank_ref, cnt_ref, carry_ref)


def _router(h2, router_w, router_b, sg, su, sd):
    n_tok, d = h2.shape
    n_exp = router_w.shape[1]
    ds_ = sg.shape[1]
    tr = ROUTER_TILE
    assert n_tok % tr == 0
    sel = pl.BlockSpec((TOP_K, tr), lambda i: (0, i))
    tok = pl.BlockSpec((tr, d), lambda i: (i, 0))
    return pl.pallas_call(
        _router_kernel,
        grid=(n_tok // tr,),
        in_specs=[tok,
                  pl.BlockSpec((n_exp, d), lambda i: (0, 0)),
                  pl.BlockSpec((n_exp, 1), lambda i: (0, 0)),
                  pl.BlockSpec((d, ds_), lambda i: (0, 0)),
                  pl.BlockSpec((d, ds_), lambda i: (0, 0)),
                  pl.BlockSpec((ds_, d), lambda i: (0, 0))],
        out_specs=[sel, sel, sel, pl.BlockSpec((n_exp, LANES), lambda i: (0, 0)), tok],
        out_shape=[jax.ShapeDtypeStruct((TOP_K, n_tok), jnp.int32),
                   jax.ShapeDtypeStruct((TOP_K, n_tok), F32),
                   jax.ShapeDtypeStruct((TOP_K, n_tok), jnp.int32),
                   jax.ShapeDtypeStruct((n_exp, LANES), F32),
                   jax.ShapeDtypeStruct((n_tok, d), F32)],
        scratch_shapes=[pltpu.VMEM((n_exp, LANES), F32)],
        compiler_params=pltpu.CompilerParams(vmem_limit_bytes=VMEM_LIMIT),
    )(h2, router_w.astype(F32).T, router_b.astype(F32).reshape(n_exp, 1), sg, su, sd)


def _dispatch_kernel(pe_ref, pd_ref, nu_ref, slot_hbm, hp_ref, xs_hbm, idx_ref, zero_ref, isem, zsem, rsem, *,
                     n_steps, n_exp, n_blocks, s_rows, bm):
    i = pl.program_id(0)
    tt = hp_ref.shape[0] // s_rows
    zrows = bm * s_rows
    per_tile = tt * TOP_K

    def idx_copy(tile, buf):
        return pltpu.make_async_copy(slot_hbm.at[tile], idx_ref.at[pl.ds(buf * per_tile, per_tile)], isem.at[buf])

    def zero_copy(row0):
        return pltpu.make_async_copy(zero_ref, xs_hbm.at[pl.ds(pl.multiple_of(row0, s_rows), zrows), :], zsem)

    @pl.when(i == 0)
    def _():
        zero_ref[...] = jnp.zeros_like(zero_ref)
        nu = nu_ref[0]

        def fill(fn):
            def per_expert(e, carry):
                @pl.when(pd_ref[e] > 0)
                def _():
                    fn(zero_copy((pe_ref[e] - bm) * s_rows))
                return carry
            lax.fori_loop(0, n_exp, per_expert, 0)

            def per_tail(b, carry):
                fn(zero_copy(b * zrows))
                return carry
            lax.fori_loop(nu, n_blocks, per_tail, 0)

        fill(lambda cp: cp.start())
        fill(lambda cp: cp.wait())
        idx_copy(0, 0).start()
        idx_copy(0, 0).wait()
        if n_steps > 1:
            idx_copy(1, 1).start()

    buf = i % 2

    @pl.when(i > 0)
    def _():
        idx_copy(i, buf).wait()

    group = 8
    assert tt % group == 0

    def body(gi, carry):
        base = pl.multiple_of(gi * (group * s_rows), group * s_rows)
        ibase = buf * per_tile + gi * (group * TOP_K)
        for u in range(group):
            src = hp_ref.at[pl.ds(base + u * s_rows, s_rows), :]
            for k in range(TOP_K):
                t = idx_ref[ibase + (u * TOP_K + k)]
                pltpu.make_async_copy(src, xs_hbm.at[pl.ds(pl.multiple_of(t * s_rows, s_rows), s_rows), :],
                                      rsem).start()
        return carry

    lax.fori_loop(0, tt // group, body, 0)

    @pl.when((i > 0) & (i + 1 < n_steps))
    def _():
        idx_copy(i + 1, 1 - buf).start()

    for _ in range(TOP_K):
        pltpu.make_async_copy(hp_ref, xs_hbm.at[pl.ds(0, tt * s_rows), :], rsem).wait()


def _dispatch(hp, slots, pad_end, padded, n_used, n_slots, bm):
    rows, _ = hp.shape
    n_steps, per_tile = slots.shape
    tt = per_tile // TOP_K
    s_rows = rows // (n_steps * tt)
    n_exp = pad_end.shape[0]
    grid_spec = pltpu.PrefetchScalarGridSpec(
        num_scalar_prefetch=3,
        grid=(n_steps,),
        in_specs=[pl.BlockSpec(memory_space=pl.ANY),
                  pl.BlockSpec((tt * s_rows, LANES), lambda i, pe, pd, nu: (i, 0))],
        out_specs=pl.BlockSpec(memory_space=pl.ANY),
        scratch_shapes=[pltpu.SMEM((2 * per_tile,), jnp.int32),
                        pltpu.VMEM((bm * s_rows, LANES), BF16),
                        pltpu.SemaphoreType.DMA((2,)),
                        pltpu.SemaphoreType.DMA(()),
                        pltpu.SemaphoreType.DMA(())])
    return pl.pallas_call(
        functools.partial(_dispatch_kernel, n_steps=n_steps, n_exp=n_exp, n_blocks=n_slots // bm,
                          s_rows=s_rows, bm=bm),
        grid_spec=grid_spec,
        out_shape=jax.ShapeDtypeStruct((n_slots * s_rows, LANES), BF16),
        compiler_params=pltpu.CompilerParams(vmem_limit_bytes=VMEM_LIMIT, disable_bounds_checks=True),
    )(pad_end, padded, n_used, slots, hp)


def _moe_kernel(be_ref, nu_ref, vb_ref, ne_ref, x_ref, wg_hbm, wu_hbm, wd_hbm, y_ref, wgb, wub, wdb,
                wg_stage, wu_stage, wd_stage, wsem, *, s_rows, layer, n_exp):
    i = pl.program_id(0)
    nu = nu_ref[0]
    parts = MOE_PARTS
    pm = x_ref.shape[0] // s_rows // parts
    rows_of = [slice(p * pm * s_rows, (p + 1) * pm * s_rows) for p in range(parts)]

    def experts(active):
        hidden = []
        for p in active:
            xt = _from_token_rows(x_ref[rows_of[p], :].astype(F32), s_rows)
            gate = up = None
            for c in range(s_rows // 2):
                xc = jnp.concatenate([xt[2 * c], xt[2 * c + 1]], axis=1).astype(BF16)
                rows = slice(2 * c * LANES, (2 * c + 2) * LANES)
                gc = jnp.dot(xc, wgb[rows, :], preferred_element_type=F32)
                uc = jnp.dot(xc, wub[rows, :], preferred_element_type=F32)
                gate = gc if gate is None else gate + gc
                up = uc if up is None else up + uc
            hidden.append((gate, up))
        outs = []
        for gate, up in hidden:
            a = (_silu(gate) * up).astype(BF16)
            outs.append(jnp.dot(a, wdb[...], preferred_element_type=F32))
        for p, y in zip(active, outs):
            y_ref[rows_of[p], :] = _to_token_rows(y).astype(BF16)
        for p in range(parts):
            if p not in active:
                y_ref[rows_of[p], :] = jnp.zeros((pm * s_rows, LANES), BF16)

    def weight_copies(e):
        return [pltpu.make_async_copy(src.at[layer, e], dst, wsem)
                for src, dst in ((wg_hbm, wg_stage), (wu_hbm, wu_stage), (wd_hbm, wd_stage))]

    @pl.when(i < nu)
    def _():
        e = be_ref[i]

        @pl.when(i == 0)
        def _():
            for cp in weight_copies(e):
                cp.start()

        @pl.when((i == 0) | (e != be_ref[jnp.maximum(i - 1, 0)]))
        def _():
            for cp in weight_copies(e):
                cp.wait()
            wgb[...] = wg_stage[...].astype(BF16)
            wub[...] = wu_stage[...].astype(BF16)
            wdb[...] = wd_stage[...].astype(BF16)
            nxt = ne_ref[e]

            @pl.when(nxt < n_exp)
            def _():
                for cp in weight_copies(nxt):
                    cp.start()

        valid = vb_ref[i]
        for n_active in range(1, parts + 1):
            above = (valid > (n_active - 1) * pm) if n_active > 1 else True
            within = (valid <= n_active * pm) if n_active < parts else True

            @pl.when(above & within)
            def _():
                experts(list(range(n_active)))

    @pl.when(i >= nu)
    def _():
        y_ref[...] = jnp.zeros_like(y_ref)


def _moe_experts(xs, blk_expert, n_used, blk_valid, next_expert, wg, wu, wd, layer, bm):
    _, n_exp, d, de = wg.shape
    s_rows = d // LANES
    nb = xs.shape[0] // (bm * s_rows)
    hbm = pl.BlockSpec(memory_space=pl.ANY)
    grid_spec = pltpu.PrefetchScalarGridSpec(
        num_scalar_prefetch=4,
        grid=(nb,),
        in_specs=[pl.BlockSpec((bm * s_rows, LANES), lambda i, be, nu, vb, ne: (jnp.minimum(i, nu[0] - 1), 0)),
                  hbm, hbm, hbm],
        out_specs=pl.BlockSpec((bm * s_rows, LANES), lambda i, be, nu, vb, ne: (i, 0)),
        scratch_shapes=[pltpu.VMEM((d, de), BF16), pltpu.VMEM((d, de), BF16), pltpu.VMEM((de, d), BF16),
                        pltpu.VMEM((d, de), F32), pltpu.VMEM((d, de), F32), pltpu.VMEM((de, d), F32),
                        pltpu.SemaphoreType.DMA(())])
    return pl.pallas_call(
        functools.partial(_moe_kernel, s_rows=s_rows, layer=layer, n_exp=n_exp),
        grid_spec=grid_spec,
        out_shape=jax.ShapeDtypeStruct(xs.shape, BF16),
        compiler_params=pltpu.CompilerParams(vmem_limit_bytes=VMEM_LIMIT),
    )(blk_expert, n_used, blk_valid, next_expert, xs, wg, wu, wd)


def _combine_kernel(slot_hbm, y_hbm, gate_ref, fsh_ref, x1_ref, g2_ref, lng_ref, lnb_ref,
                    out_ref, idx_ref, ybuf, frow_ref, isem, rsem, *, order, n_steps, s_rows):
    i = pl.program_id(0)
    tt, d = x1_ref.shape
    n_rows = TOP_K * tt

    def idx_copy(tile, buf):
        return pltpu.make_async_copy(slot_hbm.at[tile], idx_ref.at[pl.ds(buf * n_rows, n_rows)], isem.at[buf])

    group = 8
    assert tt % group == 0

    def issue_token(buf, r0, u):
        base = pl.multiple_of(r0 * s_rows, group * s_rows)
        ibase = buf * n_rows + r0
        for k in range(TOP_K):
            t = idx_ref[ibase + (k * tt + u)]
            pltpu.make_async_copy(y_hbm.at[pl.ds(pl.multiple_of(t * s_rows, s_rows), s_rows), :],
                                  ybuf.at[buf, pl.ds(base + (k * tt + u) * s_rows, s_rows), :],
                                  rsem.at[buf]).start()

    @pl.when(i == 0)
    def _():
        idx_copy(0, 0).start()
        idx_copy(0, 0).wait()

        def first(gi, carry):
            for u in range(group):
                issue_token(0, gi * group, u)
            return carry
        lax.fori_loop(0, tt // group, first, 0)
        if n_steps > 1:
            idx_copy(1, 1).start()

    buf = i % 2
    nxt = 1 - buf
    has_next = i + 1 < n_steps

    @pl.when(has_next)
    def _():
        idx_copy(i + 1, nxt).wait()

    def token_loop(cur, prefetch):
        pltpu.make_async_copy(y_hbm.at[pl.ds(0, n_rows * s_rows), :], ybuf.at[cur], rsem.at[cur]).wait()

        def body(gi, carry):
            r0 = gi * group
            base = pl.multiple_of(r0 * s_rows, group * s_rows)
            gbase = pl.multiple_of(r0 * TOP_K, group * TOP_K)
            for u in range(group):
                if prefetch:
                    issue_token(1 - cur, r0, u)
                gates = gate_ref[pl.ds(gbase + u * TOP_K, TOP_K), :]
                acc = jnp.zeros((s_rows, LANES), F32)
                for k in range(TOP_K):
                    rows = pl.ds(base + (k * tt + u) * s_rows, s_rows)
                    acc = acc + jnp.broadcast_to(gates[k:k + 1, :], (s_rows, LANES)) * ybuf[cur, rows, :].astype(F32)
                frow_ref[pl.ds(base + u * s_rows, s_rows), :] = acc
            return carry
        lax.fori_loop(0, tt // group, body, 0)

    for cur in (0, 1):
        @pl.when((buf == cur) & has_next)
        def _():
            token_loop(cur, True)

        @pl.when((buf == cur) & jnp.logical_not(has_next))
        def _():
            token_loop(cur, False)

    @pl.when(i + 2 < n_steps)
    def _():
        idx_copy(i + 2, buf).start()

    routed = _from_token_rows(frow_ref[...], s_rows)
    f = fsh_ref[...] + jnp.concatenate([routed[s] for s in range(s_rows)], axis=1)
    g = order.group(i)
    r = DEEPNORM_ALPHA * x1_ref[...] + g2_ref[0, pl.ds(g, 1), :] * f
    out_ref[...] = _layer_norm(r, lng_ref[...], lnb_ref[...])


def _combine(slots, y_sorted, gate, fsh, x1, mod, ln_g, ln_b, layer, order):
    n_tok, d = x1.shape
    tt = order.tile
    n_steps = order.n_tiles
    s_rows = d // LANES
    row = pl.BlockSpec((1, d), lambda i: (0, 0))
    tok = pl.BlockSpec((tt, d), lambda i: (i, 0))
    return pl.pallas_call(
        functools.partial(_combine_kernel, order=order, n_steps=n_steps, s_rows=s_rows),
        grid=(n_steps,),
        in_specs=[pl.BlockSpec(memory_space=pl.ANY),
                  pl.BlockSpec(memory_space=pl.ANY),
                  pl.BlockSpec((tt * TOP_K, LANES), lambda i: (i, 0)),
                  tok, tok,
                  _mod_spec(layer, MOD_G2, d, 1), row, row],
        out_specs=tok,
        out_shape=jax.ShapeDtypeStruct((n_tok, d), F32),
        scratch_shapes=[pltpu.SMEM((2 * TOP_K * tt,), jnp.int32),
                        pltpu.VMEM((2, TOP_K * tt * s_rows, LANES), BF16),
                        pltpu.VMEM((tt * s_rows, LANES), F32),
                        pltpu.SemaphoreType.DMA((2,)),
                        pltpu.SemaphoreType.DMA((2,))],
        compiler_params=pltpu.CompilerParams(vmem_limit_bytes=VMEM_LIMIT, disable_bounds_checks=True),
    )(slots, y_sorted, gate, fsh, x1, mod, ln_g.reshape(1, d), ln_b.reshape(1, d))


def _moe_layer(produced, layer, mod, order, router_w, router_b, wg, wu, wd, sg, su, sd, ln_g, ln_b):
    x1, h2, hp = produced
    n_tok, d = h2.shape
    n_exp = router_w.shape[1]
    bm = MOE_BLOCK
    tt = order.tile
    eidx_t, gate_t, rank_t, cnt, fsh = _router(h2, router_w, router_b, sg[layer].astype(BF16),
                                               su[layer].astype(BF16), sd[layer].astype(BF16))

    counts = cnt[:, 0].astype(jnp.int32)
    padded = (counts + bm - 1) // bm * bm
    pad_end = jnp.cumsum(padded).astype(jnp.int32)
    pad_start = pad_end - padded
    experts = jnp.arange(n_exp, dtype=jnp.int32)
    slot_t = rank_t + jnp.sum(jnp.where(eidx_t[:, :, None] == experts, pad_start, 0), axis=-1)
    n_slots = n_tok * TOP_K + n_exp * bm
    nb = n_slots // bm
    n_used = pad_end[-1:] // bm
    blk_expert = jnp.minimum(jnp.sum(pad_end[None, :] <= (jnp.arange(nb, dtype=jnp.int32) * bm)[:, None], axis=1),
                             n_exp - 1).astype(jnp.int32)
    of_block = blk_expert[:, None] == experts
    blk_first = jnp.sum(jnp.where(of_block, pad_start, 0), axis=1) // bm
    blk_valid = jnp.clip(jnp.sum(jnp.where(of_block, counts, 0), axis=1)
                         - (jnp.arange(nb, dtype=jnp.int32) - blk_first) * bm, 0, bm).astype(jnp.int32)
    later = (experts[None, :] > experts[:, None]) & (counts[None, :] > 0)
    next_expert = jnp.min(jnp.where(later, experts[None, :], n_exp), axis=1).astype(jnp.int32)
    n_tiles = n_tok // tt
    dt = next(t for t in range(DISPATCH_TILE, 0, -tt) if n_tok % t == 0)
    slots_tok = slot_t.T.reshape(n_tok // dt, dt * TOP_K)
    slots_k = slot_t.reshape(TOP_K, n_tiles, tt).transpose(1, 0, 2).reshape(n_tiles, TOP_K * tt)

    xs = _dispatch(hp, slots_tok, pad_end, padded, n_used, n_slots, bm)
    ys = _moe_experts(xs, blk_expert, n_used, blk_valid, next_expert, wg, wu, wd, layer, bm)
    gates = jnp.broadcast_to(gate_t.T.reshape(n_tok * TOP_K, 1), (n_tok * TOP_K, LANES))
    return _combine(slots_k, ys, gates, fsh, x1, mod, ln_g[layer], ln_b[layer], layer, order)


def kernel(x, c, ctx, c_ctx, ada_w, ada_b, ln_mix_g, ln_mix_b, ln_ffn_g, ln_ffn_b, ret_w_in, ret_w_out,
           ret_decay_fwd, ret_decay_bwd, pool_w, pool_scale, router_w, router_b, exp_w_gate, exp_w_up,
           exp_w_down, sh_w_gate, sh_w_up, sh_w_down):
    batch, seq, d = x.shape
    ctx_len = ctx.shape[1]
    heads = RET_HEADS
    qk_dim = d
    head_dim = qk_dim // heads
    v_dim = ret_w_out.shape[1]
    assert ada_w.shape[0] == DEPTH == 2 and batch + 1 <= MOD_ROWS

    cvec = jnp.zeros((MOD_ROWS, d), F32).at[:batch].set(c.astype(F32)).at[batch].set(c_ctx.astype(F32))
    mod = _modulation(cvec, ada_w.astype(F32), ada_b.astype(F32))

    x2d = x.reshape(batch * seq, d)
    ctx2d = ctx.reshape(batch * ctx_len, d)
    order0 = _Order(batch, seq, ctx_len, TOKEN_TILE)

    w_in = ret_w_in[0].astype(F32)
    tabs = _rope_tables(seq, head_dim, TOKEN_TILE)
    qk = _inproj(x2d, ctx2d, mod, w_in, 0, 2 * qk_dim, order0, BF16, rope_tabs=tabs, k_col0=qk_dim,
                 head_dim=head_dim)
    v = _inproj(x2d, ctx2d, mod, w_in, 2 * qk_dim, v_dim, order0, BF16)
    g = _inproj(x2d, ctx2d, mod, w_in, 2 * qk_dim + v_dim, v_dim, order0, F32)
    og = _retention(qk, v, g, ret_decay_fwd[0], ret_decay_bwd[0], batch, ctx_len + seq, ctx_len)
    produced = _outproj(og, x2d, ctx2d, ret_w_out[0].astype(BF16), mod, ln_mix_g[0], ln_mix_b[0], order0)
    xa = _moe_layer(produced, 0, mod, order0, router_w[0], router_b[0], exp_w_gate, exp_w_up, exp_w_down,
                    sh_w_gate, sh_w_up, sh_w_down, ln_ffn_g, ln_ffn_b)

    order1 = _Order(batch, seq, 0, TOKEN_TILE)
    produced = _pool_layer(xa, order0, pool_w[0], pool_scale[0], mod, ln_mix_g[1], ln_mix_b[1], 1, batch, seq)
    xb = _moe_layer(produced, 1, mod, order1, router_w[1], router_b[1], exp_w_gate, exp_w_up, exp_w_down,
                    sh_w_gate, sh_w_up, sh_w_down, ln_ffn_g, ln_ffn_b)
    return xb.reshape(batch, seq, d)
```

```python
import functools

import jax
import jax.numpy as jnp
from jax import lax
from jax.experimental import pallas as pl
from jax.experimental.pallas import tpu as pltpu

F32 = jnp.float32
BF16 = jnp.bfloat16
HIGHEST = lax.Precision.HIGHEST

DEPTH = 2
GRID_W = 64
RET_HEADS = 8
ROPE_BASE = 10000.0
POOL_WINDOWS = (2, 4, 8, 16)
N_EXPERT_GROUPS = 8
TOPK_GROUPS = 4
TOP_K = 8
ROUTED_SCALE = 2.5
LN_EPS = 1e-5
N_MOD = 6
DEEPNORM_ALPHA = (2.0 * DEPTH) ** 0.25
MOD_SH1, MOD_SC1, MOD_G1, MOD_SH2, MOD_SC2, MOD_G2 = range(N_MOD)

LANES = 128
TOKEN_TILE = 256
RET_BLOCK = 2816
RET_CHUNK = 256
ROUTER_TILE = 512
DISPATCH_TILE = 2048
MOE_BLOCK = 512
MOE_PARTS = 2
MOD_ROWS = 8
VMEM_LIMIT = 56 * 1024 * 1024


def _silu(x):
    return x * jax.nn.sigmoid(x)


def _layer_norm(r, g, b):
    mu = jnp.mean(r, axis=-1, keepdims=True)
    var = jnp.mean(jnp.square(r - mu), axis=-1, keepdims=True)
    return (r - mu) * lax.rsqrt(var + LN_EPS) * g + b


def _to_token_rows(h):
    tm, d = h.shape
    s_rows = d // LANES
    blocks = jnp.stack([h[:, s * LANES:(s + 1) * LANES] for s in range(s_rows)], axis=0)
    return jnp.swapaxes(blocks, 0, 1).reshape(tm * s_rows, LANES)


def _from_token_rows(rows, s_rows):
    return jnp.swapaxes(rows.reshape(rows.shape[0] // s_rows, s_rows, LANES), 0, 1)


class _Order:
    def __init__(self, batch, seq, ctx_len, tile):
        assert seq % tile == 0 and ctx_len % tile == 0
        self.batch, self.seq, self.ctx_len, self.tile = batch, seq, ctx_len, tile
        self.ctx_tiles = ctx_len // tile
        self.lat_tiles = seq // tile
        self.per_batch = self.ctx_tiles + self.lat_tiles
        self.n_tiles = batch * self.per_batch
        self.n_tok = self.n_tiles * tile

    def split(self, i):
        b = i // self.per_batch
        r = i % self.per_batch
        return b, r, r < self.ctx_tiles

    def group(self, i):
        b, _, is_ctx = self.split(i)
        return jnp.where(is_ctx, self.batch, b)

    def lat_block(self, i):
        b, r, _ = self.split(i)
        return b * self.lat_tiles + jnp.maximum(r - self.ctx_tiles, 0)

    def ctx_block(self, i):
        b, r, _ = self.split(i)
        return b * self.ctx_tiles + jnp.minimum(r, max(self.ctx_tiles - 1, 0))


def _mod_kernel(c_ref, w_ref, b_ref, o_ref):
    s = _silu(c_ref[...])
    o_ref[0] = jnp.dot(s, w_ref[0], precision=HIGHEST, preferred_element_type=F32) + b_ref[0]


def _modulation(cvec, ada_w, ada_b):
    depth, d, n = ada_w.shape
    tn = 2048 if n % 2048 == 0 else 1024
    return pl.pallas_call(
        _mod_kernel,
        grid=(depth, n // tn),
        in_specs=[pl.BlockSpec((MOD_ROWS, d), lambda l, j: (0, 0)),
                  pl.BlockSpec((1, d, tn), lambda l, j: (l, 0, j)),
                  pl.BlockSpec((1, 1, tn), lambda l, j: (l, 0, j))],
        out_specs=pl.BlockSpec((1, MOD_ROWS, tn), lambda l, j: (l, 0, j)),
        out_shape=jax.ShapeDtypeStruct((depth, MOD_ROWS, n), F32),
        compiler_params=pltpu.CompilerParams(vmem_limit_bytes=VMEM_LIMIT),
    )(cvec, ada_w, ada_b.reshape(depth, 1, n))


def _mod_spec(layer, chunk, d, n_grid_args):
    if n_grid_args == 1:
        return pl.BlockSpec((1, MOD_ROWS, d), lambda i: (layer, 0, chunk))
    return pl.BlockSpec((1, MOD_ROWS, d), lambda j, i: (layer, 0, chunk))


def _inproj_kernel(x_ref, c_ref, sc_ref, sh_ref, w_ref, *rest, order, rope, k_col0, head_dim):
    i = pl.program_id(1)
    j = pl.program_id(0)
    if rope:
        cos_ref, sin_ref, o_ref, wb_ref = rest
    else:
        o_ref, wb_ref = rest

    @pl.when(i == 0)
    def _():
        wb_ref[...] = w_ref[...].astype(BF16)

    g = order.group(i)
    _, _, is_ctx = order.split(i)
    sc = sc_ref[0, pl.ds(g, 1), :]
    sh = sh_ref[0, pl.ds(g, 1), :]
    xin = jnp.where(is_ctx, c_ref[...], x_ref[...])
    a = (xin * (1.0 + sc) + sh).astype(BF16)
    acc = jnp.dot(a, wb_ref[...], preferred_element_type=F32)
    if not rope:
        o_ref[...] = acc.astype(o_ref.dtype)
        return
    tn = acc.shape[1]
    lane = lax.broadcasted_iota(jnp.int32, acc.shape, 1)
    partner = jnp.where((lane & 1) == 0, pltpu.roll(acc, tn - 1, 1), pltpu.roll(acc, 1, 1))
    cos = cos_ref[...]
    sin = sin_ref[...]
    scale = jnp.where(j * tn >= k_col0, head_dim ** -0.5, 1.0).astype(F32)
    for h in range(tn // head_dim):
        sl = slice(h * head_dim, (h + 1) * head_dim)
        o_ref[:, sl] = ((acc[:, sl] * cos + partner[:, sl] * sin) * scale).astype(o_ref.dtype)


def _inproj(x2d, ctx2d, mod, w_in, col0, n, order, out_dtype, rope_tabs=None, k_col0=0, head_dim=0):
    d = w_in.shape[0]
    tm = order.tile
    tn = min(n, 2048)
    assert col0 % tn == 0 and n % tn == 0
    rope = rope_tabs is not None
    in_specs = [pl.BlockSpec((tm, d), lambda j, i: (order.lat_block(i), 0)),
                pl.BlockSpec((tm, d), lambda j, i: (order.ctx_block(i), 0)),
                _mod_spec(0, MOD_SC1, d, 2), _mod_spec(0, MOD_SH1, d, 2),
                pl.BlockSpec((d, tn), lambda j, i: (0, col0 // tn + j), pipeline_mode=pl.Buffered(1))]
    args = [x2d, ctx2d, mod, mod, w_in]
    if rope:
        def tab_map(j, i):
            _, r, is_ctx = order.split(i)
            return (jnp.where(is_ctx, order.lat_tiles, r - order.ctx_tiles), 0)
        in_specs += [pl.BlockSpec((tm, head_dim), tab_map), pl.BlockSpec((tm, head_dim), tab_map)]
        args += list(rope_tabs)
    return pl.pallas_call(
        functools.partial(_inproj_kernel, order=order, rope=rope, k_col0=k_col0, head_dim=head_dim),
        grid=(n // tn, order.n_tiles),
        in_specs=in_specs,
        out_specs=pl.BlockSpec((tm, tn), lambda j, i: (i, j)),
        out_shape=jax.ShapeDtypeStruct((order.n_tok, n), out_dtype),
        scratch_shapes=[pltpu.VMEM((d, tn), BF16)],
        compiler_params=pltpu.CompilerParams(vmem_limit_bytes=VMEM_LIMIT),
    )(*args)


def _rope_tables(seq, head_dim, tile):
    n_freq = head_dim // 4
    pos = jnp.arange(seq, dtype=jnp.int32)
    row = (pos // GRID_W).astype(F32)
    col = (pos % GRID_W).astype(F32)
    freqs = ROPE_BASE ** (-jnp.arange(n_freq, dtype=F32) / n_freq)
    ang = jnp.concatenate([row[:, None] * freqs, col[:, None] * freqs], axis=-1)
    cos = jnp.repeat(jnp.cos(ang), 2, axis=-1)
    sin = jnp.stack([-jnp.sin(ang), jnp.sin(ang)], axis=-1).reshape(seq, head_dim)
    ident = jnp.zeros((tile, head_dim), F32)
    return jnp.concatenate([cos, ident + 1.0], axis=0), jnp.concatenate([sin, ident], axis=0)


def _ret_kernel(q_ref, k_ref, v_ref, g_ref, df_ref, db_ref, o_ref, sf_ref, sb_ref, obuf_ref, *,
                n_blocks, chunks_per_block, ctx_chunks):
    C = RET_CHUNK
    NB, CPB, NCC = n_blocks, chunks_per_block, ctx_chunks
    st = pl.program_id(2)
    phase = st // (NB + 1)
    s = st % (NB + 1)
    ldf = jax.nn.log_sigmoid(df_ref[0])[:, :1]
    ldb = jax.nn.log_sigmoid(db_ref[0])[:, :1]
    pos = lax.broadcasted_iota(jnp.int32, (C, 1), 0).astype(F32)
    tdims = (((0,), (0,)), ((), ()))

    @pl.when(phase == 0)
    def _backward_states():
        @pl.when(s == 0)
        def _():
            sb_ref[...] = jnp.zeros_like(sb_ref)

        q_dec = jnp.exp(ldb * (C - pos))
        k_dec = jnp.exp(ldb * pos)
        chunk_dec = jnp.exp(ldb * C)

        def chunk(blk, c):
            rows = pl.ds(c * C, C)
            q = q_ref[rows, :].astype(F32)
            k = k_ref[rows, :].astype(F32)
            v = v_ref[rows, :]
            state = sb_ref[...]
            goff = pl.multiple_of((blk * CPB + c) * C, C)
            obuf_ref[pl.ds(goff, C), :] = jnp.dot((q * q_dec).astype(BF16), state.astype(BF16),
                                                  preferred_element_type=F32)
            sb_ref[...] = state * chunk_dec + lax.dot_general((k * k_dec).astype(BF16), v, tdims,
                                                              preferred_element_type=F32)

        @pl.when(s == 0)
        def _():
            for c in reversed(range(NCC)):
                chunk(0, c)

        @pl.when((s > 0) & (s < NB))
        def _():
            for c in reversed(range(CPB)):
                chunk(NB - s, c)

        @pl.when(s == NB)
        def _():
            for c in reversed(range(NCC, CPB)):
                chunk(0, c)

    @pl.when((phase == 1) & (s < NB))
    def _forward_and_output():
        @pl.when(s == 0)
        def _():
            sf_ref[...] = jnp.zeros_like(sf_ref)

        q_dec = jnp.exp(ldf * (pos + 1.0))
        k_dec = jnp.exp(ldf * (C - 1.0 - pos))
        chunk_dec = jnp.exp(ldf * C)
        ii = lax.broadcasted_iota(jnp.int32, (C, C), 0)
        jj = lax.broadcasted_iota(jnp.int32, (C, C), 1)
        rel = (ii - jj).astype(F32)
        decay = jnp.where(rel >= 0, jnp.exp(ldf * jnp.maximum(rel, 0.0)), jnp.exp(ldb * jnp.maximum(-rel, 0.0)))

        for c in range(CPB):
            rows = pl.ds(c * C, C)
            qb = q_ref[rows, :]
            kb = k_ref[rows, :]
            v = v_ref[rows, :]
            q = qb.astype(F32)
            k = kb.astype(F32)
            state = sf_ref[...]
            scores = lax.dot_general(qb, kb, (((1,), (1,)), ((), ())), preferred_element_type=F32) * decay
            goff = pl.multiple_of((s * CPB + c) * C, C)
            o = (jnp.dot(scores.astype(BF16), v, preferred_element_type=F32)
                 + jnp.dot((q * q_dec).astype(BF16), state.astype(BF16), preferred_element_type=F32)
                 + obuf_ref[pl.ds(goff, C), :])
            sf_ref[...] = state * chunk_dec + lax.dot_general((k * k_dec).astype(BF16), v, tdims,
                                                              preferred_element_type=F32)
            mu = jnp.mean(o, axis=-1, keepdims=True)
            var = jnp.mean(jnp.square(o - mu), axis=-1, keepdims=True)
            on = (o - mu) * lax.rsqrt(var + LN_EPS)
            o_ref[rows, :] = (_silu(g_ref[rows, :]) * on).astype(o_ref.dtype)


def _retention(qk, v, g, raw_fwd, raw_bwd, batch, tokens_per_batch, ctx_len):
    n_tok, qk2 = qk.shape
    v_dim = v.shape[1]
    heads = RET_HEADS
    dk = qk2 // 2 // heads
    dv = v_dim // heads
    tb = next(t for t in range(RET_BLOCK, 0, -RET_CHUNK) if tokens_per_batch % t == 0)
    assert tb % RET_CHUNK == 0 and ctx_len % RET_CHUNK == 0 and ctx_len <= tb
    nb = tokens_per_batch // tb
    steps = 2 * (nb + 1)

    def blk(st):
        phase = st // (nb + 1)
        s = st % (nb + 1)
        back = jnp.where(s == 0, 0, nb - s)
        return jnp.where(phase == 0, back, jnp.minimum(s, nb - 1))

    def fwd_blk(st):
        phase = st // (nb + 1)
        s = st % (nb + 1)
        return jnp.where(phase == 0, 0, jnp.minimum(s, nb - 1))

    dec = lambda raw: jnp.broadcast_to(raw.astype(F32)[:, None, None], (heads, 1, LANES))
    return pl.pallas_call(
        functools.partial(_ret_kernel, n_blocks=nb, chunks_per_block=tb // RET_CHUNK,
                          ctx_chunks=ctx_len // RET_CHUNK),
        grid=(batch, heads, steps),
        in_specs=[pl.BlockSpec((tb, dk), lambda b, h, st: (b * nb + blk(st), h)),
                  pl.BlockSpec((tb, dk), lambda b, h, st: (b * nb + blk(st), heads + h)),
                  pl.BlockSpec((tb, dv), lambda b, h, st: (b * nb + blk(st), h)),
                  pl.BlockSpec((tb, dv), lambda b, h, st: (b * nb + fwd_blk(st), h)),
                  pl.BlockSpec((1, 1, LANES), lambda b, h, st: (h, 0, 0)),
                  pl.BlockSpec((1, 1, LANES), lambda b, h, st: (h, 0, 0))],
        out_specs=pl.BlockSpec((tb, dv), lambda b, h, st: (b * nb + fwd_blk(st), h)),
        out_shape=jax.ShapeDtypeStruct((n_tok, v_dim), BF16),
        scratch_shapes=[pltpu.VMEM((dk, dv), F32), pltpu.VMEM((dk, dv), F32),
                        pltpu.VMEM((tokens_per_batch, dv), F32)],
        compiler_params=pltpu.CompilerParams(vmem_limit_bytes=VMEM_LIMIT),
    )(qk, qk, v, g, dec(raw_fwd), dec(raw_bwd))


def _outproj_kernel(o_ref, x_ref, c_ref, w_ref, g1_ref, sc2_ref, sh2_ref, lng_ref, lnb_ref,
                    x1_ref, h2_ref, hp_ref, *, order):
    i = pl.program_id(0)
    g = order.group(i)
    _, _, is_ctx = order.split(i)
    y = jnp.dot(o_ref[...], w_ref[...], preferred_element_type=F32)
    xin = jnp.where(is_ctx, c_ref[...], x_ref[...])
    r = DEEPNORM_ALPHA * xin + g1_ref[0, pl.ds(g, 1), :] * y
    x1 = _layer_norm(r, lng_ref[...], lnb_ref[...])
    x1_ref[...] = x1
    h2 = x1 * (1.0 + sc2_ref[0, pl.ds(g, 1), :]) + sh2_ref[0, pl.ds(g, 1), :]
    h2_ref[...] = h2
    hp_ref[...] = _to_token_rows(h2).astype(BF16)


def _outproj(og, x2d, ctx2d, w_out, mod, ln_g, ln_b, order):
    v_dim, d = w_out.shape
    tm = order.tile
    s_rows = d // LANES
    row = pl.BlockSpec((1, d), lambda i: (0, 0))
    tok = pl.BlockSpec((tm, d), lambda i: (i, 0))
    return pl.pallas_call(
        functools.partial(_outproj_kernel, order=order),
        grid=(order.n_tiles,),
        in_specs=[pl.BlockSpec((tm, v_dim), lambda i: (i, 0)),
                  pl.BlockSpec((tm, d), lambda i: (order.lat_block(i), 0)),
                  pl.BlockSpec((tm, d), lambda i: (order.ctx_block(i), 0)),
                  pl.BlockSpec((v_dim, d), lambda i: (0, 0), pipeline_mode=pl.Buffered(1)),
                  _mod_spec(0, MOD_G1, d, 1), _mod_spec(0, MOD_SC2, d, 1), _mod_spec(0, MOD_SH2, d, 1),
                  row, row],
        out_specs=[tok, tok, pl.BlockSpec((tm * s_rows, LANES), lambda i: (i, 0))],
        out_shape=[jax.ShapeDtypeStruct((order.n_tok, d), F32)] * 2
                  + [jax.ShapeDtypeStruct((order.n_tok * s_rows, LANES), BF16)],
        compiler_params=pltpu.CompilerParams(vmem_limit_bytes=VMEM_LIMIT),
    )(og, x2d, ctx2d, w_out, mod, mod, mod, ln_g.reshape(1, d), ln_b.reshape(1, d))


def _pool_kernel(x_ref, pw_ref, ps_ref, sc1_ref, sh1_ref, g1_ref, sc2_ref, sh2_ref, lng_ref, lnb_ref,
                 x1_ref, h2_ref, hp_ref, y_ref, *, lat_tiles):
    i = pl.program_id(0)
    g = i // lat_tiles
    x = x_ref[...]
    tm, d = x.shape
    gd = d // len(POOL_WINDOWS)
    h = x * (1.0 + sc1_ref[0, pl.ds(g, 1), :]) + sh1_ref[0, pl.ds(g, 1), :]
    ni = lax.broadcasted_iota(jnp.int32, (tm, tm), 0)
    mi = lax.broadcasted_iota(jnp.int32, (tm, tm), 1)
    same_row = (ni // GRID_W) == (mi // GRID_W)
    delta = mi - ni
    col = lax.broadcasted_iota(jnp.int32, (tm, 1), 0) % GRID_W
    for gi, w in enumerate(POOL_WINDOWS):
        window = jnp.where(same_row & (delta >= -(w // 2)) & (delta < w // 2), 1.0, 0.0).astype(BF16)
        count = (jnp.minimum(col + w // 2, GRID_W) - jnp.maximum(col - w // 2, 0)).astype(F32)
        hg = h[:, gi * gd:(gi + 1) * gd]
        h_hi = hg.astype(BF16)
        r1 = hg - h_hi.astype(F32)
        h_mid = r1.astype(BF16)
        h_lo = (r1 - h_mid.astype(F32)).astype(BF16)
        total = (jnp.dot(window, h_hi, preferred_element_type=F32)
                 + jnp.dot(window, h_mid, preferred_element_type=F32)
                 + jnp.dot(window, h_lo, preferred_element_type=F32))
        p = total / count - hg
        y_ref[:, gi * gd:(gi + 1) * gd] = jnp.dot(p.astype(BF16), pw_ref[gi], preferred_element_type=F32)
    y = y_ref[...] * ps_ref[...]
    r = DEEPNORM_ALPHA * x + g1_ref[0, pl.ds(g, 1), :] * y
    x1 = _layer_norm(r, lng_ref[...], lnb_ref[...])
    x1_ref[...] = x1
    h2 = x1 * (1.0 + sc2_ref[0, pl.ds(g, 1), :]) + sh2_ref[0, pl.ds(g, 1), :]
    h2_ref[...] = h2
    hp_ref[...] = _to_token_rows(h2).astype(BF16)


def _pool_layer(x_prev, order0, pool_w, pool_scale, mod, ln_g, ln_b, layer, batch, seq):
    d = x_prev.shape[1]
    tm = order0.tile
    s_rows = d // LANES
    lat_tiles = seq // tm
    n_tiles = batch * lat_tiles
    ng, gd, _ = pool_w.shape

    def src(i):
        b = i // lat_tiles
        return (b * order0.per_batch + order0.ctx_tiles + i % lat_tiles, 0)

    row = pl.BlockSpec((1, d), lambda i: (0, 0))
    tok = pl.BlockSpec((tm, d), lambda i: (i, 0))
    ms = lambda chunk: _mod_spec(layer, chunk, d, 1)
    return pl.pallas_call(
        functools.partial(_pool_kernel, lat_tiles=lat_tiles),
        grid=(n_tiles,),
        in_specs=[pl.BlockSpec((tm, d), src),
                  pl.BlockSpec((ng, gd, gd), lambda i: (0, 0, 0)),
                  row, ms(MOD_SC1), ms(MOD_SH1), ms(MOD_G1), ms(MOD_SC2), ms(MOD_SH2), row, row],
        out_specs=[tok, tok, pl.BlockSpec((tm * s_rows, LANES), lambda i: (i, 0))],
        out_shape=[jax.ShapeDtypeStruct((n_tiles * tm, d), F32)] * 2
                  + [jax.ShapeDtypeStruct((n_tiles * tm * s_rows, LANES), BF16)],
        scratch_shapes=[pltpu.VMEM((tm, d), F32)],
        compiler_params=pltpu.CompilerParams(vmem_limit_bytes=VMEM_LIMIT),
    )(x_prev, pool_w.astype(BF16), pool_scale.reshape(1, d), mod, mod, mod, mod, mod,
      ln_g.reshape(1, d), ln_b.reshape(1, d))


def _split_bf16(x):
    hi = x.astype(BF16)
    return hi, (x - hi.astype(F32)).astype(BF16)


def _route_logits(h_hi, h_mid, w_ref):
    nt = (((1,), (1,)), ((), ()))
    w_hi, w_mid = _split_bf16(w_ref[...])
    return (lax.dot_general(w_hi, h_hi, nt, preferred_element_type=F32)
            + (lax.dot_general(w_hi, h_mid, nt, preferred_element_type=F32)
               + lax.dot_general(w_mid, h_hi, nt, preferred_element_type=F32)))


def _route_tile(logits, b_ref, eidx_ref, gate_ref, rank_ref, cnt_ref, carry_ref):
    i = pl.program_id(0)
    n_exp, tr = logits.shape
    per = n_exp // N_EXPERT_GROUPS
    neg = -jnp.inf

    @pl.when(i == 0)
    def _():
        carry_ref[...] = jnp.zeros_like(carry_ref)

    scores = jax.nn.sigmoid(logits)
    biased = scores + b_ref[...]

    sub_idx = lax.broadcasted_iota(jnp.int32, (per, tr), 0)
    group_score = []
    for gi in range(N_EXPERT_GROUPS):
        sub = biased[gi * per:(gi + 1) * per, :]
        m1 = jnp.max(sub, axis=0, keepdims=True)
        first = jnp.min(jnp.where(sub == m1, sub_idx, per), axis=0, keepdims=True)
        m2 = jnp.max(jnp.where(sub_idx == first, neg, sub), axis=0, keepdims=True)
        group_score.append(m1 + m2)
    masked = []
    for gi in range(N_EXPERT_GROUPS):
        ahead = jnp.zeros((1, tr), jnp.int32)
        for gj in range(N_EXPERT_GROUPS):
            if gj == gi:
                continue
            wins = (group_score[gj] >= group_score[gi]) if gj < gi else (group_score[gj] > group_score[gi])
            ahead = ahead + wins.astype(jnp.int32)
        keep = ahead < TOPK_GROUPS
        masked.append(jnp.where(keep, biased[gi * per:(gi + 1) * per, :], neg))
    masked = jnp.concatenate(masked, axis=0)

    eid = lax.broadcasted_iota(jnp.int32, (n_exp, tr), 0)
    picks, gates = [], []
    onehot = jnp.zeros((n_exp, tr), F32)
    for _ in range(TOP_K):
        m = jnp.max(masked, axis=0, keepdims=True)
        first = jnp.min(jnp.where(masked == m, eid, n_exp), axis=0, keepdims=True)
        sel = eid == first
        gates.append(jnp.sum(jnp.where(sel, scores, 0.0), axis=0, keepdims=True))
        picks.append(first)
        masked = jnp.where(sel, neg, masked)
        onehot = onehot + sel.astype(F32)
    gate_sum = gates[0]
    for gk in gates[1:]:
        gate_sum = gate_sum + gk

    si = lax.broadcasted_iota(jnp.int32, (tr, tr), 0)
    ti = lax.broadcasted_iota(jnp.int32, (tr, tr), 1)
    before = jnp.where(si < ti, 1.0, 0.0).astype(BF16)
    prefix = jnp.dot(onehot.astype(BF16), before, preferred_element_type=F32) + carry_ref[:, :1]
    for k in range(TOP_K):
        eidx_ref[k:k + 1, :] = picks[k]
        gate_ref[k:k + 1, :] = gates[k] / gate_sum * ROUTED_SCALE
        rank_ref[k:k + 1, :] = jnp.sum(jnp.where(eid == picks[k], prefix, 0.0), axis=0,
                                       keepdims=True).astype(jnp.int32)
    carry_ref[...] = carry_ref[...] + jnp.sum(onehot, axis=1, keepdims=True)
    cnt_ref[...] = carry_ref[...]


def _router_kernel(h_ref, w_ref, b_ref, sg_ref, su_ref, sd_ref, eidx_ref, gate_ref, rank_ref, cnt_ref, fsh_ref,
                   carry_ref):
    hb, h_mid = _split_bf16(h_ref[...])
    logits = _route_logits(hb, h_mid, w_ref)
    a = (_silu(jnp.dot(hb, sg_ref[...], preferred_element_type=F32))
         * jnp.dot(hb, su_ref[...], preferred_element_type=F32))
    fsh_ref[...] = jnp.dot(a.astype(BF16), sd_ref[...], preferred_element_type=F32)
    _route_tile(logits, b_ref, eidx_ref, gate_ref, rank_ref, cnt_ref, carry_ref)


def _router(h2, router_w, router_b, sg, su, sd):
    n_tok, d = h2.shape
    n_exp = router_w.shape[1]
    ds_ = sg.shape[1]
    tr = ROUTER_TILE
    assert n_tok % tr == 0
    sel = pl.BlockSpec((TOP_K, tr), lambda i: (0, i))
    tok = pl.BlockSpec((tr, d), lambda i: (i, 0))
    return pl.pallas_call(
        _router_kernel,
        grid=(n_tok // tr,),
        in_specs=[tok,
                  pl.BlockSpec((n_exp, d), lambda i: (0, 0)),
                  pl.BlockSpec((n_exp, 1), lambda i: (0, 0)),
                  pl.BlockSpec((d, ds_), lambda i: (0, 0)),
                  pl.BlockSpec((d, ds_), lambda i: (0, 0)),
                  pl.BlockSpec((ds_, d), lambda i: (0, 0))],
        out_specs=[sel, sel, sel, pl.BlockSpec((n_exp, LANES), lambda i: (0, 0)), tok],
        out_shape=[jax.ShapeDtypeStruct((TOP_K, n_tok), jnp.int32),
                   jax.ShapeDtypeStruct((TOP_K, n_tok), F32),
                   jax.ShapeDtypeStruct((TOP_K, n_tok), jnp.int32),
                   jax.ShapeDtypeStruct((n_exp, LANES), F32),
                   jax.ShapeDtypeStruct((n_tok, d), F32)],
        scratch_shapes=[pltpu.VMEM((n_exp, LANES), F32)],
        compiler_params=pltpu.CompilerParams(vmem_limit_bytes=VMEM_LIMIT),
    )(h2, router_w.astype(F32).T, router_b.astype(F32).reshape(n_exp, 1), sg, su, sd)


def _dispatch_kernel(pe_ref, pd_ref, nu_ref, slot_hbm, hp_ref, xs_hbm, idx_ref, zero_ref, isem, zsem, rsem, *,
                     n_steps, n_exp, n_blocks, s_rows, bm):
    i = pl.program_id(0)
    tt = hp_ref.shape[0] // s_rows
    zrows = bm * s_rows
    per_tile = tt * TOP_K

    def idx_copy(tile, buf):
        return pltpu.make_async_copy(slot_hbm.at[tile], idx_ref.at[pl.ds(buf * per_tile, per_tile)], isem.at[buf])

    def zero_copy(row0):
        return pltpu.make_async_copy(zero_ref, xs_hbm.at[pl.ds(pl.multiple_of(row0, s_rows), zrows), :], zsem)

    @pl.when(i == 0)
    def _():
        zero_ref[...] = jnp.zeros_like(zero_ref)
        nu = nu_ref[0]

        def fill(fn):
            def per_expert(e, carry):
                @pl.when(pd_ref[e] > 0)
                def _():
                    fn(zero_copy((pe_ref[e] - bm) * s_rows))
                return carry
            lax.fori_loop(0, n_exp, per_expert, 0)

            def per_tail(b, carry):
                fn(zero_copy(b * zrows))
                return carry
            lax.fori_loop(nu, n_blocks, per_tail, 0)

        fill(lambda cp: cp.start())
        fill(lambda cp: cp.wait())
        idx_copy(0, 0).start()
        idx_copy(0, 0).wait()
        if n_steps > 1:
            idx_copy(1, 1).start()

    buf = i % 2

    @pl.when(i > 0)
    def _():
        idx_copy(i, buf).wait()

    group = 8
    assert tt % group == 0

    def body(gi, carry):
        base = pl.multiple_of(gi * (group * s_rows), group * s_rows)
        ibase = buf * per_tile + gi * (group * TOP_K)
        for u in range(group):
            src = hp_ref.at[pl.ds(base + u * s_rows, s_rows), :]
            for k in range(TOP_K):
                t = idx_ref[ibase + (u * TOP_K + k)]
                pltpu.make_async_copy(src, xs_hbm.at[pl.ds(pl.multiple_of(t * s_rows, s_rows), s_rows), :],
                                      rsem).start(priority=k % 2)
        return carry

    lax.fori_loop(0, tt // group, body, 0)

    @pl.when((i > 0) & (i + 1 < n_steps))
    def _():
        idx_copy(i + 1, 1 - buf).start()

    for _ in range(TOP_K):
        pltpu.make_async_copy(hp_ref, xs_hbm.at[pl.ds(0, tt * s_rows), :], rsem).wait()


def _dispatch(hp, slots, pad_end, padded, n_used, n_slots, bm):
    rows, _ = hp.shape
    n_steps, per_tile = slots.shape
    tt = per_tile // TOP_K
    s_rows = rows // (n_steps * tt)
    n_exp = pad_end.shape[0]
    grid_spec = pltpu.PrefetchScalarGridSpec(
        num_scalar_prefetch=3,
        grid=(n_steps,),
        in_specs=[pl.BlockSpec(memory_space=pl.ANY),
                  pl.BlockSpec((tt * s_rows, LANES), lambda i, pe, pd, nu: (i, 0))],
        out_specs=pl.BlockSpec(memory_space=pl.ANY),
        scratch_shapes=[pltpu.SMEM((2 * per_tile,), jnp.int32),
                        pltpu.VMEM((bm * s_rows, LANES), BF16),
                        pltpu.SemaphoreType.DMA((2,)),
                        pltpu.SemaphoreType.DMA(()),
                        pltpu.SemaphoreType.DMA(())])
    return pl.pallas_call(
        functools.partial(_dispatch_kernel, n_steps=n_steps, n_exp=n_exp, n_blocks=n_slots // bm,
                          s_rows=s_rows, bm=bm),
        grid_spec=grid_spec,
        out_shape=jax.ShapeDtypeStruct((n_slots * s_rows, LANES), BF16),
        compiler_params=pltpu.CompilerParams(vmem_limit_bytes=VMEM_LIMIT, disable_bounds_checks=True),
    )(pad_end, padded, n_used, slots, hp)


def _moe_kernel(be_ref, nu_ref, vb_ref, ne_ref, x_ref, wg_hbm, wu_hbm, wd_hbm, y_ref, wgb, wub, wdb,
                wg_stage, wu_stage, wd_stage, wsem, *, s_rows, layer, n_exp):
    i = pl.program_id(0)
    nu = nu_ref[0]
    parts = MOE_PARTS
    pm = x_ref.shape[0] // s_rows // parts
    rows_of = [slice(p * pm * s_rows, (p + 1) * pm * s_rows) for p in range(parts)]

    def experts(active):
        hidden = []
        for p in active:
            xt = _from_token_rows(x_ref[rows_of[p], :].astype(F32), s_rows)
            gate = up = None
            for c in range(s_rows // 2):
                xc = jnp.concatenate([xt[2 * c], xt[2 * c + 1]], axis=1).astype(BF16)
                rows = slice(2 * c * LANES, (2 * c + 2) * LANES)
                gc = jnp.dot(xc, wgb[rows, :], preferred_element_type=F32)
                uc = jnp.dot(xc, wub[rows, :], preferred_element_type=F32)
                gate = gc if gate is None else gate + gc
                up = uc if up is None else up + uc
            hidden.append((gate, up))
        outs = []
        for gate, up in hidden:
            a = (_silu(gate) * up).astype(BF16)
            outs.append(jnp.dot(a, wdb[...], preferred_element_type=F32))
        for p, y in zip(active, outs):
            y_ref[rows_of[p], :] = _to_token_rows(y).astype(BF16)
        for p in range(parts):
            if p not in active:
                y_ref[rows_of[p], :] = jnp.zeros((pm * s_rows, LANES), BF16)

    def weight_copies(e):
        return [pltpu.make_async_copy(src.at[layer, e], dst, wsem)
                for src, dst in ((wg_hbm, wg_stage), (wu_hbm, wu_stage), (wd_hbm, wd_stage))]

    @pl.when(i < nu)
    def _():
        e = be_ref[i]

        @pl.when(i == 0)
        def _():
            for cp in weight_copies(e):
                cp.start()

        @pl.when((i == 0) | (e != be_ref[jnp.maximum(i - 1, 0)]))
        def _():
            for cp in weight_copies(e):
                cp.wait()
            wgb[...] = wg_stage[...].astype(BF16)
            wub[...] = wu_stage[...].astype(BF16)
            wdb[...] = wd_stage[...].astype(BF16)
            nxt = ne_ref[e]

            @pl.when(nxt < n_exp)
            def _():
                for cp in weight_copies(nxt):
                    cp.start()

        valid = vb_ref[i]
        for n_active in range(1, parts + 1):
            above = (valid > (n_active - 1) * pm) if n_active > 1 else True
            within = (valid <= n_active * pm) if n_active < parts else True

            @pl.when(above & within)
            def _():
                experts(list(range(n_active)))

    @pl.when(i >= nu)
    def _():
        y_ref[...] = jnp.zeros_like(y_ref)


def _moe_experts(xs, blk_expert, n_used, blk_valid, next_expert, wg, wu, wd, layer, bm):
    _, n_exp, d, de = wg.shape
    s_rows = d // LANES
    nb = xs.shape[0] // (bm * s_rows)
    hbm = pl.BlockSpec(memory_space=pl.ANY)
    grid_spec = pltpu.PrefetchScalarGridSpec(
        num_scalar_prefetch=4,
        grid=(nb,),
        in_specs=[pl.BlockSpec((bm * s_rows, LANES), lambda i, be, nu, vb, ne: (jnp.minimum(i, nu[0] - 1), 0)),
                  hbm, hbm, hbm],
        out_specs=pl.BlockSpec((bm * s_rows, LANES), lambda i, be, nu, vb, ne: (i, 0)),
        scratch_shapes=[pltpu.VMEM((d, de), BF16), pltpu.VMEM((d, de), BF16), pltpu.VMEM((de, d), BF16),
                        pltpu.VMEM((d, de), F32), pltpu.VMEM((d, de), F32), pltpu.VMEM((de, d), F32),
                        pltpu.SemaphoreType.DMA(())])
    return pl.pallas_call(
        functools.partial(_moe_kernel, s_rows=s_rows, layer=layer, n_exp=n_exp),
        grid_spec=grid_spec,
        out_shape=jax.ShapeDtypeStruct(xs.shape, BF16),
        compiler_params=pltpu.CompilerParams(vmem_limit_bytes=VMEM_LIMIT),
    )(blk_expert, n_used, blk_valid, next_expert, xs, wg, wu, wd)


def _combine_kernel(slot_hbm, y_hbm, gate_ref, fsh_ref, x1_ref, g2_ref, lng_ref, lnb_ref,
                    out_ref, idx_ref, ybuf, frow_ref, isem, rsem, *, order, n_steps, s_rows):
    i = pl.program_id(0)
    tt, d = x1_ref.shape
    n_rows = TOP_K * tt

    def idx_copy(tile, buf):
        return pltpu.make_async_copy(slot_hbm.at[tile], idx_ref.at[pl.ds(buf * n_rows, n_rows)], isem.at[buf])

    group = 8
    assert tt % group == 0

    def issue_token(buf, r0, u):
        base = pl.multiple_of(r0 * s_rows, group * s_rows)
        ibase = buf * n_rows + r0
        for k in range(TOP_K):
            t = idx_ref[ibase + (k * tt + u)]
            pltpu.make_async_copy(y_hbm.at[pl.ds(pl.multiple_of(t * s_rows, s_rows), s_rows), :],
                                  ybuf.at[buf, pl.ds(base + (k * tt + u) * s_rows, s_rows), :],
                                  rsem.at[buf]).start(priority=k % 2)

    @pl.when(i == 0)
    def _():
        idx_copy(0, 0).start()
        idx_copy(0, 0).wait()

        def first(gi, carry):
            for u in range(group):
                issue_token(0, gi * group, u)
            return carry
        lax.fori_loop(0, tt // group, first, 0)
        if n_steps > 1:
            idx_copy(1, 1).start()

    buf = i % 2
    nxt = 1 - buf
    has_next = i + 1 < n_steps

    @pl.when(has_next)
    def _():
        idx_copy(i + 1, nxt).wait()

    def token_loop(cur, prefetch):
        pltpu.make_async_copy(y_hbm.at[pl.ds(0, n_rows * s_rows), :], ybuf.at[cur], rsem.at[cur]).wait()

        def body(gi, carry):
            r0 = gi * group
            base = pl.multiple_of(r0 * s_rows, group * s_rows)
            gbase = pl.multiple_of(r0 * TOP_K, group * TOP_K)
            for u in range(group):
                if prefetch:
                    issue_token(1 - cur, r0, u)
                gates = gate_ref[pl.ds(gbase + u * TOP_K, TOP_K), :]
                acc = jnp.zeros((s_rows, LANES), F32)
                for k in range(TOP_K):
                    rows = pl.ds(base + (k * tt + u) * s_rows, s_rows)
                    acc = acc + jnp.broadcast_to(gates[k:k + 1, :], (s_rows, LANES)) * ybuf[cur, rows, :].astype(F32)
                frow_ref[pl.ds(base + u * s_rows, s_rows), :] = acc
            return carry
        lax.fori_loop(0, tt // group, body, 0)

    for cur in (0, 1):
        @pl.when((buf == cur) & has_next)
        def _():
            token_loop(cur, True)

        @pl.when((buf == cur) & jnp.logical_not(has_next))
        def _():
            token_loop(cur, False)

    @pl.when(i + 2 < n_steps)
    def _():
        idx_copy(i + 2, buf).start()

    routed = _from_token_rows(frow_ref[...], s_rows)
    f = fsh_ref[...] + jnp.concatenate([routed[s] for s in range(s_rows)], axis=1)
    g = order.group(i)
    r = DEEPNORM_ALPHA * x1_ref[...] + g2_ref[0, pl.ds(g, 1), :] * f
    out_ref[...] = _layer_norm(r, lng_ref[...], lnb_ref[...])


def _combine(slots, y_sorted, gate, fsh, x1, mod, ln_g, ln_b, layer, order):
    n_tok, d = x1.shape
    tt = order.tile
    n_steps = order.n_tiles
    s_rows = d // LANES
    row = pl.BlockSpec((1, d), lambda i: (0, 0))
    tok = pl.BlockSpec((tt, d), lambda i: (i, 0))
    return pl.pallas_call(
        functools.partial(_combine_kernel, order=order, n_steps=n_steps, s_rows=s_rows),
        grid=(n_steps,),
        in_specs=[pl.BlockSpec(memory_space=pl.ANY),
                  pl.BlockSpec(memory_space=pl.ANY),
                  pl.BlockSpec((tt * TOP_K, LANES), lambda i: (i, 0)),
                  tok, tok,
                  _mod_spec(layer, MOD_G2, d, 1), row, row],
        out_specs=tok,
        out_shape=jax.ShapeDtypeStruct((n_tok, d), F32),
        scratch_shapes=[pltpu.SMEM((2 * TOP_K * tt,), jnp.int32),
                        pltpu.VMEM((2, TOP_K * tt * s_rows, LANES), BF16),
                        pltpu.VMEM((tt * s_rows, LANES), F32),
                        pltpu.SemaphoreType.DMA((2,)),
                        pltpu.SemaphoreType.DMA((2,))],
        compiler_params=pltpu.CompilerParams(vmem_limit_bytes=VMEM_LIMIT, disable_bounds_checks=True),
    )(slots, y_sorted, gate, fsh, x1, mod, ln_g.reshape(1, d), ln_b.reshape(1, d))


def _moe_layer(produced, layer, mod, order, router_w, router_b, wg, wu, wd, sg, su, sd, ln_g, ln_b):
    x1, h2, hp = produced
    n_tok, d = h2.shape
    n_exp = router_w.shape[1]
    bm = MOE_BLOCK
    tt = order.tile
    eidx_t, gate_t, rank_t, cnt, fsh = _router(h2, router_w, router_b, sg[layer].astype(BF16),
                                               su[layer].astype(BF16), sd[layer].astype(BF16))

    counts = cnt[:, 0].astype(jnp.int32)
    padded = (counts + bm - 1) // bm * bm
    pad_end = jnp.cumsum(padded).astype(jnp.int32)
    pad_start = pad_end - padded
    experts = jnp.arange(n_exp, dtype=jnp.int32)
    slot_t = rank_t + jnp.sum(jnp.where(eidx_t[:, :, None] == experts, pad_start, 0), axis=-1)
    n_slots = n_tok * TOP_K + n_exp * bm
    nb = n_slots // bm
    n_used = pad_end[-1:] // bm
    blk_expert = jnp.minimum(jnp.sum(pad_end[None, :] <= (jnp.arange(nb, dtype=jnp.int32) * bm)[:, None], axis=1),
                             n_exp - 1).astype(jnp.int32)
    of_block = blk_expert[:, None] == experts
    blk_first = jnp.sum(jnp.where(of_block, pad_start, 0), axis=1) // bm
    blk_valid = jnp.clip(jnp.sum(jnp.where(of_block, counts, 0), axis=1)
                         - (jnp.arange(nb, dtype=jnp.int32) - blk_first) * bm, 0, bm).astype(jnp.int32)
    later = (experts[None, :] > experts[:, None]) & (counts[None, :] > 0)
    next_expert = jnp.min(jnp.where(later, experts[None, :], n_exp), axis=1).astype(jnp.int32)
    n_tiles = n_tok // tt
    dt = next(t for t in range(DISPATCH_TILE, 0, -tt) if n_tok % t == 0)
    slots_tok = slot_t.T.reshape(n_tok // dt, dt * TOP_K)
    slots_k = slot_t.reshape(TOP_K, n_tiles, tt).transpose(1, 0, 2).reshape(n_tiles, TOP_K * tt)

    xs = _dispatch(hp, slots_tok, pad_end, padded, n_used, n_slots, bm)
    ys = _moe_experts(xs, blk_expert, n_used, blk_valid, next_expert, wg, wu, wd, layer, bm)
    gates = jnp.broadcast_to(gate_t.T.reshape(n_tok * TOP_K, 1), (n_tok * TOP_K, LANES))
    return _combine(slots_k, ys, gates, fsh, x1, mod, ln_g[layer], ln_b[layer], layer, order)


def kernel(x, c, ctx, c_ctx, ada_w, ada_b, ln_mix_g, ln_mix_b, ln_ffn_g, ln_ffn_b, ret_w_in, ret_w_out,
           ret_decay_fwd, ret_decay_bwd, pool_w, pool_scale, router_w, router_b, exp_w_gate, exp_w_up,
           exp_w_down, sh_w_gate, sh_w_up, sh_w_down):
    batch, seq, d = x.shape
    ctx_len = ctx.shape[1]
    heads = RET_HEADS
    qk_dim = d
    head_dim = qk_dim // heads
    v_dim = ret_w_out.shape[1]
    assert ada_w.shape[0] == DEPTH == 2 and batch + 1 <= MOD_ROWS

    cvec = jnp.zeros((MOD_ROWS, d), F32).at[:batch].set(c.astype(F32)).at[batch].set(c_ctx.astype(F32))
    mod = _modulation(cvec, ada_w.astype(F32), ada_b.astype(F32))

    x2d = x.reshape(batch * seq, d)
    ctx2d = ctx.reshape(batch * ctx_len, d)
    order0 = _Order(batch, seq, ctx_len, TOKEN_TILE)

    w_in = ret_w_in[0].astype(F32)
    tabs = _rope_tables(seq, head_dim, TOKEN_TILE)
    qk = _inproj(x2d, ctx2d, mod, w_in, 0, 2 * qk_dim, order0, BF16, rope_tabs=tabs, k_col0=qk_dim,
                 head_dim=head_dim)
    v = _inproj(x2d, ctx2d, mod, w_in, 2 * qk_dim, v_dim, order0, BF16)
    g = _inproj(x2d, ctx2d, mod, w_in, 2 * qk_dim + v_dim, v_dim, order0, F32)
    og = _retention(qk, v, g, ret_decay_fwd[0], ret_decay_bwd[0], batch, ctx_len + seq, ctx_len)
    produced = _outproj(og, x2d, ctx2d, ret_w_out[0].astype(BF16), mod, ln_mix_g[0], ln_mix_b[0], order0)
    xa = _moe_layer(produced, 0, mod, order0, router_w[0], router_b[0], exp_w_gate, exp_w_up, exp_w_down,
                    sh_w_gate, sh_w_up, sh_w_down, ln_ffn_g, ln_ffn_b)

    order1 = _Order(batch, seq, 0, TOKEN_TILE)
    produced = _pool_layer(xa, order0, pool_w[0], pool_scale[0], mod, ln_mix_g[1], ln_mix_b[1], 1, batch, seq)
    xb = _moe_layer(produced, 1, mod, order1, router_w[1], router_b[1], exp_w_gate, exp_w_up, exp_w_down,
                    sh_w_gate, sh_w_up, sh_w_down, ln_ffn_g, ln_ffn_b)
    return xb.reshape(batch, seq, d)
```
